```python
import math
import jax, jax.numpy as jnp
from jax import lax
import numpy as np

D_MODEL = 4096
BATCH = 32
SEQ = 256
DEPTH = 2
DEC_BATCH = 4
DEC_SEQ = 1024
PAST_LEN = 256

GRID_W = 64
HEAD_DIM = 128
S5_WIDTH = 2048
S5_GROUP = 16
S5_GROUPS = S5_WIDTH // S5_GROUP
S5_STATE = 64
DT_MIN = 0.001
DT_MAX = 0.1
WIN_HEADS = 16
WIN_KV_HEADS = 4
WIN_GROUP = WIN_HEADS // WIN_KV_HEADS
WIN_RADIUS = 128
WIN_BLOCK = 128
WIN_Q_WIDTH = WIN_HEADS * HEAD_DIM
WIN_KV_WIDTH = WIN_KV_HEADS * HEAD_DIM
MLA_HEADS = 16
MLA_Q_LORA = 1024
MLA_KV_LORA = 512
MLA_NOPE = 128
MLA_ROPE = 64
MLA_V = 128
NA_HEADS = 16
NA_ROWS = 8
NA_COLS = 16
NA_WIDTH = NA_HEADS * HEAD_DIM
D_FF = 11008
CONV_W = 3
Q_BLOCK = 128
ROPE_BASE = 10000.0
EPS = 1e-6
NEG = -1e30
F32 = jnp.float32
EVEN_IN = S5_WIDTH + WIN_Q_WIDTH + 2 * WIN_KV_WIDTH
ODD_IN = MLA_Q_LORA + MLA_KV_LORA + MLA_ROPE + 3 * NA_WIDTH
EVEN_MIX = S5_WIDTH + WIN_Q_WIDTH
ODD_MIX = MLA_HEADS * MLA_V + NA_WIDTH

kernel_name = 'hybrid_diffusion_prefix_step'


def rmsnorm(x, g):
    xf = x.astype(F32)
    y = xf * lax.rsqrt(jnp.mean(xf * xf, axis=-1, keepdims=True) + EPS)
    return (y * g.astype(F32)).astype(x.dtype)


def modulation(cond, ada_w, ada_b):
    m = jax.nn.silu(cond) @ ada_w + ada_b
    return jnp.split(m[:, None, :], 6, axis=-1)


def split_heads(x, n_heads):
    b, L, _ = x.shape
    return x.reshape(b, L, n_heads, -1).transpose(0, 2, 1, 3)


def merge_heads(x):
    b, h, L, d = x.shape
    return x.transpose(0, 2, 1, 3).reshape(b, L, h * d)


def rotate_half(x, ang):
    cos = jnp.cos(ang).astype(x.dtype)
    sin = jnp.sin(ang).astype(x.dtype)
    x1, x2 = jnp.split(x, 2, axis=-1)
    return jnp.concatenate([x1 * cos - x2 * sin, x1 * sin + x2 * cos], axis=-1)


def rope_2d(x):
    L, rot = x.shape[-2], x.shape[-1]
    n_freq = rot // 4
    t = jnp.arange(L)
    row = (t // GRID_W).astype(F32)
    col = (t % GRID_W).astype(F32)
    inv = ROPE_BASE ** (-jnp.arange(n_freq, dtype=F32) / n_freq)
    xr, xc = jnp.split(x, 2, axis=-1)
    return jnp.concatenate([rotate_half(xr, row[:, None] * inv), rotate_half(xc, col[:, None] * inv)], axis=-1)


def attend_dense(q, k, v, sink=None):
    b, hk, r, lq, dk = q.shape
    nb = lq // Q_BLOCK
    scale = dk ** -0.5
    qb = jnp.moveaxis(q.reshape(b, hk, r, nb, Q_BLOCK, dk), 3, 0)

    def one_block(qi):
        s = jnp.einsum('bgrqd,bgkd->bgrqk', qi, k).astype(F32) * scale
        if sink is not None:
            sk = jnp.broadcast_to(sink.astype(F32)[None, :, :, None, None], s.shape[:-1] + (1,))
            p = jax.nn.softmax(jnp.concatenate([s, sk], axis=-1), axis=-1)[..., :-1]
        else:
            p = jax.nn.softmax(s, axis=-1)
        return jnp.einsum('bgrqk,bgkd->bgrqd', p.astype(v.dtype), v)

    o = lax.map(one_block, qb)
    return jnp.moveaxis(o, 0, 3).reshape(b, hk, r, lq, v.shape[-1])


def window_attention_latent(q, k, v, k_ctx, v_ctx, sink):
    b, g, r, L, d = q.shape
    nb = L // WIN_BLOCK
    lc = k_ctx.shape[2]
    pad = ((0, 0), (0, 0), (WIN_RADIUS, WIN_RADIUS), (0, 0))
    kp = jnp.pad(k, pad).reshape(b, g, nb + 2, WIN_BLOCK, d)
    vp = jnp.pad(v, pad).reshape(b, g, nb + 2, WIN_BLOCK, d)
    kb = jnp.concatenate([kp[:, :, 0:nb], kp[:, :, 1:nb + 1], kp[:, :, 2:nb + 2]], axis=3)
    vb = jnp.concatenate([vp[:, :, 0:nb], vp[:, :, 1:nb + 1], vp[:, :, 2:nb + 2]], axis=3)
    qb = q.reshape(b, g, r, nb, WIN_BLOCK, d)
    scale = d ** -0.5
    s_loc = jnp.einsum('bgrnqd,bgnkd->bgrnqk', qb, kb).astype(F32) * scale
    s_ctx = jnp.einsum('bgrnqd,bgcd->bgrnqc', qb, k_ctx).astype(F32) * scale
    blk = jnp.arange(nb)[:, None] * WIN_BLOCK
    qpos = blk + jnp.arange(WIN_BLOCK)[None, :]
    kpos = blk - WIN_RADIUS + jnp.arange(3 * WIN_BLOCK)[None, :]
    valid = ((kpos[:, None, :] >= 0) & (kpos[:, None, :] < L)
             & (jnp.abs(qpos[:, :, None] - kpos[:, None, :]) <= WIN_RADIUS))
    s_loc = jnp.where(valid, s_loc, NEG)
    sk = jnp.broadcast_to(sink.astype(F32)[None, :, :, None, None, None], s_loc.shape[:-1] + (1,))
    p = jax.nn.softmax(jnp.concatenate([s_loc, s_ctx, sk], axis=-1), axis=-1).astype(v.dtype)
    nk = 3 * WIN_BLOCK
    o = (jnp.einsum('bgrnqk,bgnkd->bgrnqd', p[..., :nk], vb)
         + jnp.einsum('bgrnqc,bgcd->bgrnqd', p[..., nk:nk + lc], v_ctx))
    return o.reshape(b, g, r, L, d)


def neighborhood_attention_latent(q, k, v, k_ctx, v_ctx, rpb):
    b, H, L, d = q.shape
    rows = L // GRID_W
    kr = min(NA_ROWS, rows)
    lc = k_ctx.shape[2]
    r_idx = jnp.arange(rows)
    r_start = jnp.clip(r_idx - kr // 2, 0, rows - kr)
    key_rows = r_start[:, None] + jnp.arange(kr)[None, :]
    kg = k.reshape(b, H, rows, GRID_W, d)[:, :, key_rows]
    vg = v.reshape(b, H, rows, GRID_W, d)[:, :, key_rows]
    qg = q.reshape(b, H, rows, GRID_W, d)
    scale = d ** -0.5
    s_nb = jnp.einsum('bhrqd,bhrkwd->bhrqkw', qg, kg).astype(F32) * scale
    s_ctx = jnp.einsum('bhrqd,bhcd->bhrqc', qg, k_ctx).astype(F32) * scale
    col = jnp.arange(GRID_W)
    c_start = jnp.clip(col - NA_COLS // 2, 0, GRID_W - NA_COLS)
    col_valid = (col[None, :] >= c_start[:, None]) & (col[None, :] < c_start[:, None] + NA_COLS)
    off_r = key_rows - r_idx[:, None]
    off_c = jnp.clip(col[None, :] - col[:, None], -(NA_COLS - 1), NA_COLS - 1)
    bias = rpb.astype(F32)[:, off_r[:, None, :, None] + NA_ROWS - 1, off_c[None, :, None, :] + NA_COLS - 1]
    s_nb = jnp.where(col_valid[None, None, None, :, None, :], s_nb + bias[None], NEG)
    nk = kr * GRID_W
    s_all = jnp.concatenate([s_nb.reshape(b, H, rows, GRID_W, nk), s_ctx], axis=-1)
    p = jax.nn.softmax(s_all, axis=-1).astype(v.dtype)
    p_nb = p[..., :nk].reshape(b, H, rows, GRID_W, kr, GRID_W)
    o = (jnp.einsum('bhrqkw,bhrkwd->bhrqd', p_nb, vg)
         + jnp.einsum('bhrqc,bhcd->bhrqd', p[..., nk:nk + lc], v_ctx))
    return o.reshape(b, H, L, d)


def s5_discretize(lam_re, lam_im, log_dt, b_re, b_im):
    dt = jnp.exp(log_dt.astype(F32))[:, None]
    lr, li = lam_re.astype(F32), lam_im.astype(F32)
    mag = jnp.exp(lr * dt)
    ab_re, ab_im = mag * jnp.cos(li * dt), mag * jnp.sin(li * dt)
    den = lr * lr + li * li
    nr, ni = ab_re - 1.0, ab_im
    f_re = ((nr * lr + ni * li) / den)[..., None]
    f_im = ((ni * lr - nr * li) / den)[..., None]
    br, bi = b_re.astype(F32), b_im.astype(F32)
    return ab_re, ab_im, f_re * br - f_im * bi, f_re * bi + f_im * br


def s5_scan(u, ab_re, ab_im, bb_re, bb_im, h0_re, h0_im):
    x_re = jnp.einsum('blgc,gpc->blgp', u, bb_re)
    x_im = jnp.einsum('blgc,gpc->blgp', u, bb_im)
    x_re = x_re.at[:, 0].add(ab_re * h0_re - ab_im * h0_im)
    x_im = x_im.at[:, 0].add(ab_re * h0_im + ab_im * h0_re)
    a_re = jnp.broadcast_to(ab_re, x_re.shape)
    a_im = jnp.broadcast_to(ab_im, x_im.shape)

    def combine(e1, e2):
        a1r, a1i, b1r, b1i = e1
        a2r, a2i, b2r, b2i = e2
        return (a1r * a2r - a1i * a2i, a1r * a2i + a1i * a2r,
                a2r * b1r - a2i * b1i + b2r, a2r * b1i + a2i * b1r + b2i)

    _, _, h_re, h_im = lax.associative_scan(combine, (a_re, a_im, x_re, x_im), axis=1)
    return h_re, h_im


def s5_mixer(u, p, h0_re, h0_im):
    b, L, _ = u.shape
    ug = u.astype(F32).reshape(b, L, S5_GROUPS, S5_GROUP)
    y = p['d_skip'].astype(F32) * ug
    fin_re, fin_im = [], []
    for dr in range(2):
        ab_re, ab_im, bb_re, bb_im = s5_discretize(p['lambda_re'][dr], p['lambda_im'][dr], p['log_dt'][dr],
                                                   p['b_re'][dr], p['b_im'][dr])
        src = ug if dr == 0 else jnp.flip(ug, axis=1)
        h_re, h_im = s5_scan(src, ab_re, ab_im, bb_re, bb_im,
                             h0_re[:, dr].astype(F32), h0_im[:, dr].astype(F32))
        fin_re.append(h_re[:, -1])
        fin_im.append(h_im[:, -1])
        yd = (jnp.einsum('blgp,gcp->blgc', h_re, p['c_re'][dr].astype(F32))
              - jnp.einsum('blgp,gcp->blgc', h_im, p['c_im'][dr].astype(F32)))
        y = y + (yd if dr == 0 else jnp.flip(yd, axis=1))
    y = jax.nn.gelu(y.reshape(b, L, S5_WIDTH)).astype(u.dtype)
    out = y * jax.nn.sigmoid(y @ p['w_glu'] + p['b_glu'])
    return out, jnp.stack(fin_re, axis=1), jnp.stack(fin_im, axis=1)


def even_project(h, p):
    b, L, _ = h.shape
    z = h @ p['w_in']
    u, q, k, v = jnp.split(z, [S5_WIDTH, S5_WIDTH + WIN_Q_WIDTH, S5_WIDTH + WIN_Q_WIDTH + WIN_KV_WIDTH], axis=-1)
    q = q.reshape(b, L, WIN_KV_HEADS, WIN_GROUP, HEAD_DIM).transpose(0, 2, 3, 1, 4)
    return u, q, split_heads(k, WIN_KV_HEADS), split_heads(v, WIN_KV_HEADS)


def even_merge(a_out, o, p):
    b, g, r, L, d = o.shape
    o = o.transpose(0, 3, 1, 2, 4).reshape(b, L, g * r * d)
    return jnp.concatenate([a_out, o], axis=-1) @ p['w_out']


def even_mixer_context(h, p):
    u, q, k, v = even_project(h, p)
    zeros = jnp.zeros((h.shape[0], 2, S5_GROUPS, S5_STATE), F32)
    a_out, s_re, s_im = s5_mixer(u, p, zeros, zeros)
    o = attend_dense(q, k, v, p['sink'].reshape(WIN_KV_HEADS, WIN_GROUP))
    return even_merge(a_out, o, p), (k, v, s_re, s_im)


def even_mixer_latent(h, p, cache):
    k_ctx, v_ctx, s_re, s_im = cache
    u, q, k, v = even_project(h, p)
    a_out, _, _ = s5_mixer(u, p, s_re, s_im)
    o = window_attention_latent(rope_2d(q), rope_2d(k), v, k_ctx, v_ctx,
                                p['sink'].reshape(WIN_KV_HEADS, WIN_GROUP))
    return even_merge(a_out, o, p), None


def odd_project(h, p):
    z = h @ p['w_in']
    s0 = MLA_Q_LORA
    s1 = s0 + MLA_KV_LORA
    s2 = s1 + MLA_ROPE
    s3 = s2 + NA_WIDTH
    s4 = s3 + NA_WIDTH
    cq, ckv, kpe, qd, kd, vd = jnp.split(z, [s0, s1, s2, s3, s4], axis=-1)
    q = split_heads(rmsnorm(cq, p['q_norm']) @ p['w_uq'], MLA_HEADS)
    ckv = rmsnorm(ckv, p['kv_norm'])
    return q, ckv, kpe, split_heads(qd, NA_HEADS), split_heads(kd, NA_HEADS), split_heads(vd, NA_HEADS)


def mla_up(ckv, kpe, p):
    b, L, _ = ckv.shape
    kv = split_heads(ckv @ p['w_ukv'], MLA_HEADS)
    k_nope, v = jnp.split(kv, [MLA_NOPE], axis=-1)
    k = jnp.concatenate([k_nope, jnp.broadcast_to(kpe[:, None], (b, MLA_HEADS, L, MLA_ROPE))], axis=-1)
    return k, v


def odd_merge(oc, od, p):
    return jnp.concatenate([merge_heads(oc), merge_heads(od)], axis=-1) @ p['w_out']


def odd_mixer_context(h, p):
    q, ckv, kpe, qd, kd, vd = odd_project(h, p)
    k, v = mla_up(ckv, kpe, p)
    oc = attend_dense(q[:, :, None], k, v)[:, :, 0]
    od = attend_dense(qd[:, :, None], kd, vd)[:, :, 0]
    return odd_merge(oc, od, p), (ckv, kpe, kd, vd)


def odd_mixer_latent(h, p, cache):
    ckv_ctx, kpe_ctx, kd_ctx, vd_ctx = cache
    q, ckv, kpe, qd, kd, vd = odd_project(h, p)
    q = jnp.concatenate([q[..., :MLA_NOPE], rope_2d(q[..., MLA_NOPE:])], axis=-1)
    k_lat, v_lat = mla_up(ckv, rope_2d(kpe), p)
    k_ctx, v_ctx = mla_up(ckv_ctx, kpe_ctx, p)
    k_all = jnp.concatenate([k_lat, k_ctx], axis=2)
    v_all = jnp.concatenate([v_lat, v_ctx], axis=2)
    oc = attend_dense(q[:, :, None], k_all, v_all)[:, :, 0]
    od = neighborhood_attention_latent(qd, kd, vd, kd_ctx, vd_ctx, p['rpb'])
    return odd_merge(oc, od, p), None


def conv_ffn(h, p):
    u = h @ p['ffn_w_up']
    up = jnp.pad(u, ((0, 0), (1, 1), (0, 0)))
    w = p['ffn_conv_w']
    u = up[:, :-2] * w[0] + up[:, 1:-1] * w[1] + up[:, 2:] * w[2] + p['ffn_conv_b']
    g, val = jnp.split(u, 2, axis=-1)
    return (jax.nn.silu(g) * val) @ p['ffn_w_down']


def trunk_layer(x, cond, p, mixer):
    sh1, sc1, g1, sh2, sc2, g2 = modulation(cond, p['ada_w'], p['ada_b'])
    h = rmsnorm(x, p['norm1']) * (1 + sc1) + sh1
    out, st = mixer(h)
    x = x + g1 * out
    h = rmsnorm(x, p['norm2']) * (1 + sc2) + sh2
    x = x + g2 * conv_ffn(h, p)
    return x, st


def setup_inputs(seed: int = 0) -> dict:
    key = jax.random.key(seed)
    keys = iter(jax.random.split(key, 64))

    def nrm(shape, scale):
        return jax.random.normal(next(keys), shape, F32) * scale

    def gain(n):
        return 1.0 + nrm((n,), 0.05)

    D = D_MODEL
    G, P, C = S5_GROUPS, S5_STATE, S5_GROUP
    return {
        'x_prompt': nrm((BATCH, SEQ, D), 1.0),
        'x_sample': nrm((DEC_BATCH, DEC_SEQ, D), 1.0),
        'cache_l0_k': nrm((DEC_BATCH, WIN_KV_HEADS, PAST_LEN, HEAD_DIM), 1.0),
        'cache_l0_v': nrm((DEC_BATCH, WIN_KV_HEADS, PAST_LEN, HEAD_DIM), 1.0),
        'state_l0_re': nrm((DEC_BATCH, 2, G, P), 0.5),
        'state_l0_im': nrm((DEC_BATCH, 2, G, P), 0.5),
        'cache_l1_ckv': nrm((DEC_BATCH, PAST_LEN, MLA_KV_LORA), 1.0),
        'cache_l1_kpe': nrm((DEC_BATCH, PAST_LEN, MLA_ROPE), 1.0),
        'cache_l1_k': nrm((DEC_BATCH, NA_HEADS, PAST_LEN, HEAD_DIM), 1.0),
        'cache_l1_v': nrm((DEC_BATCH, NA_HEADS, PAST_LEN, HEAD_DIM), 1.0),
        'c': nrm((DEC_BATCH, D), 1.0),
        'c_ctx': nrm((D,), 1.0),
        'l0_ada_w': nrm((D, 6 * D), D ** -0.5),
        'l0_ada_b': nrm((6 * D,), 0.01),
        'l0_norm1': gain(D),
        'l0_norm2': gain(D),
        'l0_w_in': nrm((D, EVEN_IN), D ** -0.5),
        'l0_lambda_re': -0.5 + nrm((2, G, P), 0.01),
        'l0_lambda_im': jnp.pi * jnp.arange(P, dtype=F32) + nrm((2, G, P), 0.01),
        'l0_log_dt': jax.random.uniform(next(keys), (2, G), F32, math.log(DT_MIN), math.log(DT_MAX)),
        'l0_b_re': nrm((2, G, P, C), (2 * C) ** -0.5),
        'l0_b_im': nrm((2, G, P, C), (2 * C) ** -0.5),
        'l0_c_re': nrm((2, G, C, P), P ** -0.5),
        'l0_c_im': nrm((2, G, C, P), P ** -0.5),
        'l0_d_skip': nrm((G, C), 1.0),
        'l0_w_glu': nrm((S5_WIDTH, S5_WIDTH), S5_WIDTH ** -0.5),
        'l0_b_glu': nrm((S5_WIDTH,), 0.01),
        'l0_sink': nrm((WIN_HEADS,), 1.0),
        'l0_w_out': nrm((EVEN_MIX, D), EVEN_MIX ** -0.5),
        'l0_ffn_w_up': nrm((D, 2 * D_FF), D ** -0.5),
        'l0_ffn_conv_w': nrm((CONV_W, 2 * D_FF), 0.5),
        'l0_ffn_conv_b': nrm((2 * D_FF,), 0.01),
        'l0_ffn_w_down': nrm((D_FF, D), D_FF ** -0.5),
        'l1_ada_w': nrm((D, 6 * D), D ** -0.5),
        'l1_ada_b': nrm((6 * D,), 0.01),
        'l1_norm1': gain(D),
        'l1_norm2': gain(D),
        'l1_w_in': nrm((D, ODD_IN), D ** -0.5),
        'l1_q_norm': gain(MLA_Q_LORA),
        'l1_kv_norm': gain(MLA_KV_LORA),
        'l1_w_uq': nrm((MLA_Q_LORA, MLA_HEADS * (MLA_NOPE + MLA_ROPE)), MLA_Q_LORA ** -0.5),
        'l1_w_ukv': nrm((MLA_KV_LORA, MLA_HEADS * (MLA_NOPE + MLA_V)), MLA_KV_LORA ** -0.5),
        'l1_rpb': nrm((NA_HEADS, 2 * NA_ROWS - 1, 2 * NA_COLS - 1), 0.1),
        'l1_w_out': nrm((ODD_MIX, D), ODD_MIX ** -0.5),
        'l1_ffn_w_up': nrm((D, 2 * D_FF), D ** -0.5),
        'l1_ffn_conv_w': nrm((CONV_W, 2 * D_FF), 0.5),
        'l1_ffn_conv_b': nrm((2 * D_FF,), 0.01),
        'l1_ffn_w_down': nrm((D_FF, D), D_FF ** -0.5),
        'final_norm': gain(D),
    }


def reference(x_prompt, x_sample, cache_l0_k, cache_l0_v, state_l0_re, state_l0_im,
              cache_l1_ckv, cache_l1_kpe, cache_l1_k, cache_l1_v, c, c_ctx,
              l0_ada_w, l0_ada_b, l0_norm1, l0_norm2, l0_w_in, l0_lambda_re, l0_lambda_im, l0_log_dt,
              l0_b_re, l0_b_im, l0_c_re, l0_c_im, l0_d_skip, l0_w_glu, l0_b_glu, l0_sink, l0_w_out,
              l0_ffn_w_up, l0_ffn_conv_w, l0_ffn_conv_b, l0_ffn_w_down,
              l1_ada_w, l1_ada_b, l1_norm1, l1_norm2, l1_w_in, l1_q_norm, l1_kv_norm, l1_w_uq, l1_w_ukv,
              l1_rpb, l1_w_out, l1_ffn_w_up, l1_ffn_conv_w, l1_ffn_conv_b, l1_ffn_w_down, final_norm):
    layers = [
        dict(ada_w=l0_ada_w, ada_b=l0_ada_b, norm1=l0_norm1, norm2=l0_norm2, w_in=l0_w_in,
             lambda_re=l0_lambda_re, lambda_im=l0_lambda_im, log_dt=l0_log_dt, b_re=l0_b_re, b_im=l0_b_im,
             c_re=l0_c_re, c_im=l0_c_im, d_skip=l0_d_skip, w_glu=l0_w_glu, b_glu=l0_b_glu, sink=l0_sink,
             w_out=l0_w_out, ffn_w_up=l0_ffn_w_up, ffn_conv_w=l0_ffn_conv_w, ffn_conv_b=l0_ffn_conv_b,
             ffn_w_down=l0_ffn_w_down),
        dict(ada_w=l1_ada_w, ada_b=l1_ada_b, norm1=l1_norm1, norm2=l1_norm2, w_in=l1_w_in,
             q_norm=l1_q_norm, kv_norm=l1_kv_norm, w_uq=l1_w_uq, w_ukv=l1_w_ukv, rpb=l1_rpb,
             w_out=l1_w_out, ffn_w_up=l1_ffn_w_up, ffn_conv_w=l1_ffn_conv_w, ffn_conv_b=l1_ffn_conv_b,
             ffn_w_down=l1_ffn_w_down),
    ]
    caches = [(cache_l0_k, cache_l0_v, state_l0_re, state_l0_im),
              (cache_l1_ckv, cache_l1_kpe, cache_l1_k, cache_l1_v)]
    ctx_mixers = [even_mixer_context, odd_mixer_context]
    lat_mixers = [even_mixer_latent, odd_mixer_latent]
    xp, xs = x_prompt, x_sample
    new_state = []
    for layer in range(DEPTH):
        p = layers[layer]
        kind = layer % 2
        xp, st = trunk_layer(xp, c_ctx[None, :], p, lambda h: ctx_mixers[kind](h, p))
        new_state.append(st)
        xs, _ = trunk_layer(xs, c, p, lambda h: lat_mixers[kind](h, p, caches[layer]))
    y_prompt = rmsnorm(xp, final_norm)
    y_sample = rmsnorm(xs, final_norm)
    return (y_prompt, y_sample,
            new_state[0][0], new_state[0][1], new_state[0][2], new_state[0][3],
            new_state[1][0], new_state[1][1], new_state[1][2], new_state[1][3])
```

```python
import functools
import math

import jax
import jax.numpy as jnp
from jax import lax
from jax.experimental import pallas as pl
from jax.experimental.pallas import tpu as pltpu

F32 = jnp.float32
BF16 = jnp.bfloat16

D_MODEL = 4096
GRID_W = 64
HEAD_DIM = 128
S5_WIDTH = 2048
S5_GROUP = 16
S5_GROUPS = S5_WIDTH // S5_GROUP
S5_STATE = 64
S5_CHUNK = 16
WIN_HEADS = 16
WIN_KV_HEADS = 4
WIN_GROUP = WIN_HEADS // WIN_KV_HEADS
WIN_RADIUS = 128
WIN_Q_WIDTH = WIN_HEADS * HEAD_DIM
WIN_KV_WIDTH = WIN_KV_HEADS * HEAD_DIM
MLA_HEADS = 16
MLA_Q_LORA = 1024
MLA_KV_LORA = 512
MLA_NOPE = 128
MLA_ROPE = 64
MLA_V = 128
NA_HEADS = 16
NA_ROWS = 8
NA_COLS = 16
NA_WIDTH = NA_HEADS * HEAD_DIM
D_FF = 11008
ROPE_BASE = 10000.0
EPS = 1e-6
NEG = -1e30

V7X_VMEM_BYTES = 64 * 1024 * 1024
LANE = 128
MAX_CONDS = 8


def _params(sem, vmem_mb):
    return pltpu.CompilerParams(dimension_semantics=sem, vmem_limit_bytes=vmem_mb * 1024 * 1024)


def _cond_index(i, bm, n_prompt_rows, lat_len):
    first = n_prompt_rows // bm
    per = lat_len // bm
    return jnp.where(i < first, 0, 1 + (i - first) // per)


def _dot(a, b):
    return jnp.dot(a, b, preferred_element_type=F32)


def _dot_t(a, b):
    return lax.dot_general(a, b, (((1,), (1,)), ((), ())), preferred_element_type=F32)


def _mod_kernel(c_ref, w_ref, b_ref, o_ref):
    c = c_ref[...]
    s = c * (1.0 / (1.0 + jnp.exp(-c)))
    o_ref[...] = _dot(s.astype(BF16), w_ref[...].astype(BF16)) + b_ref[...]


def _modulation(cond, w, b):
    d, n = w.shape
    bn = 512
    out = pl.pallas_call(
        _mod_kernel,
        grid=(n // bn,),
        in_specs=[pl.BlockSpec((MAX_CONDS, d), lambda j: (0, 0)),
                  pl.BlockSpec((d, bn), lambda j: (0, j)),
                  pl.BlockSpec((1, bn), lambda j: (0, j))],
        out_specs=pl.BlockSpec((MAX_CONDS, bn), lambda j: (0, j)),
        out_shape=jax.ShapeDtypeStruct((MAX_CONDS, n), F32),
        compiler_params=_params(("arbitrary",), 40),
        name="modulation",
    )(cond, w, b.reshape(1, n))
    return out.reshape(MAX_CONDS, 1, n)


def _norm_mod_kernel(x_ref, g_ref, sh_ref, sc_ref, o_ref):
    x = x_ref[...]
    ms = jnp.mean(x * x, axis=-1, keepdims=True)
    y = x * lax.rsqrt(ms + EPS) * g_ref[...]
    o_ref[...] = (y * (1.0 + sc_ref[...]) + sh_ref[...]).astype(o_ref.dtype)


def _norm_mod(x, gain, mod, shift_slot, scale_slot, n_prompt_rows, lat_len):
    t, d = x.shape
    bm = 256
    cidx = functools.partial(_cond_index, bm=bm, n_prompt_rows=n_prompt_rows, lat_len=lat_len)
    return pl.pallas_call(
        _norm_mod_kernel,
        grid=(t // bm,),
        in_specs=[pl.BlockSpec((bm, d), lambda i: (i, 0)),
                  pl.BlockSpec((1, d), lambda i: (0, 0)),
                  pl.BlockSpec((None, 1, d), lambda i: (cidx(i), 0, shift_slot)),
                  pl.BlockSpec((None, 1, d), lambda i: (cidx(i), 0, scale_slot))],
        out_specs=pl.BlockSpec((bm, d), lambda i: (i, 0)),
        out_shape=jax.ShapeDtypeStruct((t, d), BF16),
        compiler_params=_params(("arbitrary",), 40),
        name="norm_mod",
    )(x, gain.reshape(1, d), mod, mod)


def _rmsnorm_kernel(x_ref, g_ref, o_ref):
    x = x_ref[...].astype(F32)
    ms = jnp.mean(x * x, axis=-1, keepdims=True)
    o_ref[...] = (x * lax.rsqrt(ms + EPS) * g_ref[...]).astype(o_ref.dtype)


def _rmsnorm_cols(x, gain, col_block, width, out_dtype, row0=0, rows=None):
    bm = 256
    rows = x.shape[0] if rows is None else rows
    rb0 = row0 // bm
    return pl.pallas_call(
        _rmsnorm_kernel,
        grid=(rows // bm,),
        in_specs=[pl.BlockSpec((bm, width), lambda i: (i + rb0, col_block)),
                  pl.BlockSpec((1, width), lambda i: (0, 0))],
        out_specs=pl.BlockSpec((bm, width), lambda i: (i, 0)),
        out_shape=jax.ShapeDtypeStruct((rows, width), out_dtype),
        compiler_params=_params(("arbitrary",), 40),
        name="rmsnorm",
    )(x, gain.reshape(1, width))


def _mm_kernel(x_ref, w_ref, o_ref):
    o_ref[...] = _dot(x_ref[...].astype(BF16), w_ref[...]).astype(o_ref.dtype)


def _matmul(x, w, out_dtype, bm, bn, name):
    m, k = x.shape
    n = w.shape[1]
    bm = min(bm, m)
    assert m % bm == 0 and n % bn == 0
    return pl.pallas_call(
        _mm_kernel,
        grid=(m // bm, n // bn),
        in_specs=[pl.BlockSpec((bm, k), lambda i, j: (i, 0)),
                  pl.BlockSpec((k, bn), lambda i, j: (0, j))],
        out_specs=pl.BlockSpec((bm, bn), lambda i, j: (i, j)),
        out_shape=jax.ShapeDtypeStruct((m, n), out_dtype),
        compiler_params=_params(("arbitrary", "arbitrary"), 56),
        name=name,
    )(x, w)


def _mm_res_kernel(x_ref, w_ref, res_ref, gate_ref, o_ref, *scratch, nk):
    part = _dot(x_ref[...], w_ref[...])
    if nk == 1:
        o_ref[...] = res_ref[...] + gate_ref[...] * part
        return
    acc_ref, = scratch
    k = pl.program_id(2)

    @pl.when(k == 0)
    def _():
        acc_ref[...] = part

    @pl.when(jnp.logical_and(k > 0, k < nk - 1))
    def _():
        acc_ref[...] += part

    @pl.when(k == nk - 1)
    def _():
        o_ref[...] = res_ref[...] + gate_ref[...] * (acc_ref[...] + part)


def _matmul_residual(x, w, res, mod, gate_slot, bm, bn, nk, n_prompt_rows, lat_len, name):
    m, kdim = x.shape
    n = w.shape[1]
    bk = kdim // nk
    cidx = functools.partial(_cond_index, bm=bm, n_prompt_rows=n_prompt_rows, lat_len=lat_len)
    gate_col0 = gate_slot * (D_MODEL // bn)
    scratch = [pltpu.VMEM((bm, bn), F32)] if nk > 1 else []
    return pl.pallas_call(
        functools.partial(_mm_res_kernel, nk=nk),
        grid=(m // bm, n // bn, nk),
        in_specs=[pl.BlockSpec((bm, bk), lambda i, j, k: (i, k)),
                  pl.BlockSpec((bk, bn), lambda i, j, k: (k, j)),
                  pl.BlockSpec((bm, bn), lambda i, j, k: (i, j)),
                  pl.BlockSpec((None, 1, bn), lambda i, j, k: (cidx(i), 0, gate_col0 + j))],
        out_specs=pl.BlockSpec((bm, bn), lambda i, j, k: (i, j)),
        out_shape=jax.ShapeDtypeStruct((m, n), F32),
        scratch_shapes=scratch,
        compiler_params=_params(("arbitrary", "arbitrary", "arbitrary"), 56),
        name=name,
    )(x, w, res, mod)


def _glu_kernel(y_ref, w_ref, b_ref, yt_ref, o_ref):
    z = _dot(y_ref[...], w_ref[...]) + b_ref[...]
    o_ref[...] = (yt_ref[...].astype(F32) * (1.0 / (1.0 + jnp.exp(-z)))).astype(o_ref.dtype)


def _glu(y, w, b, bm, bn):
    m, k = y.shape
    n = w.shape[1]
    return pl.pallas_call(
        _glu_kernel,
        grid=(m // bm, n // bn),
        in_specs=[pl.BlockSpec((bm, k), lambda i, j: (i, 0)),
                  pl.BlockSpec((k, bn), lambda i, j: (0, j)),
                  pl.BlockSpec((1, bn), lambda i, j: (0, j)),
                  pl.BlockSpec((bm, bn), lambda i, j: (i, j))],
        out_specs=pl.BlockSpec((bm, bn), lambda i, j: (i, j)),
        out_shape=jax.ShapeDtypeStruct((m, n), BF16),
        compiler_params=_params(("arbitrary", "arbitrary"), 56),
        name="s5_glu",
    )(y, w, b.reshape(1, n), y)


def _ffn_up_kernel(x_ref, wg_ref, wv_ref, cwg_ref, cwv_ref, cbg_ref, cbv_ref, o_ref, *,
                   bm, prompt_blocks, prompt_len, lat_len):
    i = pl.program_id(0)
    x = x_ref[...]
    period = jnp.where(i < prompt_blocks, prompt_len, lat_len)
    pos = lax.broadcasted_iota(jnp.int32, (bm, 1), 0) & (period - 1)
    first = pos == 0
    last = pos == period - 1

    def conv(u, cw_ref, cb_ref):
        prev = jnp.where(first, 0.0, pltpu.roll(u, 1, 0))
        nxt = jnp.where(last, 0.0, pltpu.roll(u, bm - 1, 0))
        return prev * cw_ref[0:1, :] + u * cw_ref[1:2, :] + nxt * cw_ref[2:3, :] + cb_ref[...]

    g = conv(_dot(x, wg_ref[...]), cwg_ref, cbg_ref)
    v = conv(_dot(x, wv_ref[...]), cwv_ref, cbv_ref)
    o_ref[...] = (g * (1.0 / (1.0 + jnp.exp(-g))) * v).astype(o_ref.dtype)


def _ffn_up(h, w_up, conv_w, conv_b, n_prompt_rows, prompt_len, lat_len):
    t, d = h.shape
    bm, bn = lat_len, 256
    nj = D_FF // bn
    conv_b = conv_b.reshape(1, 2 * D_FF)
    body = functools.partial(_ffn_up_kernel, bm=bm, prompt_blocks=n_prompt_rows // bm,
                             prompt_len=prompt_len, lat_len=lat_len)
    return pl.pallas_call(
        body,
        grid=(t // bm, nj),
        in_specs=[pl.BlockSpec((bm, d), lambda i, j: (i, 0)),
                  pl.BlockSpec((d, bn), lambda i, j: (0, j)),
                  pl.BlockSpec((d, bn), lambda i, j: (0, nj + j)),
                  pl.BlockSpec((3, bn), lambda i, j: (0, j)),
                  pl.BlockSpec((3, bn), lambda i, j: (0, nj + j)),
                  pl.BlockSpec((1, bn), lambda i, j: (0, j)),
                  pl.BlockSpec((1, bn), lambda i, j: (0, nj + j))],
        out_specs=pl.BlockSpec((bm, bn), lambda i, j: (i, j)),
        out_shape=jax.ShapeDtypeStruct((t, D_FF), BF16),
        compiler_params=_params(("arbitrary", "arbitrary"), 56),
        name="ffn_up_conv",
    )(h, w_up, w_up, conv_w, conv_w, conv_b, conv_b)


def _s5_operators(lam_re, lam_im, log_dt, b_re, b_im, c_re, c_im, d_skip):
    hp = lax.Precision.HIGHEST
    q, g, p, c = S5_CHUNK, S5_GROUPS, S5_STATE, S5_GROUP
    dt = jnp.exp(log_dt)[None, :, :, None]
    tau = jnp.arange(q + 1, dtype=F32)[:, None, None, None]
    mag = jnp.exp(lam_re[None] * dt * tau)
    ang = lam_im[None] * dt * tau
    pw_re, pw_im = mag * jnp.cos(ang), mag * jnp.sin(ang)
    dt1 = jnp.exp(log_dt)[:, :, None]
    m1 = jnp.exp(lam_re * dt1)
    ab_re, ab_im = m1 * jnp.cos(lam_im * dt1), m1 * jnp.sin(lam_im * dt1)
    den = lam_re * lam_re + lam_im * lam_im
    nr, ni = ab_re - 1.0, ab_im
    f_re = ((nr * lam_re + ni * lam_im) / den)[..., None]
    f_im = ((ni * lam_re - nr * lam_im) / den)[..., None]
    bb_re = f_re * b_re - f_im * b_im
    bb_im = f_re * b_im + f_im * b_re
    m_re = pw_re[..., None] * bb_re[None] - pw_im[..., None] * bb_im[None]
    m_im = pw_re[..., None] * bb_im[None] + pw_im[..., None] * bb_re[None]
    kk = (jnp.einsum('dgop,tdgpi->tdgio', c_re, m_re, precision=hp)
          - jnp.einsum('dgop,tdgpi->tdgio', c_im, m_im, precision=hp))
    s_idx = jnp.arange(q)[:, None]
    t_idx = jnp.arange(q)[None, :]
    lag_f = t_idx - s_idx
    oh_f = (lag_f[:, :, None] == jnp.arange(q)[None, None, :]).astype(F32)
    oh_b = ((-lag_f)[:, :, None] == jnp.arange(q)[None, None, :]).astype(F32)
    t_f = jnp.einsum('stl,lgio->gsito', oh_f, kk[:q, 0], precision=hp)
    t_b = jnp.einsum('stl,lgio->gsito', oh_b, kk[:q, 1], precision=hp)
    eye_t = jnp.eye(q, dtype=F32)
    eye_c = jnp.eye(c, dtype=F32)
    skip = d_skip[:, None, :, None, None] * eye_t[None, :, None, :, None] * eye_c[None, None, :, None, :]
    t_op = (t_f + t_b + skip).reshape(g, q * c, q * c)
    pf_re, pf_im = pw_re[1:, 0], pw_im[1:, 0]
    pb_re, pb_im = pw_re[1:, 1][::-1], pw_im[1:, 1][::-1]

    def e_op(cr, ci, pr, pi):
        e_r = jnp.einsum('gop,tgp->gpto', cr, pr) - jnp.einsum('gop,tgp->gpto', ci, pi)
        e_i = -(jnp.einsum('gop,tgp->gpto', cr, pi) + jnp.einsum('gop,tgp->gpto', ci, pr))
        return e_r.reshape(g, p, q * c), e_i.reshape(g, p, q * c)

    ef_r, ef_i = e_op(c_re[0], c_im[0], pf_re, pf_im)
    eb_r, eb_i = e_op(c_re[1], c_im[1], pb_re, pb_im)
    zero = jnp.zeros_like(ef_r)
    te = jnp.concatenate([t_op, ef_r, zero, ef_i, zero, zero, eb_r, zero, eb_i], axis=1)
    sf_re = jnp.transpose(m_re[:q, 0][::-1], (1, 0, 3, 2)).reshape(g, q * c, p)
    sf_im = jnp.transpose(m_im[:q, 0][::-1], (1, 0, 3, 2)).reshape(g, q * c, p)
    sb_re = jnp.transpose(m_re[:q, 1], (1, 0, 3, 2)).reshape(g, q * c, p)
    sb_im = jnp.transpose(m_im[:q, 1], (1, 0, 3, 2)).reshape(g, q * c, p)
    sb = jnp.concatenate([sf_re, sb_re, sf_im, sb_im], axis=2)
    a16 = jnp.concatenate([pw_re[q, 0], pw_re[q, 1], pw_im[q, 0], pw_im[q, 1]], axis=-1)[:, None, :]
    return sb.astype(BF16), te.astype(BF16), a16


def _gelu_tanh(x):
    return 0.5 * x * (1.0 + jnp.tanh(math.sqrt(2.0 / math.pi) * (x + 0.044715 * (x * x * x))))


def _s5_kernel(u_ref, sb_ref, te_ref, a_ref, h0_ref, y_ref, hfin_ref, w_scr, h_scr, *, segments):
    u = u_ref[...]
    w_scr[...] = _dot(u, sb_ref[...])
    half = S5_STATE
    ar = a_ref[0:1, 0:2 * half]
    ai = a_ref[0:1, 2 * half:4 * half]
    fwd_lane = lax.broadcasted_iota(jnp.int32, (1, 2 * half), 1) < half
    for row0, nb, nc, h_row0, write_final in segments:
        hr = h0_ref[h_row0:h_row0 + nb, 0:2 * half]
        hi = h0_ref[h_row0:h_row0 + nb, 2 * half:4 * half]
        for i in range(nc):
            ri = row0 + i * nb
            rj = row0 + (nc - 1 - i) * nb
            h_scr[ri:ri + nb, 0:2 * half] = hr
            h_scr[ri:ri + nb, 2 * half:4 * half] = hi
            h_scr[rj:rj + nb, 4 * half:6 * half] = hr
            h_scr[rj:rj + nb, 6 * half:8 * half] = hi
            wr = jnp.where(fwd_lane, w_scr[ri:ri + nb, 0:2 * half], w_scr[rj:rj + nb, 0:2 * half])
            wi = jnp.where(fwd_lane, w_scr[ri:ri + nb, 2 * half:4 * half], w_scr[rj:rj + nb, 2 * half:4 * half])
            hr, hi = ar * hr - ai * hi + wr, ar * hi + ai * hr + wi
        if write_final:
            hfin_ref[:, 0:2 * half] = hr
            hfin_ref[:, 2 * half:4 * half] = hi
    qc = S5_CHUNK * S5_GROUP
    y = _dot(u, te_ref[0:qc, :]) + _dot(h_scr[...].astype(BF16), te_ref[qc:, :])
    y_ref[...] = _gelu_tanh(y).astype(y_ref.dtype)


def _s5_chunked(u_chunks, sb, te, a16, h0, n_prompt, prompt_chunks, n_lat, lat_chunks):
    g, rows, qc = u_chunks.shape
    sw = 4 * S5_STATE
    segments = ((0, n_prompt, prompt_chunks, 0, True),
                (n_prompt * prompt_chunks, n_lat, lat_chunks, n_prompt, False))
    return pl.pallas_call(
        functools.partial(_s5_kernel, segments=segments),
        grid=(g,),
        in_specs=[pl.BlockSpec((None, rows, qc), lambda i: (i, 0, 0)),
                  pl.BlockSpec((None, qc, sw), lambda i: (i, 0, 0)),
                  pl.BlockSpec((None, qc + 2 * sw, qc), lambda i: (i, 0, 0)),
                  pl.BlockSpec((None, 1, sw), lambda i: (i, 0, 0)),
                  pl.BlockSpec((None, n_prompt + n_lat, sw), lambda i: (i, 0, 0))],
        out_specs=[pl.BlockSpec((None, rows, qc), lambda i: (i, 0, 0)),
                   pl.BlockSpec((None, n_prompt, sw), lambda i: (i, 0, 0))],
        out_shape=[jax.ShapeDtypeStruct((g, rows, qc), BF16),
                   jax.ShapeDtypeStruct((g, n_prompt, sw), F32)],
        scratch_shapes=[pltpu.VMEM((rows, sw), F32), pltpu.VMEM((rows, 2 * sw), F32)],
        compiler_params=_params(("arbitrary",), 40),
        name="s5_chunked",
    )(u_chunks, sb, te, a16, h0)


def _rope_tables(length, rot):
    n_freq = rot // 4
    t = jnp.arange(length)
    row = (t // GRID_W).astype(F32)
    col = (t % GRID_W).astype(F32)
    inv = ROPE_BASE ** (-jnp.arange(n_freq, dtype=F32) / n_freq)
    ar, ac = row[:, None] * inv, col[:, None] * inv
    cos = jnp.concatenate([jnp.cos(ar), jnp.cos(ar), jnp.cos(ac), jnp.cos(ac)], axis=-1)
    sin = jnp.concatenate([-jnp.sin(ar), jnp.sin(ar), -jnp.sin(ac), jnp.sin(ac)], axis=-1)
    pad = LANE - rot
    if pad:
        cos = jnp.concatenate([cos, jnp.ones((length, pad), F32)], axis=-1)
        sin = jnp.concatenate([sin, jnp.zeros((length, pad), F32)], axis=-1)
    return cos, sin


def _rope(x, cos, sin, blk):
    lane = lax.broadcasted_iota(jnp.int32, (1, LANE), 1)
    lower = (lane & blk) == 0
    partner = jnp.where(lower, pltpu.roll(x, LANE - blk, 1), pltpu.roll(x, blk, 1))
    return x * cos + partner * sin


def _softmax_pv(scores, values, extra_logit=None):
    m = scores[0].max(axis=-1, keepdims=True)
    for s in scores[1:]:
        m = jnp.maximum(m, s.max(axis=-1, keepdims=True))
    if extra_logit is not None:
        m = jnp.maximum(m, extra_logit)
    den = None
    out = None
    for s, v in zip(scores, values):
        p = jnp.exp(s - m)
        d = p.sum(axis=-1, keepdims=True)
        o = _dot(p.astype(BF16), v)
        den = d if den is None else den + d
        out = o if out is None else out + o
    if extra_logit is not None:
        den = den + jnp.exp(extra_logit - m)
    return out / den


def _ctx_gqa_kernel(sink_ref, q_ref, k_ref, v_ref, o_ref):
    g = pl.program_id(1)
    scale = HEAD_DIM ** -0.5
    k = k_ref[...].astype(BF16)
    v = v_ref[...].astype(BF16)
    for r in range(WIN_GROUP):
        q = q_ref[:, r * HEAD_DIM:(r + 1) * HEAD_DIM].astype(BF16)
        s = _dot_t(q, k) * scale
        o = _softmax_pv([s], [v], sink_ref[g * WIN_GROUP + r])
        o_ref[:, r * HEAD_DIM:(r + 1) * HEAD_DIM] = o.astype(o_ref.dtype)


def _ctx_gqa_attention(z3, sink, n_prompt, length):
    qw = WIN_GROUP * HEAD_DIM
    q0 = S5_WIDTH // qw
    k0 = (S5_WIDTH + WIN_Q_WIDTH) // HEAD_DIM
    v0 = k0 + WIN_KV_HEADS
    return pl.pallas_call(
        _ctx_gqa_kernel,
        grid=(n_prompt, WIN_KV_HEADS),
        in_specs=[pl.BlockSpec(memory_space=pltpu.SMEM),
                  pl.BlockSpec((None, length, qw), lambda b, g: (b, 0, q0 + g)),
                  pl.BlockSpec((None, length, HEAD_DIM), lambda b, g: (b, 0, k0 + g)),
                  pl.BlockSpec((None, length, HEAD_DIM), lambda b, g: (b, 0, v0 + g))],
        out_specs=pl.BlockSpec((None, length, qw), lambda b, g: (b, 0, g)),
        out_shape=jax.ShapeDtypeStruct((n_prompt, length, WIN_Q_WIDTH), BF16),
        compiler_params=_params(("arbitrary", "arbitrary"), 40),
        name="ctx_gqa_attention",
    )(sink, z3, z3, z3)


def _lat_window_kernel(sink_ref, q_ref, k_ref, v_ref, kc_ref, vc_ref, cos_ref, sin_ref, o_ref, *, length):
    g = pl.program_id(1)
    scale = HEAD_DIM ** -0.5
    blk = WIN_RADIUS
    nb = length // blk
    k = _rope(k_ref[...], cos_ref[...], sin_ref[...], HEAD_DIM // 4).astype(BF16)
    v = v_ref[...].astype(BF16)
    kc = kc_ref[...].astype(BF16)
    vc = vc_ref[...].astype(BF16)
    rows = WIN_GROUP * blk
    row = lax.broadcasted_iota(jnp.int32, (rows, 1), 0)
    sk = jnp.zeros((rows, 1), F32)
    for r in range(WIN_GROUP):
        sk = jnp.where(jnp.logical_and(row >= r * blk, row < (r + 1) * blk), sink_ref[g * WIN_GROUP + r], sk)
    qoff = row & (blk - 1)
    for n in range(nb):
        lo = max(0, n - 1) * blk
        hi = min(nb, n + 2) * blk
        cos = cos_ref[n * blk:(n + 1) * blk, :]
        sin = sin_ref[n * blk:(n + 1) * blk, :]
        q = jnp.concatenate(
            [_rope(q_ref[n * blk:(n + 1) * blk, r * HEAD_DIM:(r + 1) * HEAD_DIM], cos, sin, HEAD_DIM // 4)
             for r in range(WIN_GROUP)], axis=0).astype(BF16)
        dist = (n * blk + qoff) - (lo + lax.broadcasted_iota(jnp.int32, (1, hi - lo), 1))
        visible = jnp.logical_and(dist <= WIN_RADIUS, dist >= -WIN_RADIUS)
        s_loc = jnp.where(visible, _dot_t(q, k[lo:hi]) * scale, NEG)
        s_ctx = _dot_t(q, kc) * scale
        o = _softmax_pv([s_loc, s_ctx], [v[lo:hi], vc], sk)
        for r in range(WIN_GROUP):
            o_ref[n * blk:(n + 1) * blk, r * HEAD_DIM:(r + 1) * HEAD_DIM] = o[r * blk:(r + 1) * blk].astype(o_ref.dtype)


def _lat_window_attention(z3, seq0, n_lat, length, k_ctx, v_ctx, sink, cos, sin):
    qw = WIN_GROUP * HEAD_DIM
    q0 = S5_WIDTH // qw
    k0 = (S5_WIDTH + WIN_Q_WIDTH) // HEAD_DIM
    v0 = k0 + WIN_KV_HEADS
    lc = k_ctx.shape[2]
    return pl.pallas_call(
        functools.partial(_lat_window_kernel, length=length),
        grid=(n_lat, WIN_KV_HEADS),
        in_specs=[pl.BlockSpec(memory_space=pltpu.SMEM),
                  pl.BlockSpec((None, length, qw), lambda b, g: (seq0 + b, 0, q0 + g)),
                  pl.BlockSpec((None, length, HEAD_DIM), lambda b, g: (seq0 + b, 0, k0 + g)),
                  pl.BlockSpec((None, length, HEAD_DIM), lambda b, g: (seq0 + b, 0, v0 + g)),
                  pl.BlockSpec((None, None, lc, HEAD_DIM), lambda b, g: (b, g, 0, 0)),
                  pl.BlockSpec((None, None, lc, HEAD_DIM), lambda b, g: (b, g, 0, 0)),
                  pl.BlockSpec((length, LANE), lambda b, g: (0, 0)),
                  pl.BlockSpec((length, LANE), lambda b, g: (0, 0))],
        out_specs=pl.BlockSpec((None, length, qw), lambda b, g: (b, 0, g)),
        out_shape=jax.ShapeDtypeStruct((n_lat, length, WIN_Q_WIDTH), BF16),
        compiler_params=_params(("arbitrary", "arbitrary"), 48),
        name="latent_window_attention",
    )(sink, z3, z3, z3, k_ctx, v_ctx, cos, sin)


def _ctx_odd_kernel(q_ref, kv_ref, kpe_ref, qd_ref, kd_ref, vd_ref, o_ref):
    kw = MLA_NOPE + LANE
    kpe = kpe_ref[...].astype(BF16)
    scale_c = (MLA_NOPE + MLA_ROPE) ** -0.5
    for h in range(MLA_HEADS):
        q = q_ref[:, h * kw:(h + 1) * kw]
        k = jnp.concatenate([kv_ref[:, h * kw:h * kw + MLA_NOPE], kpe], axis=1)
        v = kv_ref[:, h * kw + MLA_NOPE:(h + 1) * kw]
        o = _softmax_pv([_dot_t(q, k) * scale_c], [v])
        o_ref[:, h * MLA_V:(h + 1) * MLA_V] = o.astype(o_ref.dtype)
    scale_d = HEAD_DIM ** -0.5
    base = MLA_HEADS * MLA_V
    for h in range(NA_HEADS):
        sl = slice(h * HEAD_DIM, (h + 1) * HEAD_DIM)
        q = qd_ref[:, sl].astype(BF16)
        k = kd_ref[:, sl].astype(BF16)
        v = vd_ref[:, sl].astype(BF16)
        o = _softmax_pv([_dot_t(q, k) * scale_d], [v])
        o_ref[:, base + h * HEAD_DIM:base + (h + 1) * HEAD_DIM] = o.astype(o_ref.dtype)


def _ctx_odd_attention(q3, kv3, kpe3, z3, n_prompt, length):
    qn = q3.shape[-1]
    return pl.pallas_call(
        _ctx_odd_kernel,
        grid=(n_prompt,),
        in_specs=[pl.BlockSpec((None, length, qn), lambda b: (b, 0, 0)),
                  pl.BlockSpec((None, length, qn), lambda b: (b, 0, 0)),
                  pl.BlockSpec((None, length, LANE), lambda b: (b, 0, 0)),
                  pl.BlockSpec((None, length, NA_WIDTH), lambda b: (b, 0, 0)),
                  pl.BlockSpec((None, length, NA_WIDTH), lambda b: (b, 0, 1)),
                  pl.BlockSpec((None, length, NA_WIDTH), lambda b: (b, 0, 2))],
        out_specs=pl.BlockSpec((None, length, MLA_HEADS * MLA_V + NA_WIDTH), lambda b: (b, 0, 0)),
        out_shape=jax.ShapeDtypeStruct((n_prompt, length, MLA_HEADS * MLA_V + NA_WIDTH), BF16),
        compiler_params=_params(("arbitrary",), 48),
        name="ctx_odd_attention",
    )(q3, kv3, kpe3, z3, z3, z3)


def _lat_mla_kernel(q_ref, kv_ref, kpe_ref, kvc_ref, kpec_ref, cos_ref, sin_ref, o_ref, *, length):
    scale = (MLA_NOPE + MLA_ROPE) ** -0.5
    rb = MLA_ROPE // 4
    kpe = _rope(kpe_ref[...], cos_ref[...], sin_ref[...], rb).astype(BF16)
    k_lat = jnp.concatenate([kv_ref[:, 0:MLA_NOPE], kpe], axis=1)
    v_lat = kv_ref[:, MLA_NOPE:]
    k_ctx = jnp.concatenate([kvc_ref[:, 0:MLA_NOPE], kpec_ref[...].astype(BF16)], axis=1)
    v_ctx = kvc_ref[:, MLA_NOPE:]
    qb = 256
    for n in range(length // qb):
        rows = slice(n * qb, (n + 1) * qb)
        q_pe = _rope(q_ref[rows, MLA_NOPE:].astype(F32), cos_ref[rows, :], sin_ref[rows, :], rb).astype(BF16)
        q = jnp.concatenate([q_ref[rows, 0:MLA_NOPE], q_pe], axis=1)
        o = _softmax_pv([_dot_t(q, k_lat) * scale, _dot_t(q, k_ctx) * scale], [v_lat, v_ctx])
        o_ref[rows, :] = o.astype(o_ref.dtype)


def _lat_mla_attention(q3, kv3, kvc3, kpe3, kpe_ctx, seq0, n_lat, length, cos, sin):
    kw = MLA_NOPE + LANE
    lc = kpe_ctx.shape[1]
    ctx0 = 0
    return pl.pallas_call(
        functools.partial(_lat_mla_kernel, length=length),
        grid=(n_lat, MLA_HEADS),
        in_specs=[pl.BlockSpec((None, length, kw), lambda b, h: (seq0 + b, 0, h)),
                  pl.BlockSpec((None, length, kw), lambda b, h: (seq0 + b, 0, h)),
                  pl.BlockSpec((None, length, LANE), lambda b, h: (seq0 + b, 0, 0)),
                  pl.BlockSpec((None, lc, kw), lambda b, h: (ctx0 + b, 0, h)),
                  pl.BlockSpec((None, lc, LANE), lambda b, h: (b, 0, 0)),
                  pl.BlockSpec((length, LANE), lambda b, h: (0, 0)),
                  pl.BlockSpec((length, LANE), lambda b, h: (0, 0))],
        out_specs=pl.BlockSpec((None, length, MLA_V), lambda b, h: (b, 0, h)),
        out_shape=jax.ShapeDtypeStruct((n_lat, length, MLA_HEADS * MLA_V), BF16),
        compiler_params=_params(("arbitrary", "arbitrary"), 48),
        name="latent_mla_attention",
    )(q3, kv3, kpe3, kvc3, kpe_ctx, cos, sin)


def _na_row_start(r, rows):
    kr = min(NA_ROWS, rows)
    return min(max(r - kr // 2, 0), rows - kr)


def _lat_na_kernel(q_ref, k_ref, v_ref, kc_ref, vc_ref, bias_ref, o_ref, *, length):
    scale = HEAD_DIM ** -0.5
    rows = length // GRID_W
    kr = min(NA_ROWS, rows)
    k = k_ref[...].astype(BF16)
    v = v_ref[...].astype(BF16)
    kc = kc_ref[...].astype(BF16)
    vc = vc_ref[...].astype(BF16)
    for r in range(rows):
        r0 = _na_row_start(r, rows) * GRID_W
        q = q_ref[r * GRID_W:(r + 1) * GRID_W, :].astype(BF16)
        s_nb = _dot_t(q, k[r0:r0 + kr * GRID_W]) * scale + bias_ref[r]
        s_ctx = _dot_t(q, kc) * scale
        o = _softmax_pv([s_nb, s_ctx], [v[r0:r0 + kr * GRID_W], vc])
        o_ref[r * GRID_W:(r + 1) * GRID_W, :] = o.astype(o_ref.dtype)


def _na_bias(rpb, length):
    rows = length // GRID_W
    kr = min(NA_ROWS, rows)
    col = jnp.arange(GRID_W)
    c_start = jnp.clip(col - NA_COLS // 2, 0, GRID_W - NA_COLS)
    col_valid = (col[None, :] >= c_start[:, None]) & (col[None, :] < c_start[:, None] + NA_COLS)
    off_c = jnp.clip(col[None, :] - col[:, None], -(NA_COLS - 1), NA_COLS - 1) + NA_COLS - 1
    onehot = (off_c[:, :, None] == jnp.arange(2 * NA_COLS - 1)[None, None, :]).astype(F32)
    table = jnp.einsum('hdj,qkj->hdqk', rpb.astype(F32), onehot, precision=lax.Precision.HIGHEST)
    table = jnp.where(col_valid[None, None], table, NEG)
    per_row = []
    for r in range(rows):
        r0 = _na_row_start(r, rows)
        per_row.append(jnp.concatenate([table[:, r0 + j - r + NA_ROWS - 1] for j in range(kr)], axis=-1))
    return jnp.stack(per_row, axis=1)


def _lat_na_attention(z3, seq0, n_lat, length, k_ctx, v_ctx, bias):
    rows = length // GRID_W
    kr = min(NA_ROWS, rows)
    lc = k_ctx.shape[2]
    return pl.pallas_call(
        functools.partial(_lat_na_kernel, length=length),
        grid=(NA_HEADS, n_lat),
        in_specs=[pl.BlockSpec((None, length, HEAD_DIM), lambda h, b: (seq0 + b, 0, h)),
                  pl.BlockSpec((None, length, HEAD_DIM), lambda h, b: (seq0 + b, 0, NA_HEADS + h)),
                  pl.BlockSpec((None, length, HEAD_DIM), lambda h, b: (seq0 + b, 0, 2 * NA_HEADS + h)),
                  pl.BlockSpec((None, None, lc, HEAD_DIM), lambda h, b: (b, h, 0, 0)),
                  pl.BlockSpec((None, None, lc, HEAD_DIM), lambda h, b: (b, h, 0, 0)),
                  pl.BlockSpec((None, rows, GRID_W, kr * GRID_W), lambda h, b: (h, 0, 0, 0))],
        out_specs=pl.BlockSpec((None, length, HEAD_DIM), lambda h, b: (b, 0, h)),
        out_shape=jax.ShapeDtypeStruct((n_lat, length, NA_WIDTH), BF16),
        compiler_params=_params(("arbitrary", "arbitrary"), 48),
        name="latent_neighborhood_attention",
    )(z3, z3, z3, k_ctx, v_ctx, bias)


def _split_heads_out(x2d, n_seq, length, heads):
    return x2d.reshape(n_seq, length, heads, -1).transpose(0, 2, 1, 3)


def kernel(x_prompt, x_sample, cache_l0_k, cache_l0_v, state_l0_re, state_l0_im, cache_l1_ckv, cache_l1_kpe, cache_l1_k, cache_l1_v, c, c_ctx, l0_ada_w, l0_ada_b, l0_norm1, l0_norm2, l0_w_in, l0_lambda_re, l0_lambda_im, l0_log_dt, l0_b_re, l0_b_im, l0_c_re, l0_c_im, l0_d_skip, l0_w_glu, l0_b_glu, l0_sink, l0_w_out, l0_ffn_w_up, l0_ffn_conv_w, l0_ffn_conv_b, l0_ffn_w_down, l1_ada_w, l1_ada_b, l1_norm1, l1_norm2, l1_w_in, l1_q_norm, l1_kv_norm, l1_w_uq, l1_w_ukv, l1_rpb, l1_w_out, l1_ffn_w_up, l1_ffn_conv_w, l1_ffn_conv_b, l1_ffn_w_down, final_norm):
    n_p, len_p, d = x_prompt.shape
    n_s, len_s, _ = x_sample.shape
    rows_p, rows_s = n_p * len_p, n_s * len_s
    t = rows_p + rows_s
    assert d == D_MODEL and len_s % len_p == 0 and n_s + 1 <= MAX_CONDS
    assert len_p % S5_CHUNK == 0 and len_s % (2 * WIN_RADIUS) == 0 and len_s % GRID_W == 0
    seqs_p_units = t // len_p
    seqs_s_units = t // len_s
    seq0_s = rows_p // len_s
    groups = dict(n_prompt_rows=rows_p, lat_len=len_s)

    cond = jnp.zeros((MAX_CONDS, d), F32).at[0].set(c_ctx).at[1:1 + n_s].set(c)
    mod0 = _modulation(cond, l0_ada_w, l0_ada_b)
    mod1 = _modulation(cond, l1_ada_w, l1_ada_b)

    x = jnp.concatenate([x_prompt.reshape(rows_p, d), x_sample.reshape(rows_s, d)], axis=0)

    def ffn(x, mod, norm2, w_up, conv_w, conv_b, w_down):
        h = _norm_mod(x, norm2, mod, 3, 4, **groups)
        act = _ffn_up(h, w_up.astype(BF16), conv_w, conv_b, rows_p, len_p, len_s)
        return _matmul_residual(act, w_down.astype(BF16), x, mod, 5, 1024, 512, 2, name="ffn_down", **groups)

    h = _norm_mod(x, l0_norm1, mod0, 0, 1, **groups)
    z = _matmul(h, l0_w_in.astype(BF16), F32, 1024, 512, "l0_in_proj")
    kv0 = S5_WIDTH + WIN_Q_WIDTH
    out_k0 = _split_heads_out(z[:rows_p, kv0:kv0 + WIN_KV_WIDTH], n_p, len_p, WIN_KV_HEADS)
    out_v0 = _split_heads_out(z[:rows_p, kv0 + WIN_KV_WIDTH:], n_p, len_p, WIN_KV_HEADS)

    q16, gc = S5_CHUNK, S5_GROUP
    ch_p, ch_s = len_p // q16, len_s // q16

    np8, ns8 = -(-n_p // 8) * 8, -(-n_s // 8) * 8

    def to_chunks(u2d, nb, nb8, nch):
        u5 = u2d.reshape(nb, nch, q16, S5_GROUPS, gc).transpose(3, 1, 0, 2, 4)
        u5 = jnp.pad(u5, ((0, 0), (0, 0), (0, nb8 - nb), (0, 0), (0, 0)))
        return u5.reshape(S5_GROUPS, nch * nb8, q16 * gc)

    def from_chunks(y3, nb, nb8, nch):
        return (y3.reshape(S5_GROUPS, nch, nb8, q16, gc)[:, :, :nb].transpose(2, 1, 3, 0, 4)
                .reshape(nb * nch * q16, S5_WIDTH))

    u_bf = z[:, :S5_WIDTH].astype(BF16)
    u_chunks = jnp.concatenate([to_chunks(u_bf[:rows_p], n_p, np8, ch_p),
                                to_chunks(u_bf[rows_p:], n_s, ns8, ch_s)], axis=1)
    sb, te, a16 = _s5_operators(l0_lambda_re, l0_lambda_im, l0_log_dt, l0_b_re, l0_b_im, l0_c_re, l0_c_im, l0_d_skip)

    def state_cols(s):
        return s.astype(F32).transpose(2, 0, 1, 3).reshape(S5_GROUPS, s.shape[0], 2 * S5_STATE)

    h0_lat = jnp.concatenate([state_cols(state_l0_re), state_cols(state_l0_im)], axis=-1)
    h0 = jnp.zeros((S5_GROUPS, np8 + ns8, 4 * S5_STATE), F32).at[:, np8:np8 + n_s].set(h0_lat)
    y_chunks, hfin = _s5_chunked(u_chunks, sb, te, a16, h0, np8, ch_p, ns8, ch_s)
    yg = jnp.concatenate([from_chunks(y_chunks[:, :np8 * ch_p], n_p, np8, ch_p),
                          from_chunks(y_chunks[:, np8 * ch_p:], n_s, ns8, ch_s)], axis=0)
    a_out = _glu(yg, l0_w_glu.astype(BF16), l0_b_glu, 1024, 512)

    def state_out(cols):
        return cols[:, :n_p].reshape(S5_GROUPS, n_p, 2, S5_STATE).transpose(1, 2, 0, 3)

    out_sre = state_out(hfin[:, :, :2 * S5_STATE])
    out_sim = state_out(hfin[:, :, 2 * S5_STATE:])

    cos_e, sin_e = _rope_tables(len_s, HEAD_DIM)
    o_p = _ctx_gqa_attention(z.reshape(seqs_p_units, len_p, -1), l0_sink, n_p, len_p)
    o_s = _lat_window_attention(z.reshape(seqs_s_units, len_s, -1), seq0_s, n_s, len_s,
                                cache_l0_k, cache_l0_v, l0_sink, cos_e, sin_e)
    o_all = jnp.concatenate([o_p.reshape(rows_p, -1), o_s.reshape(rows_s, -1)], axis=0)
    mix = jnp.concatenate([a_out, o_all], axis=1)
    x = _matmul_residual(mix, l0_w_out.astype(BF16), x, mod0, 2, 1024, 512, 1, name="l0_out_proj", **groups)
    x = ffn(x, mod0, l0_norm2, l0_ffn_w_up, l0_ffn_conv_w, l0_ffn_conv_b, l0_ffn_w_down)

    s0, s1, s2 = MLA_Q_LORA, MLA_Q_LORA + MLA_KV_LORA, MLA_Q_LORA + MLA_KV_LORA + MLA_ROPE
    w_in1 = jnp.concatenate([l1_w_in[:, s2:], l1_w_in[:, :s1]], axis=1).astype(BF16)
    w_kpe = jnp.pad(l1_w_in[:, s1:s2], ((0, 0), (0, LANE - MLA_ROPE))).astype(BF16)
    w_uq = l1_w_uq.reshape(MLA_Q_LORA, MLA_HEADS, MLA_NOPE + MLA_ROPE)
    w_uq = jnp.pad(w_uq, ((0, 0), (0, 0), (0, LANE - MLA_ROPE))).reshape(MLA_Q_LORA, -1).astype(BF16)

    h = _norm_mod(x, l1_norm1, mod1, 0, 1, **groups)
    z = _matmul(h, w_in1, F32, 1024, 512, "l1_in_proj")
    kpe = _matmul(h, w_kpe, F32, 1024, LANE, "l1_kpe_proj")
    cq_col = 3 * NA_WIDTH // MLA_Q_LORA
    ckv_col = (3 * NA_WIDTH + MLA_Q_LORA) // MLA_KV_LORA
    cqn = _rmsnorm_cols(z, l1_q_norm, cq_col, MLA_Q_LORA, BF16)
    ckvn = _rmsnorm_cols(z, l1_kv_norm, ckv_col, MLA_KV_LORA, F32)
    q_all = _matmul(cqn, w_uq, BF16, 1024, 512, "l1_q_up")
    w_ukv = l1_w_ukv.astype(BF16)
    kv_all = _matmul(ckvn, w_ukv, BF16, 1024, 512, "l1_kv_up")
    lc = cache_l1_ckv.shape[1]
    kv_ctx = _matmul(cache_l1_ckv.reshape(-1, MLA_KV_LORA), w_ukv, BF16, 1024, 512, "l1_kv_up_ctx")

    out_ckv = ckvn[:rows_p].reshape(n_p, len_p, MLA_KV_LORA)
    out_kpe = kpe[:rows_p, :MLA_ROPE].reshape(n_p, len_p, MLA_ROPE)
    out_k1 = _split_heads_out(z[:rows_p, NA_WIDTH:2 * NA_WIDTH], n_p, len_p, NA_HEADS)
    out_v1 = _split_heads_out(z[:rows_p, 2 * NA_WIDTH:3 * NA_WIDTH], n_p, len_p, NA_HEADS)

    o_p = _ctx_odd_attention(q_all.reshape(seqs_p_units, len_p, -1),
                             kv_all.reshape(seqs_p_units, len_p, -1),
                             kpe.reshape(seqs_p_units, len_p, LANE),
                             z.reshape(seqs_p_units, len_p, -1), n_p, len_p)
    cos_o, sin_o = _rope_tables(len_s, MLA_ROPE)
    kpe_ctx = jnp.pad(cache_l1_kpe, ((0, 0), (0, 0), (0, LANE - MLA_ROPE)))
    oc_s = _lat_mla_attention(q_all.reshape(seqs_s_units, len_s, -1),
                              kv_all.reshape(seqs_s_units, len_s, -1),
                              kv_ctx.reshape(n_s, lc, -1),
                              kpe.reshape(seqs_s_units, len_s, LANE), kpe_ctx,
                              seq0_s, n_s, len_s, cos_o, sin_o)
    od_s = _lat_na_attention(z.reshape(seqs_s_units, len_s, -1), seq0_s, n_s, len_s,
                             cache_l1_k, cache_l1_v, _na_bias(l1_rpb, len_s))
    o_s = jnp.concatenate([oc_s, od_s], axis=-1)
    mix = jnp.concatenate([o_p.reshape(rows_p, -1), o_s.reshape(rows_s, -1)], axis=0)
    x = _matmul_residual(mix, l1_w_out.astype(BF16), x, mod1, 2, 1024, 512, 1, name="l1_out_proj", **groups)
    x = ffn(x, mod1, l1_norm2, l1_ffn_w_up, l1_ffn_conv_w, l1_ffn_conv_b, l1_ffn_w_down)

    y_prompt = _rmsnorm_cols(x, final_norm, 0, d, F32, 0, rows_p).reshape(n_p, len_p, d)
    y_sample = _rmsnorm_cols(x, final_norm, 0, d, F32, rows_p, rows_s).reshape(n_s, len_s, d)
    return (y_prompt, y_sample, out_k0, out_v0, out_sre, out_sim, out_ckv, out_kpe, out_k1, out_v1)
```

```python
import functools
import math

import jax
import jax.numpy as jnp
from jax import lax
from jax.experimental import pallas as pl
from jax.experimental.pallas import tpu as pltpu

F32 = jnp.float32
BF16 = jnp.bfloat16

D_MODEL = 4096
GRID_W = 64
HEAD_DIM = 128
S5_WIDTH = 2048
S5_GROUP = 16
S5_GROUPS = S5_WIDTH // S5_GROUP
S5_STATE = 64
S5_CHUNK = 16
WIN_HEADS = 16
WIN_KV_HEADS = 4
WIN_GROUP = WIN_HEADS // WIN_KV_HEADS
WIN_RADIUS = 128
WIN_Q_WIDTH = WIN_HEADS * HEAD_DIM
WIN_KV_WIDTH = WIN_KV_HEADS * HEAD_DIM
MLA_HEADS = 16
MLA_Q_LORA = 1024
MLA_KV_LORA = 512
MLA_NOPE = 128
MLA_ROPE = 64
MLA_V = 128
NA_HEADS = 16
NA_ROWS = 8
NA_COLS = 16
NA_WIDTH = NA_HEADS * HEAD_DIM
D_FF = 11008
ROPE_BASE = 10000.0
EPS = 1e-6
NEG = -1e30

LANE = 128
SUBLANE = 8
MAX_CONDS = 8
ROW_TILE = 1024


def _params(sem, vmem_mb):
    return pltpu.CompilerParams(dimension_semantics=sem, vmem_limit_bytes=vmem_mb * 1024 * 1024)


def _cond_index(i, bm, n_prompt_rows, lat_len):
    first = n_prompt_rows // bm
    per = lat_len // bm
    return jnp.where(i < first, 0, 1 + (i - first) // per)


def _dot(a, b):
    return jnp.dot(a, b, preferred_element_type=F32)


def _dot_t(a, b):
    return lax.dot_general(a, b, (((1,), (1,)), ((), ())), preferred_element_type=F32)


def _sigmoid(x):
    return 1.0 / (1.0 + jnp.exp(-x))


def _row_specs(parts, bm, width, col_fn, n_prompt_blocks, single_buffer=False):
    mode = dict(pipeline_mode=pl.Buffered(1)) if single_buffer else {}
    if len(parts) == 1:
        cb = parts[0][1]
        return [pl.BlockSpec((bm, width), lambda i, j: (i, col_fn(j, cb)), **mode)]
    cb_p, cb_s = parts[0][1], parts[1][1]
    last_p = n_prompt_blocks - 1
    return [pl.BlockSpec((bm, width), lambda i, j: (jnp.minimum(i, last_p), col_fn(j, cb_p)), **mode),
            pl.BlockSpec((bm, width), lambda i, j: (jnp.maximum(i - n_prompt_blocks, 0), col_fn(j, cb_s)), **mode)]


def _by_stream(i, n_prompt_blocks, operands, body):
    if all(len(o) == 1 for o in operands):
        body([o[0] for o in operands])
        return

    @pl.when(i < n_prompt_blocks)
    def _():
        body([o[0] for o in operands])

    @pl.when(i >= n_prompt_blocks)
    def _():
        body([o[-1] for o in operands])


def _mod_kernel(c_ref, w_ref, b_ref, o_ref):
    c = c_ref[...]
    s = c * _sigmoid(c)
    o_ref[...] = _dot(s.astype(BF16), w_ref[...].astype(BF16)) + b_ref[...]


def _modulation(cond, w, b):
    d, n = w.shape
    bn = 512
    out = pl.pallas_call(
        _mod_kernel,
        grid=(n // bn,),
        in_specs=[pl.BlockSpec((MAX_CONDS, d), lambda j: (0, 0)),
                  pl.BlockSpec((d, bn), lambda j: (0, j)),
                  pl.BlockSpec((1, bn), lambda j: (0, j))],
        out_specs=pl.BlockSpec((MAX_CONDS, bn), lambda j: (0, j)),
        out_shape=jax.ShapeDtypeStruct((MAX_CONDS, n), F32),
        compiler_params=_params(("arbitrary",), 40),
        name="modulation",
    )(cond, w, b.reshape(1, n))
    return out.reshape(MAX_CONDS, 1, n)


def _norm_mod_kernel(*refs, n_x, n_prompt_blocks):
    x_refs, (g_ref, sh_ref, sc_ref, o_ref) = refs[:n_x], refs[n_x:]

    def body(r):
        x = r[0][...]
        ms = jnp.mean(x * x, axis=-1, keepdims=True)
        y = x * lax.rsqrt(ms + EPS) * g_ref[...]
        o_ref[...] = (y * (1.0 + sc_ref[...]) + sh_ref[...]).astype(o_ref.dtype)

    _by_stream(pl.program_id(0), n_prompt_blocks, [x_refs], body)


def _norm_mod(x_parts, gain, mod, shift_slot, scale_slot, n_prompt_rows, lat_len):
    d = D_MODEL
    bm = 256
    t = sum(x.shape[0] for x in x_parts)
    npb = n_prompt_rows // bm
    cidx = functools.partial(_cond_index, bm=bm, n_prompt_rows=n_prompt_rows, lat_len=lat_len)
    specs = _row_specs([(x, 0) for x in x_parts], bm, d, lambda j, cb: 0, npb)
    to1d = lambda spec: pl.BlockSpec(spec.block_shape, lambda i, f=spec.index_map: f(i, 0))
    return pl.pallas_call(
        functools.partial(_norm_mod_kernel, n_x=len(x_parts), n_prompt_blocks=npb),
        grid=(t // bm,),
        in_specs=[to1d(s) for s in specs] + [
            pl.BlockSpec((1, d), lambda i: (0, 0)),
            pl.BlockSpec((None, 1, d), lambda i: (cidx(i), 0, shift_slot)),
            pl.BlockSpec((None, 1, d), lambda i: (cidx(i), 0, scale_slot))],
        out_specs=pl.BlockSpec((bm, d), lambda i: (i, 0)),
        out_shape=jax.ShapeDtypeStruct((t, d), BF16),
        compiler_params=_params(("arbitrary",), 40),
        name="norm_mod",
    )(*x_parts, gain.reshape(1, d), mod, mod)


def _rmsnorm_kernel(x_ref, g_ref, o_ref):
    x = x_ref[...].astype(F32)
    ms = jnp.mean(x * x, axis=-1, keepdims=True)
    o_ref[...] = (x * lax.rsqrt(ms + EPS) * g_ref[...]).astype(o_ref.dtype)


def _rmsnorm_cols(x, gain, col_block, width, out_dtype, row0=0, rows=None):
    bm = 256
    rows = x.shape[0] if rows is None else rows
    rb0 = row0 // bm
    return pl.pallas_call(
        _rmsnorm_kernel,
        grid=(rows // bm,),
        in_specs=[pl.BlockSpec((bm, width), lambda i: (i + rb0, col_block)),
                  pl.BlockSpec((1, width), lambda i: (0, 0))],
        out_specs=pl.BlockSpec((bm, width), lambda i: (i, 0)),
        out_shape=jax.ShapeDtypeStruct((rows, width), out_dtype),
        compiler_params=_params(("arbitrary",), 40),
        name="rmsnorm",
    )(x, gain.reshape(1, width))


def _mm_kernel(x_ref, w_ref, o_ref):
    o_ref[...] = _dot(x_ref[...].astype(BF16), w_ref[...].astype(BF16)).astype(o_ref.dtype)


def _matmul(x, w, out_dtype, bn, name, col0=0, ncols=None):
    m, k = x.shape
    ncols = w.shape[1] - col0 if ncols is None else ncols
    bm = min(ROW_TILE, m)
    assert m % bm == 0 and ncols % bn == 0 and col0 % bn == 0
    cb0 = col0 // bn
    return pl.pallas_call(
        _mm_kernel,
        grid=(m // bm, ncols // bn),
        in_specs=[pl.BlockSpec((bm, k), lambda i, j: (i, 0)),
                  pl.BlockSpec((k, bn), lambda i, j: (0, cb0 + j))],
        out_specs=pl.BlockSpec((bm, bn), lambda i, j: (i, j)),
        out_shape=jax.ShapeDtypeStruct((m, ncols), out_dtype),
        compiler_params=_params(("arbitrary", "arbitrary"), 56),
        name=name,
    )(x, w)


def _mm_heads_kernel(x_ref, w_ref, o_ref, *, seqs, seq_len, heads):
    acc = _dot(x_ref[...], w_ref[...].astype(BF16))
    for b in range(seqs):
        for hh in range(heads):
            o_ref[b, hh] = acc[b * seq_len:(b + 1) * seq_len, hh * HEAD_DIM:(hh + 1) * HEAD_DIM]


def _matmul_heads(x, w, col0, heads, row0, rows, seq_len, name):
    k = x.shape[1]
    bm = ROW_TILE
    hb = 4
    bn = hb * HEAD_DIM
    assert rows % bm == 0 and row0 % bm == 0 and bm % seq_len == 0 and heads % hb == 0 and col0 % bn == 0
    seqs = bm // seq_len
    rb0, cb0 = row0 // bm, col0 // bn
    return pl.pallas_call(
        functools.partial(_mm_heads_kernel, seqs=seqs, seq_len=seq_len, heads=hb),
        grid=(rows // bm, heads // hb),
        in_specs=[pl.BlockSpec((bm, k), lambda i, j: (rb0 + i, 0)),
                  pl.BlockSpec((k, bn), lambda i, j: (0, cb0 + j))],
        out_specs=pl.BlockSpec((seqs, hb, seq_len, HEAD_DIM), lambda i, j: (i, j, 0, 0)),
        out_shape=jax.ShapeDtypeStruct((rows // seq_len, heads, seq_len, HEAD_DIM), F32),
        compiler_params=_params(("arbitrary", "arbitrary"), 56),
        name=name,
    )(x, w)


def _mm_res_kernel(*refs, n_lhs, n_parts, n_res, n_prompt_blocks):
    pos = 0
    lhs = []
    for n in n_lhs:
        lhs.append(refs[pos:pos + n])
        pos += n
    w_refs = refs[pos:pos + n_parts]
    pos += n_parts
    res = refs[pos:pos + n_res]
    pos += n_res
    gate_ref, o_ref = refs[pos], refs[pos + 1]

    def body(r):
        acc = None
        for x_ref, w_ref in zip(r[:n_parts], w_refs):
            part = _dot(x_ref[...], w_ref[...].astype(BF16))
            acc = part if acc is None else acc + part
        o_ref[...] = r[n_parts][...] + gate_ref[...] * acc

    _by_stream(pl.program_id(0), n_prompt_blocks, lhs + [res], body)


def _matmul_residual(lhs_parts, w, res_parts, mod, gate_slot, bn, n_prompt_rows, lat_len, name,
                     single_buffer_x=False, vmem_mb=56):
    bm = ROW_TILE
    n = w.shape[1]
    kq = w.shape[0] // len(lhs_parts)
    t = sum(r.shape[0] for r in res_parts)
    npb = n_prompt_rows // bm
    cidx = functools.partial(_cond_index, bm=bm, n_prompt_rows=n_prompt_rows, lat_len=lat_len)
    gate_col0 = gate_slot * (D_MODEL // bn)
    in_specs, args = [], []
    for parts in lhs_parts:
        in_specs += _row_specs(parts, bm, kq, lambda j, cb: cb, npb, single_buffer=single_buffer_x)
        args += [a for a, _ in parts]
    for q in range(len(lhs_parts)):
        in_specs.append(pl.BlockSpec((kq, bn), lambda i, j, q=q: (q, j)))
        args.append(w)
    in_specs += _row_specs([(r, 0) for r in res_parts], bm, bn, lambda j, cb: j, npb)
    args += list(res_parts)
    in_specs.append(pl.BlockSpec((None, 1, bn), lambda i, j: (cidx(i), 0, gate_col0 + j)))
    args.append(mod)
    body = functools.partial(_mm_res_kernel, n_lhs=tuple(len(p) for p in lhs_parts), n_parts=len(lhs_parts),
                             n_res=len(res_parts), n_prompt_blocks=npb)
    return pl.pallas_call(
        body,
        grid=(t // bm, n // bn),
        in_specs=in_specs,
        out_specs=pl.BlockSpec((bm, bn), lambda i, j: (i, j)),
        out_shape=jax.ShapeDtypeStruct((t, n), F32),
        compiler_params=_params(("arbitrary", "arbitrary"), vmem_mb),
        name=name,
    )(*args)


def _glu_kernel(y_ref, w_ref, b_ref, yt_ref, o_ref):
    z = _dot(y_ref[...], w_ref[...].astype(BF16)) + b_ref[...]
    o_ref[...] = (yt_ref[...].astype(F32) * _sigmoid(z)).astype(o_ref.dtype)


def _glu(y, w, b, bn):
    m, k = y.shape
    n = w.shape[1]
    bm = ROW_TILE
    return pl.pallas_call(
        _glu_kernel,
        grid=(m // bm, n // bn),
        in_specs=[pl.BlockSpec((bm, k), lambda i, j: (i, 0)),
                  pl.BlockSpec((k, bn), lambda i, j: (0, j)),
                  pl.BlockSpec((1, bn), lambda i, j: (0, j)),
                  pl.BlockSpec((bm, bn), lambda i, j: (i, j))],
        out_specs=pl.BlockSpec((bm, bn), lambda i, j: (i, j)),
        out_shape=jax.ShapeDtypeStruct((m, n), BF16),
        compiler_params=_params(("arbitrary", "arbitrary"), 56),
        name="s5_glu",
    )(y, w, b.reshape(1, n), y)


def _ffn_up_kernel(x_ref, wg_ref, wv_ref, cwg_ref, cwv_ref, cbg_ref, cbv_ref, o_ref, *,
                   sub, n_sub, prompt_blocks, prompt_len, lat_len):
    i = pl.program_id(0)
    period = jnp.where(i < prompt_blocks, prompt_len, lat_len)
    pos = lax.broadcasted_iota(jnp.int32, (sub, 1), 0) & (period - 1)
    first = pos == 0
    last = pos == period - 1
    wg = wg_ref[...].astype(BF16)
    wv = wv_ref[...].astype(BF16)

    def conv(u, cw_ref, cb_ref):
        prev = jnp.where(first, 0.0, pltpu.roll(u, 1, 0))
        nxt = jnp.where(last, 0.0, pltpu.roll(u, sub - 1, 0))
        return prev * cw_ref[0:1, :] + u * cw_ref[1:2, :] + nxt * cw_ref[2:3, :] + cb_ref[...]

    for s in range(n_sub):
        x = x_ref[s * sub:(s + 1) * sub, :]
        g = conv(_dot(x, wg), cwg_ref, cbg_ref)
        v = conv(_dot(x, wv), cwv_ref, cbv_ref)
        o_ref[s * sub:(s + 1) * sub, :] = (g * _sigmoid(g) * v).astype(o_ref.dtype)


def _ffn_up(h, w_up, conv_w, conv_b, n_prompt_rows, prompt_len, lat_len):
    t, d = h.shape
    sub, n_sub, bn = lat_len, 2, 256
    bm = sub * n_sub
    assert t % bm == 0 and n_prompt_rows % bm == 0 and sub % prompt_len == 0
    assert prompt_len & (prompt_len - 1) == 0 and lat_len & (lat_len - 1) == 0
    nj = D_FF // bn
    conv_b = conv_b.reshape(1, 2 * D_FF)
    body = functools.partial(_ffn_up_kernel, sub=sub, n_sub=n_sub, prompt_blocks=n_prompt_rows // bm,
                             prompt_len=prompt_len, lat_len=lat_len)
    return pl.pallas_call(
        body,
        grid=(t // bm, nj),
        in_specs=[pl.BlockSpec((bm, d), lambda i, j: (i, 0), pipeline_mode=pl.Buffered(1)),
                  pl.BlockSpec((d, bn), lambda i, j: (0, j)),
                  pl.BlockSpec((d, bn), lambda i, j: (0, nj + j)),
                  pl.BlockSpec((3, bn), lambda i, j: (0, j)),
                  pl.BlockSpec((3, bn), lambda i, j: (0, nj + j)),
                  pl.BlockSpec((1, bn), lambda i, j: (0, j)),
                  pl.BlockSpec((1, bn), lambda i, j: (0, nj + j))],
        out_specs=pl.BlockSpec((bm, bn), lambda i, j: (i, j)),
        out_shape=jax.ShapeDtypeStruct((t, D_FF), BF16),
        compiler_params=_params(("arbitrary", "arbitrary"), 56),
        name="ffn_up_conv",
    )(h, w_up, w_up, conv_w, conv_w, conv_b, conv_b)


def _s5_operators(lam_re, lam_im, log_dt, b_re, b_im, c_re, c_im, d_skip):
    hp = lax.Precision.HIGHEST
    q, g, p, c = S5_CHUNK, S5_GROUPS, S5_STATE, S5_GROUP
    dt = jnp.exp(log_dt)[None, :, :, None]
    tau = jnp.arange(q + 1, dtype=F32)[:, None, None, None]
    mag = jnp.exp(lam_re[None] * dt * tau)
    ang = lam_im[None] * dt * tau
    pw_re, pw_im = mag * jnp.cos(ang), mag * jnp.sin(ang)
    dt1 = jnp.exp(log_dt)[:, :, None]
    m1 = jnp.exp(lam_re * dt1)
    ab_re, ab_im = m1 * jnp.cos(lam_im * dt1), m1 * jnp.sin(lam_im * dt1)
    den = lam_re * lam_re + lam_im * lam_im
    nr, ni = ab_re - 1.0, ab_im
    f_re = ((nr * lam_re + ni * lam_im) / den)[..., None]
    f_im = ((ni * lam_re - nr * lam_im) / den)[..., None]
    bb_re = f_re * b_re - f_im * b_im
    bb_im = f_re * b_im + f_im * b_re
    m_re = pw_re[..., None] * bb_re[None] - pw_im[..., None] * bb_im[None]
    m_im = pw_re[..., None] * bb_im[None] + pw_im[..., None] * bb_re[None]
    kk = (jnp.einsum('dgop,tdgpi->tdgio', c_re, m_re, precision=hp)
          - jnp.einsum('dgop,tdgpi->tdgio', c_im, m_im, precision=hp))
    s_idx = jnp.arange(q)[:, None]
    t_idx = jnp.arange(q)[None, :]
    lag_f = t_idx - s_idx
    oh_f = (lag_f[:, :, None] == jnp.arange(q)[None, None, :]).astype(F32)
    oh_b = ((-lag_f)[:, :, None] == jnp.arange(q)[None, None, :]).astype(F32)
    t_f = jnp.einsum('stl,lgio->gsito', oh_f, kk[:q, 0], precision=hp)
    t_b = jnp.einsum('stl,lgio->gsito', oh_b, kk[:q, 1], precision=hp)
    eye_t = jnp.eye(q, dtype=F32)
    eye_c = jnp.eye(c, dtype=F32)
    skip = d_skip[:, None, :, None, None] * eye_t[None, :, None, :, None] * eye_c[None, None, :, None, :]
    t_op = (t_f + t_b + skip).reshape(g, q * c, q * c)
    pf_re, pf_im = pw_re[1:, 0], pw_im[1:, 0]
    pb_re, pb_im = pw_re[1:, 1][::-1], pw_im[1:, 1][::-1]

    def e_op(cr, ci, pr, pi):
        e_r = jnp.einsum('gop,tgp->gpto', cr, pr) - jnp.einsum('gop,tgp->gpto', ci, pi)
        e_i = -(jnp.einsum('gop,tgp->gpto', cr, pi) + jnp.einsum('gop,tgp->gpto', ci, pr))
        return e_r.reshape(g, p, q * c), e_i.reshape(g, p, q * c)

    ef_r, ef_i = e_op(c_re[0], c_im[0], pf_re, pf_im)
    eb_r, eb_i = e_op(c_re[1], c_im[1], pb_re, pb_im)
    zero = jnp.zeros_like(ef_r)
    te = jnp.concatenate([t_op, ef_r, zero, ef_i, zero, zero, eb_r, zero, eb_i], axis=1)
    sf_re = jnp.transpose(m_re[:q, 0][::-1], (1, 0, 3, 2)).reshape(g, q * c, p)
    sf_im = jnp.transpose(m_im[:q, 0][::-1], (1, 0, 3, 2)).reshape(g, q * c, p)
    sb_re = jnp.transpose(m_re[:q, 1], (1, 0, 3, 2)).reshape(g, q * c, p)
    sb_im = jnp.transpose(m_im[:q, 1], (1, 0, 3, 2)).reshape(g, q * c, p)
    sb = jnp.concatenate([sf_re, sb_re, sf_im, sb_im], axis=2)
    a16 = jnp.concatenate([pw_re[q, 0], pw_re[q, 1], pw_im[q, 0], pw_im[q, 1]], axis=-1)[:, None, :]
    return sb.astype(BF16), te.astype(BF16), a16


def _gelu_tanh(x):
    return 0.5 * x * (1.0 + jnp.tanh(math.sqrt(2.0 / math.pi) * (x + 0.044715 * (x * x * x))))


def _s5_kernel(u_ref, sb_ref, te_ref, a_ref, h0_ref, y_ref, hfin_ref, w_scr, h_scr, *, segments):
    u = u_ref[...]
    w_scr[...] = _dot(u, sb_ref[...])
    half = S5_STATE
    ar = a_ref[0:1, 0:2 * half]
    ai = a_ref[0:1, 2 * half:4 * half]
    fwd_lane = lax.broadcasted_iota(jnp.int32, (1, 2 * half), 1) < half
    for row0, nb, nc, h_row0, write_final in segments:
        hr = h0_ref[h_row0:h_row0 + nb, 0:2 * half]
        hi = h0_ref[h_row0:h_row0 + nb, 2 * half:4 * half]
        for i in range(nc):
            ri = row0 + i * nb
            rj = row0 + (nc - 1 - i) * nb
            h_scr[ri:ri + nb, 0:2 * half] = hr
            h_scr[ri:ri + nb, 2 * half:4 * half] = hi
            h_scr[rj:rj + nb, 4 * half:6 * half] = hr
            h_scr[rj:rj + nb, 6 * half:8 * half] = hi
            wr = jnp.where(fwd_lane, w_scr[ri:ri + nb, 0:2 * half], w_scr[rj:rj + nb, 0:2 * half])
            wi = jnp.where(fwd_lane, w_scr[ri:ri + nb, 2 * half:4 * half], w_scr[rj:rj + nb, 2 * half:4 * half])
            hr, hi = ar * hr - ai * hi + wr, ar * hi + ai * hr + wi
        if write_final:
            hfin_ref[:, 0:2 * half] = hr
            hfin_ref[:, 2 * half:4 * half] = hi
    qc = S5_CHUNK * S5_GROUP
    y = _dot(u, te_ref[0:qc, :]) + _dot(h_scr[...].astype(BF16), te_ref[qc:, :])
    y_ref[...] = _gelu_tanh(y).astype(y_ref.dtype)


def _s5_chunked(u_chunks, sb, te, a16, h0, n_prompt, prompt_chunks, n_lat, lat_chunks):
    g, rows, qc = u_chunks.shape
    sw = 4 * S5_STATE
    segments = ((0, n_prompt, prompt_chunks, 0, True),
                (n_prompt * prompt_chunks, n_lat, lat_chunks, n_prompt, False))
    return pl.pallas_call(
        functools.partial(_s5_kernel, segments=segments),
        grid=(g,),
        in_specs=[pl.BlockSpec((None, rows, qc), lambda i: (i, 0, 0)),
                  pl.BlockSpec((None, qc, sw), lambda i: (i, 0, 0)),
                  pl.BlockSpec((None, qc + 2 * sw, qc), lambda i: (i, 0, 0)),
                  pl.BlockSpec((None, 1, sw), lambda i: (i, 0, 0)),
                  pl.BlockSpec((None, n_prompt + n_lat, sw), lambda i: (i, 0, 0))],
        out_specs=[pl.BlockSpec((None, rows, qc), lambda i: (i, 0, 0)),
                   pl.BlockSpec((None, n_prompt, sw), lambda i: (i, 0, 0))],
        out_shape=[jax.ShapeDtypeStruct((g, rows, qc), BF16),
                   jax.ShapeDtypeStruct((g, n_prompt, sw), F32)],
        scratch_shapes=[pltpu.VMEM((rows, sw), F32), pltpu.VMEM((rows, 2 * sw), F32)],
        compiler_params=_params(("arbitrary",), 40),
        name="s5_chunked",
    )(u_chunks, sb, te, a16, h0)


def _rope_tables(length, rot):
    n_freq = rot // 4
    t = jnp.arange(length)
    row = (t // GRID_W).astype(F32)
    col = (t % GRID_W).astype(F32)
    inv = ROPE_BASE ** (-jnp.arange(n_freq, dtype=F32) / n_freq)
    ar, ac = row[:, None] * inv, col[:, None] * inv
    cos = jnp.concatenate([jnp.cos(ar), jnp.cos(ar), jnp.cos(ac), jnp.cos(ac)], axis=-1)
    sin = jnp.concatenate([-jnp.sin(ar), jnp.sin(ar), -jnp.sin(ac), jnp.sin(ac)], axis=-1)
    pad = LANE - rot
    if pad:
        cos = jnp.concatenate([cos, jnp.ones((length, pad), F32)], axis=-1)
        sin = jnp.concatenate([sin, jnp.zeros((length, pad), F32)], axis=-1)
    return cos, sin


def _rope(x, cos, sin, blk):
    lane = lax.broadcasted_iota(jnp.int32, (1, LANE), 1)
    lower = (lane & blk) == 0
    partner = jnp.where(lower, pltpu.roll(x, LANE - blk, 1), pltpu.roll(x, blk, 1))
    return x * cos + partner * sin


def _softmax_pv(scores, values, extra_logit=None):
    m = scores[0].max(axis=-1, keepdims=True)
    for s in scores[1:]:
        m = jnp.maximum(m, s.max(axis=-1, keepdims=True))
    if extra_logit is not None:
        m = jnp.maximum(m, extra_logit)
    den = None
    out = None
    for s, v in zip(scores, values):
        p = jnp.exp(s - m)
        d = p.sum(axis=-1, keepdims=True)
        o = _dot(p.astype(BF16), v)
        den = d if den is None else den + d
        out = o if out is None else out + o
    if extra_logit is not None:
        den = den + jnp.exp(extra_logit - m)
    return out / den


def _ctx_gqa_kernel(sink_ref, q_ref, k_ref, v_ref, o_ref):
    g = pl.program_id(1)
    scale = HEAD_DIM ** -0.5
    k = k_ref[...].astype(BF16)
    v = v_ref[...].astype(BF16)
    for r in range(WIN_GROUP):
        q = q_ref[:, r * HEAD_DIM:(r + 1) * HEAD_DIM]
        s = _dot_t(q, k) * scale
        o = _softmax_pv([s], [v], sink_ref[g * WIN_GROUP + r])
        o_ref[:, r * HEAD_DIM:(r + 1) * HEAD_DIM] = o.astype(o_ref.dtype)


def _ctx_gqa_attention(uq3, k4, v4, sink, n_prompt, length):
    qw = WIN_GROUP * HEAD_DIM
    q0 = S5_WIDTH // qw
    kv_spec = pl.BlockSpec((None, None, length, HEAD_DIM), lambda b, g: (b, g, 0, 0))
    return pl.pallas_call(
        _ctx_gqa_kernel,
        grid=(n_prompt, WIN_KV_HEADS),
        in_specs=[pl.BlockSpec(memory_space=pltpu.SMEM),
                  pl.BlockSpec((None, length, qw), lambda b, g: (b, 0, q0 + g)),
                  kv_spec, kv_spec],
        out_specs=pl.BlockSpec((None, length, qw), lambda b, g: (b, 0, g)),
        out_shape=jax.ShapeDtypeStruct((n_prompt, length, WIN_Q_WIDTH), BF16),
        compiler_params=_params(("arbitrary", "arbitrary"), 40),
        name="ctx_gqa_attention",
    )(sink, uq3, k4, v4)


def _lat_window_kernel(sink_ref, q_ref, k_ref, v_ref, kc_ref, vc_ref, cos_ref, sin_ref, o_ref, *, length):
    g = pl.program_id(1)
    scale = HEAD_DIM ** -0.5
    blk = WIN_RADIUS
    nb = length // blk
    k = _rope(k_ref[...], cos_ref[...], sin_ref[...], HEAD_DIM // 4).astype(BF16)
    v = v_ref[...].astype(BF16)
    kc = kc_ref[...].astype(BF16)
    vc = vc_ref[...].astype(BF16)
    rows = WIN_GROUP * blk
    row = lax.broadcasted_iota(jnp.int32, (rows, 1), 0)
    sk = jnp.zeros((rows, 1), F32)
    for r in range(WIN_GROUP):
        sk = jnp.where(jnp.logical_and(row >= r * blk, row < (r + 1) * blk), sink_ref[g * WIN_GROUP + r], sk)
    qoff = row & (blk - 1)
    for n in range(nb):
        lo = max(0, n - 1) * blk
        hi = min(nb, n + 2) * blk
        cos = cos_ref[n * blk:(n + 1) * blk, :]
        sin = sin_ref[n * blk:(n + 1) * blk, :]
        q = jnp.concatenate(
            [_rope(q_ref[n * blk:(n + 1) * blk, r * HEAD_DIM:(r + 1) * HEAD_DIM].astype(F32), cos, sin, HEAD_DIM // 4)
             for r in range(WIN_GROUP)], axis=0).astype(BF16)
        dist = (n * blk + qoff) - (lo + lax.broadcasted_iota(jnp.int32, (1, hi - lo), 1))
        visible = jnp.logical_and(dist <= WIN_RADIUS, dist >= -WIN_RADIUS)
        s_loc = jnp.where(visible, _dot_t(q, k[lo:hi]) * scale, NEG)
        s_ctx = _dot_t(q, kc) * scale
        o = _softmax_pv([s_loc, s_ctx], [v[lo:hi], vc], sk)
        for r in range(WIN_GROUP):
            o_ref[n * blk:(n + 1) * blk, r * HEAD_DIM:(r + 1) * HEAD_DIM] = o[r * blk:(r + 1) * blk].astype(o_ref.dtype)


def _lat_window_attention(uq3, seq0, k4, v4, k_ctx, v_ctx, sink, cos, sin):
    n_lat, _, length, _ = k4.shape
    qw = WIN_GROUP * HEAD_DIM
    q0 = S5_WIDTH // qw
    lc = k_ctx.shape[2]
    kv_spec = pl.BlockSpec((None, None, length, HEAD_DIM), lambda b, g: (b, g, 0, 0))
    ctx_spec = pl.BlockSpec((None, None, lc, HEAD_DIM), lambda b, g: (b, g, 0, 0))
    tab_spec = pl.BlockSpec((length, LANE), lambda b, g: (0, 0))
    return pl.pallas_call(
        functools.partial(_lat_window_kernel, length=length),
        grid=(n_lat, WIN_KV_HEADS),
        in_specs=[pl.BlockSpec(memory_space=pltpu.SMEM),
                  pl.BlockSpec((None, length, qw), lambda b, g: (seq0 + b, 0, q0 + g)),
                  kv_spec, kv_spec, ctx_spec, ctx_spec, tab_spec, tab_spec],
        out_specs=pl.BlockSpec((None, length, qw), lambda b, g: (b, 0, g)),
        out_shape=jax.ShapeDtypeStruct((n_lat, length, WIN_Q_WIDTH), BF16),
        compiler_params=_params(("arbitrary", "arbitrary"), 48),
        name="latent_window_attention",
    )(sink, uq3, k4, v4, k_ctx, v_ctx, cos, sin)


def _ctx_odd_kernel(q_ref, kv_ref, kpe_ref, qd_ref, kd_ref, vd_ref, o_ref):
    kw = MLA_NOPE + LANE
    kpe = kpe_ref[...].astype(BF16)
    scale_c = (MLA_NOPE + MLA_ROPE) ** -0.5
    for h in range(MLA_HEADS):
        q = q_ref[:, h * kw:(h + 1) * kw]
        k = jnp.concatenate([kv_ref[:, h * kw:h * kw + MLA_NOPE], kpe], axis=1)
        v = kv_ref[:, h * kw + MLA_NOPE:(h + 1) * kw]
        o = _softmax_pv([_dot_t(q, k) * scale_c], [v])
        o_ref[:, h * MLA_V:(h + 1) * MLA_V] = o.astype(o_ref.dtype)
    scale_d = HEAD_DIM ** -0.5
    base = MLA_HEADS * MLA_V
    for h in range(NA_HEADS):
        q = qd_ref[:, h * HEAD_DIM:(h + 1) * HEAD_DIM]
        o = _softmax_pv([_dot_t(q, kd_ref[h].astype(BF16)) * scale_d], [vd_ref[h].astype(BF16)])
        o_ref[:, base + h * HEAD_DIM:base + (h + 1) * HEAD_DIM] = o.astype(o_ref.dtype)


def _ctx_odd_attention(q3, kv3, kpe3, qd3, kd4, vd4, n_prompt, length):
    qn = q3.shape[-1]
    head_spec = pl.BlockSpec((None, NA_HEADS, length, HEAD_DIM), lambda b: (b, 0, 0, 0))
    return pl.pallas_call(
        _ctx_odd_kernel,
        grid=(n_prompt,),
        in_specs=[pl.BlockSpec((None, length, qn), lambda b: (b, 0, 0)),
                  pl.BlockSpec((None, length, qn), lambda b: (b, 0, 0)),
                  pl.BlockSpec((None, length, LANE), lambda b: (b, 0, 0)),
                  pl.BlockSpec((None, length, NA_WIDTH), lambda b: (b, 0, 0)),
                  head_spec, head_spec],
        out_specs=pl.BlockSpec((None, length, MLA_HEADS * MLA_V + NA_WIDTH), lambda b: (b, 0, 0)),
        out_shape=jax.ShapeDtypeStruct((n_prompt, length, MLA_HEADS * MLA_V + NA_WIDTH), BF16),
        compiler_params=_params(("arbitrary",), 48),
        name="ctx_odd_attention",
    )(q3, kv3, kpe3, qd3, kd4, vd4)


def _lat_mla_kernel(q_ref, kv_ref, kpe_ref, kvc_ref, kpec_ref, cos_ref, sin_ref, o_ref, *, length):
    scale = (MLA_NOPE + MLA_ROPE) ** -0.5
    rb = MLA_ROPE // 4
    kpe = _rope(kpe_ref[...], cos_ref[...], sin_ref[...], rb).astype(BF16)
    k_lat = jnp.concatenate([kv_ref[:, 0:MLA_NOPE], kpe], axis=1)
    v_lat = kv_ref[:, MLA_NOPE:]
    k_ctx = jnp.concatenate([kvc_ref[:, 0:MLA_NOPE], kpec_ref[...].astype(BF16)], axis=1)
    v_ctx = kvc_ref[:, MLA_NOPE:]
    qb = 256
    for n in range(length // qb):
        rows = slice(n * qb, (n + 1) * qb)
        q_pe = _rope(q_ref[rows, MLA_NOPE:].astype(F32), cos_ref[rows, :], sin_ref[rows, :], rb).astype(BF16)
        q = jnp.concatenate([q_ref[rows, 0:MLA_NOPE], q_pe], axis=1)
        o = _softmax_pv([_dot_t(q, k_lat) * scale, _dot_t(q, k_ctx) * scale], [v_lat, v_ctx])
        o_ref[rows, :] = o.astype(o_ref.dtype)


def _lat_mla_attention(q3, kv3, kvc3, kpe3, kpe_ctx, seq0, n_lat, length, cos, sin):
    kw = MLA_NOPE + LANE
    lc = kpe_ctx.shape[1]
    return pl.pallas_call(
        functools.partial(_lat_mla_kernel, length=length),
        grid=(n_lat, MLA_HEADS),
        in_specs=[pl.BlockSpec((None, length, kw), lambda b, h: (seq0 + b, 0, h)),
                  pl.BlockSpec((None, length, kw), lambda b, h: (seq0 + b, 0, h)),
                  pl.BlockSpec((None, length, LANE), lambda b, h: (seq0 + b, 0, 0)),
                  pl.BlockSpec((None, lc, kw), lambda b, h: (b, 0, h)),
                  pl.BlockSpec((None, lc, LANE), lambda b, h: (b, 0, 0)),
                  pl.BlockSpec((length, LANE), lambda b, h: (0, 0)),
                  pl.BlockSpec((length, LANE), lambda b, h: (0, 0))],
        out_specs=pl.BlockSpec((None, length, MLA_V), lambda b, h: (b, 0, h)),
        out_shape=jax.ShapeDtypeStruct((n_lat, length, MLA_HEADS * MLA_V), BF16),
        compiler_params=_params(("arbitrary", "arbitrary"), 48),
        name="latent_mla_attention",
    )(q3, kv3, kpe3, kvc3, kpe_ctx, cos, sin)


def _na_row_start(r, rows):
    kr = min(NA_ROWS, rows)
    return min(max(r - kr // 2, 0), rows - kr)


def _lat_na_kernel(q_ref, k_ref, v_ref, kc_ref, vc_ref, bias_ref, o_ref, *, length):
    scale = HEAD_DIM ** -0.5
    rows = length // GRID_W
    kr = min(NA_ROWS, rows)
    k = k_ref[...].astype(BF16)
    v = v_ref[...].astype(BF16)
    kc = kc_ref[...].astype(BF16)
    vc = vc_ref[...].astype(BF16)
    for r in range(rows):
        r0 = _na_row_start(r, rows) * GRID_W
        q = q_ref[r * GRID_W:(r + 1) * GRID_W, :]
        s_nb = _dot_t(q, k[r0:r0 + kr * GRID_W]) * scale + bias_ref[r]
        s_ctx = _dot_t(q, kc) * scale
        o = _softmax_pv([s_nb, s_ctx], [v[r0:r0 + kr * GRID_W], vc])
        o_ref[r * GRID_W:(r + 1) * GRID_W, :] = o.astype(o_ref.dtype)


def _na_bias(rpb, length):
    rows = length // GRID_W
    kr = min(NA_ROWS, rows)
    col = jnp.arange(GRID_W)
    c_start = jnp.clip(col - NA_COLS // 2, 0, GRID_W - NA_COLS)
    col_valid = (col[None, :] >= c_start[:, None]) & (col[None, :] < c_start[:, None] + NA_COLS)
    off_c = jnp.clip(col[None, :] - col[:, None], -(NA_COLS - 1), NA_COLS - 1) + NA_COLS - 1
    onehot = (off_c[:, :, None] == jnp.arange(2 * NA_COLS - 1)[None, None, :]).astype(F32)
    table = jnp.einsum('hdj,qkj->hdqk', rpb.astype(F32), onehot, precision=lax.Precision.HIGHEST)
    table = jnp.where(col_valid[None, None], table, NEG)
    per_row = []
    for r in range(rows):
        r0 = _na_row_start(r, rows)
        per_row.append(jnp.concatenate([table[:, r0 + j - r + NA_ROWS - 1] for j in range(kr)], axis=-1))
    return jnp.stack(per_row, axis=1)


def _lat_na_attention(qd3, seq0, k4, v4, k_ctx, v_ctx, bias):
    n_lat, _, length, _ = k4.shape
    rows = length // GRID_W
    kr = min(NA_ROWS, rows)
    lc = k_ctx.shape[2]
    kv_spec = pl.BlockSpec((None, None, length, HEAD_DIM), lambda h, b: (b, h, 0, 0))
    ctx_spec = pl.BlockSpec((None, None, lc, HEAD_DIM), lambda h, b: (b, h, 0, 0))
    return pl.pallas_call(
        functools.partial(_lat_na_kernel, length=length),
        grid=(NA_HEADS, n_lat),
        in_specs=[pl.BlockSpec((None, length, HEAD_DIM), lambda h, b: (seq0 + b, 0, h)),
                  kv_spec, kv_spec, ctx_spec, ctx_spec,
                  pl.BlockSpec((None, rows, GRID_W, kr * GRID_W), lambda h, b: (h, 0, 0, 0))],
        out_specs=pl.BlockSpec((None, length, HEAD_DIM), lambda h, b: (b, 0, h)),
        out_shape=jax.ShapeDtypeStruct((n_lat, length, NA_WIDTH), BF16),
        compiler_params=_params(("arbitrary", "arbitrary"), 48),
        name="latent_neighborhood_attention",
    )(qd3, k4, v4, k_ctx, v_ctx, bias)


def kernel(x_prompt, x_sample, cache_l0_k, cache_l0_v, state_l0_re, state_l0_im, cache_l1_ckv, cache_l1_kpe, cache_l1_k, cache_l1_v, c, c_ctx, l0_ada_w, l0_ada_b, l0_norm1, l0_norm2, l0_w_in, l0_lambda_re, l0_lambda_im, l0_log_dt, l0_b_re, l0_b_im, l0_c_re, l0_c_im, l0_d_skip, l0_w_glu, l0_b_glu, l0_sink, l0_w_out, l0_ffn_w_up, l0_ffn_conv_w, l0_ffn_conv_b, l0_ffn_w_down, l1_ada_w, l1_ada_b, l1_norm1, l1_norm2, l1_w_in, l1_q_norm, l1_kv_norm, l1_w_uq, l1_w_ukv, l1_rpb, l1_w_out, l1_ffn_w_up, l1_ffn_conv_w, l1_ffn_conv_b, l1_ffn_w_down, final_norm):
    n_p, len_p, d = x_prompt.shape
    n_s, len_s, _ = x_sample.shape
    rows_p, rows_s = n_p * len_p, n_s * len_s
    t = rows_p + rows_s
    assert d == D_MODEL and len_s == ROW_TILE and ROW_TILE % len_p == 0 and n_s + 1 <= MAX_CONDS
    assert rows_p % (2 * ROW_TILE) == 0 and rows_s % (2 * ROW_TILE) == 0
    assert len_p % S5_CHUNK == 0 and len_s % (2 * WIN_RADIUS) == 0 and len_s % GRID_W == 0
    seqs_p_units = t // len_p
    seqs_s_units = t // len_s
    seq0_s = rows_p // len_s
    groups = dict(n_prompt_rows=rows_p, lat_len=len_s)
    half = D_MODEL // 2

    cond = jnp.zeros((MAX_CONDS, d), F32).at[0].set(c_ctx).at[1:1 + n_s].set(c)
    mod0 = _modulation(cond, l0_ada_w, l0_ada_b)
    mod1 = _modulation(cond, l1_ada_w, l1_ada_b)

    x_parts = [x_prompt.reshape(rows_p, d), x_sample.reshape(rows_s, d)]

    def ffn(x, mod, norm2, w_up, conv_w, conv_b, w_down):
        h = _norm_mod([x], norm2, mod, 3, 4, **groups)
        act = _ffn_up(h, w_up, conv_w, conv_b, rows_p, len_p, len_s)
        return _matmul_residual([[(act, 0)]], w_down.astype(BF16), [x], mod, 5, 256, name="ffn_down",
                                single_buffer_x=True, **groups)

    h = _norm_mod(x_parts, l0_norm1, mod0, 0, 1, **groups)
    uq = _matmul(h, l0_w_in, BF16, 512, "l0_in_proj_uq", 0, S5_WIDTH + WIN_Q_WIDTH)
    kcol = S5_WIDTH + WIN_Q_WIDTH
    vcol = kcol + WIN_KV_WIDTH
    out_k0 = _matmul_heads(h, l0_w_in, kcol, WIN_KV_HEADS, 0, rows_p, len_p, "l0_in_proj_k_ctx")
    out_v0 = _matmul_heads(h, l0_w_in, vcol, WIN_KV_HEADS, 0, rows_p, len_p, "l0_in_proj_v_ctx")
    k0_s = _matmul_heads(h, l0_w_in, kcol, WIN_KV_HEADS, rows_p, rows_s, len_s, "l0_in_proj_k_lat")
    v0_s = _matmul_heads(h, l0_w_in, vcol, WIN_KV_HEADS, rows_p, rows_s, len_s, "l0_in_proj_v_lat")

    q16, gc = S5_CHUNK, S5_GROUP
    ch_p, ch_s = len_p // q16, len_s // q16
    np8, ns8 = -(-n_p // SUBLANE) * SUBLANE, -(-n_s // SUBLANE) * SUBLANE

    def to_chunks(u2d, nb, nb8, nch):
        u5 = u2d.reshape(nb, nch, q16, S5_GROUPS, gc).transpose(3, 1, 0, 2, 4)
        u5 = jnp.pad(u5, ((0, 0), (0, 0), (0, nb8 - nb), (0, 0), (0, 0)))
        return u5.reshape(S5_GROUPS, nch * nb8, q16 * gc)

    def from_chunks(y3, nb, nb8, nch):
        return (y3.reshape(S5_GROUPS, nch, nb8, q16, gc)[:, :, :nb].transpose(2, 1, 3, 0, 4)
                .reshape(nb * nch * q16, S5_WIDTH))

    u_bf = uq[:, :S5_WIDTH]
    u_chunks = jnp.concatenate([to_chunks(u_bf[:rows_p], n_p, np8, ch_p),
                                to_chunks(u_bf[rows_p:], n_s, ns8, ch_s)], axis=1)
    sb, te, a16 = _s5_operators(l0_lambda_re, l0_lambda_im, l0_log_dt, l0_b_re, l0_b_im, l0_c_re, l0_c_im, l0_d_skip)

    def state_cols(s):
        return s.astype(F32).transpose(2, 0, 1, 3).reshape(S5_GROUPS, s.shape[0], 2 * S5_STATE)

    h0_lat = jnp.concatenate([state_cols(state_l0_re), state_cols(state_l0_im)], axis=-1)
    h0 = jnp.zeros((S5_GROUPS, np8 + ns8, 4 * S5_STATE), F32).at[:, np8:np8 + n_s].set(h0_lat)
    y_chunks, hfin = _s5_chunked(u_chunks, sb, te, a16, h0, np8, ch_p, ns8, ch_s)
    yg = jnp.concatenate([from_chunks(y_chunks[:, :np8 * ch_p], n_p, np8, ch_p),
                          from_chunks(y_chunks[:, np8 * ch_p:], n_s, ns8, ch_s)], axis=0)
    a_out = _glu(yg, l0_w_glu, l0_b_glu, 512)

    def state_out(cols):
        return cols[:, :n_p].reshape(S5_GROUPS, n_p, 2, S5_STATE).transpose(1, 2, 0, 3)

    out_sre = state_out(hfin[:, :, :2 * S5_STATE])
    out_sim = state_out(hfin[:, :, 2 * S5_STATE:])

    cos_e, sin_e = _rope_tables(len_s, HEAD_DIM)
    o_p = _ctx_gqa_attention(uq.reshape(seqs_p_units, len_p, -1), out_k0, out_v0, l0_sink, n_p, len_p)
    o_s = _lat_window_attention(uq.reshape(seqs_s_units, len_s, -1), seq0_s, k0_s, v0_s,
                                cache_l0_k, cache_l0_v, l0_sink, cos_e, sin_e)
    x = _matmul_residual([[(a_out, 0)], [(o_p.reshape(rows_p, half), 0), (o_s.reshape(rows_s, half), 0)]],
                         l0_w_out, x_parts, mod0, 2, 512, name="l0_out_proj", single_buffer_x=True, **groups)
    x = ffn(x, mod0, l0_norm2, l0_ffn_w_up, l0_ffn_conv_w, l0_ffn_conv_b, l0_ffn_w_down)

    s1, s2 = MLA_Q_LORA + MLA_KV_LORA, MLA_Q_LORA + MLA_KV_LORA + MLA_ROPE
    w_dkv = l1_w_in[:, s2:].astype(BF16)
    w_kpe = jnp.pad(l1_w_in[:, s1:s2], ((0, 0), (0, LANE - MLA_ROPE))).astype(BF16)
    w_uq = l1_w_uq.reshape(MLA_Q_LORA, MLA_HEADS, MLA_NOPE + MLA_ROPE)
    w_uq = jnp.pad(w_uq, ((0, 0), (0, 0), (0, LANE - MLA_ROPE))).reshape(MLA_Q_LORA, -1).astype(BF16)

    h = _norm_mod([x], l1_norm1, mod1, 0, 1, **groups)
    cqkv = _matmul(h, l1_w_in, F32, 512, "l1_in_proj_lora", 0, s1)
    kpe = _matmul(h, w_kpe, F32, LANE, "l1_in_proj_kpe")
    qd = _matmul(h, w_dkv, BF16, 512, "l1_in_proj_qd", 0, NA_WIDTH)
    out_k1 = _matmul_heads(h, w_dkv, NA_WIDTH, NA_HEADS, 0, rows_p, len_p, "l1_in_proj_kd_ctx")
    out_v1 = _matmul_heads(h, w_dkv, 2 * NA_WIDTH, NA_HEADS, 0, rows_p, len_p, "l1_in_proj_vd_ctx")
    k1_s = _matmul_heads(h, w_dkv, NA_WIDTH, NA_HEADS, rows_p, rows_s, len_s, "l1_in_proj_kd_lat")
    v1_s = _matmul_heads(h, w_dkv, 2 * NA_WIDTH, NA_HEADS, rows_p, rows_s, len_s, "l1_in_proj_vd_lat")
    cqn = _rmsnorm_cols(cqkv, l1_q_norm, 0, MLA_Q_LORA, BF16)
    ckvn = _rmsnorm_cols(cqkv, l1_kv_norm, MLA_Q_LORA // MLA_KV_LORA, MLA_KV_LORA, F32)
    q_all = _matmul(cqn, w_uq, BF16, 512, "l1_q_up")
    kv_all = _matmul(ckvn, l1_w_ukv, BF16, 512, "l1_kv_up")
    lc = cache_l1_ckv.shape[1]
    kv_ctx = _matmul(cache_l1_ckv.reshape(-1, MLA_KV_LORA), l1_w_ukv, BF16, 512, "l1_kv_up_ctx")

    out_ckv = ckvn[:rows_p].reshape(n_p, len_p, MLA_KV_LORA)
    out_kpe = kpe[:rows_p, :MLA_ROPE].reshape(n_p, len_p, MLA_ROPE)

    o_p = _ctx_odd_attention(q_all.reshape(seqs_p_units, len_p, -1),
                             kv_all.reshape(seqs_p_units, len_p, -1),
                             kpe.reshape(seqs_p_units, len_p, LANE),
                             qd.reshape(seqs_p_units, len_p, -1), out_k1, out_v1, n_p, len_p)
    cos_o, sin_o = _rope_tables(len_s, MLA_ROPE)
    kpe_ctx = jnp.pad(cache_l1_kpe, ((0, 0), (0, 0), (0, LANE - MLA_ROPE)))
    oc_s = _lat_mla_attention(q_all.reshape(seqs_s_units, len_s, -1),
                              kv_all.reshape(seqs_s_units, len_s, -1),
                              kv_ctx.reshape(n_s, lc, -1),
                              kpe.reshape(seqs_s_units, len_s, LANE), kpe_ctx,
                              seq0_s, n_s, len_s, cos_o, sin_o)
    od_s = _lat_na_attention(qd.reshape(seqs_s_units, len_s, -1), seq0_s, k1_s, v1_s,
                             cache_l1_k, cache_l1_v, _na_bias(l1_rpb, len_s))
    o_p2 = o_p.reshape(rows_p, 2 * half)
    x = _matmul_residual([[(o_p2, 0), (oc_s.reshape(rows_s, half), 0)],
                          [(o_p2, 1), (od_s.reshape(rows_s, half), 0)]],
                         l1_w_out, [x], mod1, 2, 512, name="l1_out_proj", single_buffer_x=True, **groups)
    x = ffn(x, mod1, l1_norm2, l1_ffn_w_up, l1_ffn_conv_w, l1_ffn_conv_b, l1_ffn_w_down)

    y_prompt = _rmsnorm_cols(x, final_norm, 0, d, F32, 0, rows_p).reshape(n_p, len_p, d)
    y_sample = _rmsnorm_cols(x, final_norm, 0, d, F32, rows_p, rows_s).reshape(n_s, len_s, d)
    return (y_prompt, y_sample, out_k0, out_v0, out_sre, out_sim, out_ckv, out_kpe, out_k1, out_v1)
```

```python
import functools
import math

import jax
import jax.numpy as jnp
from jax import lax
from jax.experimental import pallas as pl
from jax.experimental.pallas import tpu as pltpu

F32 = jnp.float32
BF16 = jnp.bfloat16

D_MODEL = 4096
GRID_W = 64
HEAD_DIM = 128
S5_WIDTH = 2048
S5_GROUP = 16
S5_GROUPS = S5_WIDTH // S5_GROUP
S5_STATE = 64
S5_CHUNK = 16
WIN_HEADS = 16
WIN_KV_HEADS = 4
WIN_GROUP = WIN_HEADS // WIN_KV_HEADS
WIN_RADIUS = 128
WIN_Q_WIDTH = WIN_HEADS * HEAD_DIM
WIN_KV_WIDTH = WIN_KV_HEADS * HEAD_DIM
MLA_HEADS = 16
MLA_Q_LORA = 1024
MLA_KV_LORA = 512
MLA_NOPE = 128
MLA_ROPE = 64
MLA_V = 128
NA_HEADS = 16
NA_ROWS = 8
NA_COLS = 16
NA_WIDTH = NA_HEADS * HEAD_DIM
D_FF = 11008
ROPE_BASE = 10000.0
EPS = 1e-6
NEG = -1e30

LANE = 128
SUBLANE = 8
MAX_CONDS = 8
ROW_TILE = 1024


def _params(sem, vmem_mb):
    return pltpu.CompilerParams(dimension_semantics=sem, vmem_limit_bytes=vmem_mb * 1024 * 1024)


def _cond_index(i, bm, n_prompt_rows, lat_len):
    first = n_prompt_rows // bm
    per = lat_len // bm
    return jnp.where(i < first, 0, 1 + (i - first) // per)


def _dot(a, b):
    return jnp.dot(a, b, preferred_element_type=F32)


def _dot_t(a, b):
    return lax.dot_general(a, b, (((1,), (1,)), ((), ())), preferred_element_type=F32)


def _sigmoid(x):
    return 1.0 / (1.0 + jnp.exp(-x))


def _row_specs(parts, bm, width, col_fn, n_prompt_blocks, single_buffer=False):
    mode = dict(pipeline_mode=pl.Buffered(1)) if single_buffer else {}
    if len(parts) == 1:
        cb = parts[0][1]
        return [pl.BlockSpec((bm, width), lambda i, j: (i, col_fn(j, cb)), **mode)]
    cb_p, cb_s = parts[0][1], parts[1][1]
    last_p = n_prompt_blocks - 1
    return [pl.BlockSpec((bm, width), lambda i, j: (jnp.minimum(i, last_p), col_fn(j, cb_p)), **mode),
            pl.BlockSpec((bm, width), lambda i, j: (jnp.maximum(i - n_prompt_blocks, 0), col_fn(j, cb_s)), **mode)]


def _by_stream(i, n_prompt_blocks, operands, body):
    if all(len(o) == 1 for o in operands):
        body([o[0] for o in operands])
        return

    @pl.when(i < n_prompt_blocks)
    def _():
        body([o[0] for o in operands])

    @pl.when(i >= n_prompt_blocks)
    def _():
        body([o[-1] for o in operands])


def _mod_kernel(c_ref, w_ref, b_ref, o_ref):
    c = c_ref[...]
    s = c * _sigmoid(c)
    o_ref[...] = _dot(s.astype(BF16), w_ref[...].astype(BF16)) + b_ref[...]


def _modulation(cond, w, b):
    d, n = w.shape
    bn = 512
    out = pl.pallas_call(
        _mod_kernel,
        grid=(n // bn,),
        in_specs=[pl.BlockSpec((MAX_CONDS, d), lambda j: (0, 0)),
                  pl.BlockSpec((d, bn), lambda j: (0, j)),
                  pl.BlockSpec((1, bn), lambda j: (0, j))],
        out_specs=pl.BlockSpec((MAX_CONDS, bn), lambda j: (0, j)),
        out_shape=jax.ShapeDtypeStruct((MAX_CONDS, n), F32),
        compiler_params=_params(("arbitrary",), 40),
        name="modulation",
    )(cond, w, b.reshape(1, n))
    return out.reshape(MAX_CONDS, 1, n)


def _norm_mod_kernel(*refs, n_x, n_prompt_blocks):
    x_refs, (g_ref, sh_ref, sc_ref, o_ref) = refs[:n_x], refs[n_x:]

    def body(r):
        x = r[0][...]
        ms = jnp.mean(x * x, axis=-1, keepdims=True)
        y = x * lax.rsqrt(ms + EPS) * g_ref[...]
        o_ref[...] = (y * (1.0 + sc_ref[...]) + sh_ref[...]).astype(o_ref.dtype)

    _by_stream(pl.program_id(0), n_prompt_blocks, [x_refs], body)


def _norm_mod(x_parts, gain, mod, shift_slot, scale_slot, n_prompt_rows, lat_len):
    d = D_MODEL
    bm = 256
    t = sum(x.shape[0] for x in x_parts)
    npb = n_prompt_rows // bm
    cidx = functools.partial(_cond_index, bm=bm, n_prompt_rows=n_prompt_rows, lat_len=lat_len)
    specs = _row_specs([(x, 0) for x in x_parts], bm, d, lambda j, cb: 0, npb)
    to1d = lambda spec: pl.BlockSpec(spec.block_shape, lambda i, f=spec.index_map: f(i, 0))
    return pl.pallas_call(
        functools.partial(_norm_mod_kernel, n_x=len(x_parts), n_prompt_blocks=npb),
        grid=(t // bm,),
        in_specs=[to1d(s) for s in specs] + [
            pl.BlockSpec((1, d), lambda i: (0, 0)),
            pl.BlockSpec((None, 1, d), lambda i: (cidx(i), 0, shift_slot)),
            pl.BlockSpec((None, 1, d), lambda i: (cidx(i), 0, scale_slot))],
        out_specs=pl.BlockSpec((bm, d), lambda i: (i, 0)),
        out_shape=jax.ShapeDtypeStruct((t, d), BF16),
        compiler_params=_params(("arbitrary",), 40),
        name="norm_mod",
    )(*x_parts, gain.reshape(1, d), mod, mod)


def _rmsnorm_kernel(x_ref, g_ref, o_ref):
    x = x_ref[...].astype(F32)
    ms = jnp.mean(x * x, axis=-1, keepdims=True)
    o_ref[...] = (x * lax.rsqrt(ms + EPS) * g_ref[...]).astype(o_ref.dtype)


def _rmsnorm_cols(x, gain, col_block, width, out_dtype, row0=0, rows=None):
    bm = 256
    rows = x.shape[0] if rows is None else rows
    rb0 = row0 // bm
    return pl.pallas_call(
        _rmsnorm_kernel,
        grid=(rows // bm,),
        in_specs=[pl.BlockSpec((bm, width), lambda i: (i + rb0, col_block)),
                  pl.BlockSpec((1, width), lambda i: (0, 0))],
        out_specs=pl.BlockSpec((bm, width), lambda i: (i, 0)),
        out_shape=jax.ShapeDtypeStruct((rows, width), out_dtype),
        compiler_params=_params(("arbitrary",), 40),
        name="rmsnorm",
    )(x, gain.reshape(1, width))


def _mm_kernel(x_ref, w_ref, o_ref):
    o_ref[...] = _dot(x_ref[...].astype(BF16), w_ref[...].astype(BF16)).astype(o_ref.dtype)


def _matmul(x, w, out_dtype, bn, name, col0=0, ncols=None):
    m, k = x.shape
    ncols = w.shape[1] - col0 if ncols is None else ncols
    bm = min(ROW_TILE, m)
    assert m % bm == 0 and ncols % bn == 0 and col0 % bn == 0
    cb0 = col0 // bn
    return pl.pallas_call(
        _mm_kernel,
        grid=(m // bm, ncols // bn),
        in_specs=[pl.BlockSpec((bm, k), lambda i, j: (i, 0)),
                  pl.BlockSpec((k, bn), lambda i, j: (0, cb0 + j))],
        out_specs=pl.BlockSpec((bm, bn), lambda i, j: (i, j)),
        out_shape=jax.ShapeDtypeStruct((m, ncols), out_dtype),
        compiler_params=_params(("arbitrary", "arbitrary"), 56),
        name=name,
    )(x, w)


def _mm_heads_kernel(x_ref, w_ref, o_ref, *, seqs, seq_len, heads):
    acc = _dot(x_ref[...], w_ref[...].astype(BF16))
    for b in range(seqs):
        for hh in range(heads):
            o_ref[b, hh] = acc[b * seq_len:(b + 1) * seq_len, hh * HEAD_DIM:(hh + 1) * HEAD_DIM]


def _matmul_heads(x, w, col0, heads, row0, rows, seq_len, name):
    k = x.shape[1]
    bm = ROW_TILE
    hb = 4
    bn = hb * HEAD_DIM
    assert rows % bm == 0 and row0 % bm == 0 and bm % seq_len == 0 and heads % hb == 0 and col0 % bn == 0
    seqs = bm // seq_len
    rb0, cb0 = row0 // bm, col0 // bn
    return pl.pallas_call(
        functools.partial(_mm_heads_kernel, seqs=seqs, seq_len=seq_len, heads=hb),
        grid=(rows // bm, heads // hb),
        in_specs=[pl.BlockSpec((bm, k), lambda i, j: (rb0 + i, 0)),
                  pl.BlockSpec((k, bn), lambda i, j: (0, cb0 + j))],
        out_specs=pl.BlockSpec((seqs, hb, seq_len, HEAD_DIM), lambda i, j: (i, j, 0, 0)),
        out_shape=jax.ShapeDtypeStruct((rows // seq_len, heads, seq_len, HEAD_DIM), F32),
        compiler_params=_params(("arbitrary", "arbitrary"), 56),
        name=name,
    )(x, w)


def _mm_res_kernel(*refs, n_lhs, n_parts, n_res, n_prompt_blocks):
    pos = 0
    lhs = []
    for n in n_lhs:
        lhs.append(refs[pos:pos + n])
        pos += n
    w_refs = refs[pos:pos + n_parts]
    pos += n_parts
    res = refs[pos:pos + n_res]
    pos += n_res
    gate_ref, o_ref = refs[pos], refs[pos + 1]

    def body(r):
        acc = None
        for x_ref, w_ref in zip(r[:n_parts], w_refs):
            part = _dot(x_ref[...], w_ref[...].astype(BF16))
            acc = part if acc is None else acc + part
        o_ref[...] = r[n_parts][...] + gate_ref[...] * acc

    _by_stream(pl.program_id(0), n_prompt_blocks, lhs + [res], body)


def _matmul_residual(lhs_parts, w, res_parts, mod, gate_slot, bn, n_prompt_rows, lat_len, name,
                     single_buffer_x=False, vmem_mb=56):
    bm = ROW_TILE
    n = w.shape[1]
    kq = w.shape[0] // len(lhs_parts)
    t = sum(r.shape[0] for r in res_parts)
    npb = n_prompt_rows // bm
    cidx = functools.partial(_cond_index, bm=bm, n_prompt_rows=n_prompt_rows, lat_len=lat_len)
    gate_col0 = gate_slot * (D_MODEL // bn)
    in_specs, args = [], []
    for parts in lhs_parts:
        in_specs += _row_specs(parts, bm, kq, lambda j, cb: cb, npb, single_buffer=single_buffer_x)
        args += [a for a, _ in parts]
    for q in range(len(lhs_parts)):
        in_specs.append(pl.BlockSpec((kq, bn), lambda i, j, q=q: (q, j)))
        args.append(w)
    in_specs += _row_specs([(r, 0) for r in res_parts], bm, bn, lambda j, cb: j, npb)
    args += list(res_parts)
    in_specs.append(pl.BlockSpec((None, 1, bn), lambda i, j: (cidx(i), 0, gate_col0 + j)))
    args.append(mod)
    body = functools.partial(_mm_res_kernel, n_lhs=tuple(len(p) for p in lhs_parts), n_parts=len(lhs_parts),
                             n_res=len(res_parts), n_prompt_blocks=npb)
    return pl.pallas_call(
        body,
        grid=(t // bm, n // bn),
        in_specs=in_specs,
        out_specs=pl.BlockSpec((bm, bn), lambda i, j: (i, j)),
        out_shape=jax.ShapeDtypeStruct((t, n), F32),
        compiler_params=_params(("arbitrary", "arbitrary"), vmem_mb),
        name=name,
    )(*args)


def _glu_kernel(*refs, n_y, n_prompt_blocks):
    y_refs, yt_refs = refs[:n_y], refs[n_y:2 * n_y]
    w_ref, b_ref, o_ref = refs[2 * n_y:]

    def body(r):
        z = _dot(r[0][...].astype(BF16), w_ref[...].astype(BF16)) + b_ref[...]
        o_ref[...] = (r[1][...] * _sigmoid(z)).astype(o_ref.dtype)

    _by_stream(pl.program_id(0), n_prompt_blocks, [y_refs, yt_refs], body)


def _glu(y_parts, w, b, bn, n_prompt_rows):
    k, n = w.shape
    bm = ROW_TILE
    m = sum(y.shape[0] for y in y_parts)
    npb = n_prompt_rows // bm
    parts = [(y, 0) for y in y_parts]
    return pl.pallas_call(
        functools.partial(_glu_kernel, n_y=len(y_parts), n_prompt_blocks=npb),
        grid=(m // bm, n // bn),
        in_specs=(_row_specs(parts, bm, k, lambda j, cb: 0, npb, single_buffer=True)
                  + _row_specs(parts, bm, bn, lambda j, cb: j, npb)
                  + [pl.BlockSpec((k, bn), lambda i, j: (0, j)),
                     pl.BlockSpec((1, bn), lambda i, j: (0, j))]),
        out_specs=pl.BlockSpec((bm, bn), lambda i, j: (i, j)),
        out_shape=jax.ShapeDtypeStruct((m, n), BF16),
        compiler_params=_params(("arbitrary", "arbitrary"), 56),
        name="s5_glu",
    )(*y_parts, *y_parts, w, b.reshape(1, n))


def _ffn_up_kernel(x_ref, wg_ref, wv_ref, cwg_ref, cwv_ref, cbg_ref, cbv_ref, o_ref, *,
                   sub, n_sub, prompt_blocks, prompt_len, lat_len):
    i = pl.program_id(0)
    period = jnp.where(i < prompt_blocks, prompt_len, lat_len)
    pos = lax.broadcasted_iota(jnp.int32, (sub, 1), 0) & (period - 1)
    first = pos == 0
    last = pos == period - 1
    wg = wg_ref[...].astype(BF16)
    wv = wv_ref[...].astype(BF16)

    def conv(u, cw_ref, cb_ref):
        prev = jnp.where(first, 0.0, pltpu.roll(u, 1, 0))
        nxt = jnp.where(last, 0.0, pltpu.roll(u, sub - 1, 0))
        return prev * cw_ref[0:1, :] + u * cw_ref[1:2, :] + nxt * cw_ref[2:3, :] + cb_ref[...]

    for s in range(n_sub):
        x = x_ref[s * sub:(s + 1) * sub, :]
        g = conv(_dot(x, wg), cwg_ref, cbg_ref)
        v = conv(_dot(x, wv), cwv_ref, cbv_ref)
        o_ref[s * sub:(s + 1) * sub, :] = (g * _sigmoid(g) * v).astype(o_ref.dtype)


def _ffn_up(h, w_up, conv_w, conv_b, n_prompt_rows, prompt_len, lat_len):
    t, d = h.shape
    sub, n_sub, bn = lat_len, 2, 256
    bm = sub * n_sub
    assert t % bm == 0 and n_prompt_rows % bm == 0 and sub % prompt_len == 0
    assert prompt_len & (prompt_len - 1) == 0 and lat_len & (lat_len - 1) == 0
    nj = D_FF // bn
    conv_b = conv_b.reshape(1, 2 * D_FF)
    body = functools.partial(_ffn_up_kernel, sub=sub, n_sub=n_sub, prompt_blocks=n_prompt_rows // bm,
                             prompt_len=prompt_len, lat_len=lat_len)
    return pl.pallas_call(
        body,
        grid=(t // bm, nj),
        in_specs=[pl.BlockSpec((bm, d), lambda i, j: (i, 0), pipeline_mode=pl.Buffered(1)),
                  pl.BlockSpec((d, bn), lambda i, j: (0, j)),
                  pl.BlockSpec((d, bn), lambda i, j: (0, nj + j)),
                  pl.BlockSpec((3, bn), lambda i, j: (0, j)),
                  pl.BlockSpec((3, bn), lambda i, j: (0, nj + j)),
                  pl.BlockSpec((1, bn), lambda i, j: (0, j)),
                  pl.BlockSpec((1, bn), lambda i, j: (0, nj + j))],
        out_specs=pl.BlockSpec((bm, bn), lambda i, j: (i, j)),
        out_shape=jax.ShapeDtypeStruct((t, D_FF), BF16),
        compiler_params=_params(("arbitrary", "arbitrary"), 56),
        name="ffn_up_conv",
    )(h, w_up, w_up, conv_w, conv_w, conv_b, conv_b)


def _s5_operators(lam_re, lam_im, log_dt, b_re, b_im, c_re, c_im, d_skip):
    hp = lax.Precision.HIGHEST
    q, g, p, c = S5_CHUNK, S5_GROUPS, S5_STATE, S5_GROUP
    dt = jnp.exp(log_dt)[None, :, :, None]
    tau = jnp.arange(q + 1, dtype=F32)[:, None, None, None]
    mag = jnp.exp(lam_re[None] * dt * tau)
    ang = lam_im[None] * dt * tau
    pw_re, pw_im = mag * jnp.cos(ang), mag * jnp.sin(ang)
    dt1 = jnp.exp(log_dt)[:, :, None]
    m1 = jnp.exp(lam_re * dt1)
    ab_re, ab_im = m1 * jnp.cos(lam_im * dt1), m1 * jnp.sin(lam_im * dt1)
    den = lam_re * lam_re + lam_im * lam_im
    nr, ni = ab_re - 1.0, ab_im
    f_re = ((nr * lam_re + ni * lam_im) / den)[..., None]
    f_im = ((ni * lam_re - nr * lam_im) / den)[..., None]
    bb_re = f_re * b_re - f_im * b_im
    bb_im = f_re * b_im + f_im * b_re
    m_re = pw_re[..., None] * bb_re[None] - pw_im[..., None] * bb_im[None]
    m_im = pw_re[..., None] * bb_im[None] + pw_im[..., None] * bb_re[None]
    kk = (jnp.einsum('dgop,tdgpi->tdgio', c_re, m_re, precision=hp)
          - jnp.einsum('dgop,tdgpi->tdgio', c_im, m_im, precision=hp))
    s_idx = jnp.arange(q)[:, None]
    t_idx = jnp.arange(q)[None, :]
    lag_f = t_idx - s_idx
    oh_f = (lag_f[:, :, None] == jnp.arange(q)[None, None, :]).astype(F32)
    oh_b = ((-lag_f)[:, :, None] == jnp.arange(q)[None, None, :]).astype(F32)
    t_f = jnp.einsum('stl,lgio->gsito', oh_f, kk[:q, 0], precision=hp)
    t_b = jnp.einsum('stl,lgio->gsito', oh_b, kk[:q, 1], precision=hp)
    eye_t = jnp.eye(q, dtype=F32)
    eye_c = jnp.eye(c, dtype=F32)
    skip = d_skip[:, None, :, None, None] * eye_t[None, :, None, :, None] * eye_c[None, None, :, None, :]
    t_op = (t_f + t_b + skip).reshape(g, q * c, q * c)
    pf_re, pf_im = pw_re[1:, 0], pw_im[1:, 0]
    pb_re, pb_im = pw_re[1:, 1][::-1], pw_im[1:, 1][::-1]

    def e_op(cr, ci, pr, pi):
        e_r = jnp.einsum('gop,tgp->gpto', cr, pr) - jnp.einsum('gop,tgp->gpto', ci, pi)
        e_i = -(jnp.einsum('gop,tgp->gpto', cr, pi) + jnp.einsum('gop,tgp->gpto', ci, pr))
        return e_r.reshape(g, p, q * c), e_i.reshape(g, p, q * c)

    ef_r, ef_i = e_op(c_re[0], c_im[0], pf_re, pf_im)
    eb_r, eb_i = e_op(c_re[1], c_im[1], pb_re, pb_im)
    zero = jnp.zeros_like(ef_r)
    te = jnp.concatenate([t_op, ef_r, zero, ef_i, zero, zero, eb_r, zero, eb_i], axis=1)
    sf_re = jnp.transpose(m_re[:q, 0][::-1], (1, 0, 3, 2)).reshape(g, q * c, p)
    sf_im = jnp.transpose(m_im[:q, 0][::-1], (1, 0, 3, 2)).reshape(g, q * c, p)
    sb_re = jnp.transpose(m_re[:q, 1], (1, 0, 3, 2)).reshape(g, q * c, p)
    sb_im = jnp.transpose(m_im[:q, 1], (1, 0, 3, 2)).reshape(g, q * c, p)
    sb = jnp.concatenate([sf_re, sb_re, sf_im, sb_im], axis=2)
    a16 = jnp.concatenate([pw_re[q, 0], pw_re[q, 1], pw_im[q, 0], pw_im[q, 1]], axis=-1)[:, None, :]
    return sb.astype(BF16), te.astype(BF16), a16


def _gelu_tanh(x):
    return 0.5 * x * (1.0 + jnp.tanh(math.sqrt(2.0 / math.pi) * (x + 0.044715 * (x * x * x))))


GROUPS_PER_STEP = LANE // S5_GROUP
STATE_W = GROUPS_PER_STEP * 2 * S5_STATE


def _s5_lane_permutation():
    j = jnp.arange(S5_CHUNK * LANE)
    s, gl, c = j // LANE, (j % LANE) // S5_GROUP, j % S5_GROUP
    k = gl * (S5_CHUNK * S5_GROUP) + s * S5_GROUP + c
    return (k[:, None] == jnp.arange(S5_CHUNK * LANE)[None, :]).astype(BF16)


def _s5_kernel(u_ref, sel_ref, sb_ref, te_ref, a_ref, h0_ref, y_ref, hfin_ref, xg, w_scr, h_scr, *,
               batch, nb, nc, seq_len):
    qc = S5_CHUNK * S5_GROUP
    if batch < nb:
        xg[...] = jnp.zeros(xg.shape, BF16)
    for ch in range(nc):
        for s in range(S5_CHUNK):
            xg[ch * nb:ch * nb + batch, s * LANE:(s + 1) * LANE] = (
                u_ref[pl.ds(ch * S5_CHUNK + s, batch, stride=seq_len), :].astype(BF16))
    sel = sel_ref[...]
    ucat = _dot(xg[...], sel).astype(BF16)
    for gl in range(GROUPS_PER_STEP):
        w = _dot(ucat[:, gl * qc:(gl + 1) * qc], sb_ref[gl])
        w_scr[:, gl * LANE:(gl + 1) * LANE] = w[:, 0:LANE]
        w_scr[:, STATE_W + gl * LANE:STATE_W + (gl + 1) * LANE] = w[:, LANE:2 * LANE]
    ar = a_ref[0:1, 0:STATE_W]
    ai = a_ref[0:1, STATE_W:2 * STATE_W]
    fwd_lane = (lax.broadcasted_iota(jnp.int32, (1, STATE_W), 1) & S5_STATE) == 0
    hr = h0_ref[:, 0:STATE_W]
    hi = h0_ref[:, STATE_W:2 * STATE_W]
    for i in range(nc):
        ri = i * nb
        rj = (nc - 1 - i) * nb
        h_scr[ri:ri + nb, 0:STATE_W] = hr
        h_scr[ri:ri + nb, STATE_W:2 * STATE_W] = hi
        h_scr[rj:rj + nb, 2 * STATE_W:3 * STATE_W] = hr
        h_scr[rj:rj + nb, 3 * STATE_W:4 * STATE_W] = hi
        wr = jnp.where(fwd_lane, w_scr[ri:ri + nb, 0:STATE_W], w_scr[rj:rj + nb, 0:STATE_W])
        wi = jnp.where(fwd_lane, w_scr[ri:ri + nb, STATE_W:2 * STATE_W], w_scr[rj:rj + nb, STATE_W:2 * STATE_W])
        hr, hi = ar * hr - ai * hi + wr, ar * hi + ai * hr + wi
    hfin_ref[:, 0:STATE_W] = hr
    hfin_ref[:, STATE_W:2 * STATE_W] = hi
    for gl in range(GROUPS_PER_STEP):
        hcat = jnp.concatenate([h_scr[:, k * STATE_W + gl * LANE:k * STATE_W + (gl + 1) * LANE] for k in range(4)],
                               axis=1).astype(BF16)
        y = _dot(ucat[:, gl * qc:(gl + 1) * qc], te_ref[gl, 0:qc, :]) + _dot(hcat, te_ref[gl, qc:, :])
        xg[:, gl * qc:(gl + 1) * qc] = _gelu_tanh(y).astype(BF16)
    yp = _dot_t(xg[...], sel)
    for ch in range(nc):
        for s in range(S5_CHUNK):
            y_ref[pl.ds(ch * S5_CHUNK + s, batch, stride=seq_len), :] = (
                yp[ch * nb:ch * nb + batch, s * LANE:(s + 1) * LANE])


def _s5_stream(u, row_block, batch, seq_len, sel, sb, te, a_planes, h0):
    n_rows = batch * seq_len
    nb = -(-batch // SUBLANE) * SUBLANE
    nc = seq_len // S5_CHUNK
    rows = nb * nc
    nblk = S5_WIDTH // LANE
    qc = S5_CHUNK * S5_GROUP
    body = functools.partial(_s5_kernel, batch=batch, nb=nb, nc=nc, seq_len=seq_len)
    return pl.pallas_call(
        body,
        grid=(nblk,),
        in_specs=[pl.BlockSpec((n_rows, LANE), lambda g: (row_block, g), pipeline_mode=pl.Buffered(1)),
                  pl.BlockSpec((S5_CHUNK * LANE, S5_CHUNK * LANE), lambda g: (0, 0), pipeline_mode=pl.Buffered(1)),
                  pl.BlockSpec((GROUPS_PER_STEP, qc, 4 * S5_STATE), lambda g: (g, 0, 0)),
                  pl.BlockSpec((GROUPS_PER_STEP, qc + 8 * S5_STATE, qc), lambda g: (g, 0, 0)),
                  pl.BlockSpec((None, 1, 2 * STATE_W), lambda g: (g, 0, 0)),
                  pl.BlockSpec((None, nb, 2 * STATE_W), lambda g: (g, 0, 0))],
        out_specs=[pl.BlockSpec((n_rows, LANE), lambda g: (0, g)),
                   pl.BlockSpec((None, nb, 2 * STATE_W), lambda g: (g, 0, 0))],
        out_shape=[jax.ShapeDtypeStruct((n_rows, S5_WIDTH), F32),
                   jax.ShapeDtypeStruct((nblk, nb, 2 * STATE_W), F32)],
        scratch_shapes=[pltpu.VMEM((rows, S5_CHUNK * LANE), BF16),
                        pltpu.VMEM((rows, 2 * STATE_W), F32),
                        pltpu.VMEM((rows, 4 * STATE_W), F32)],
        compiler_params=_params(("arbitrary",), 56),
        name="s5_chunked",
    )(u, sel, sb, te, a_planes, h0)


def _rope_tables(length, rot):
    n_freq = rot // 4
    t = jnp.arange(length)
    row = (t // GRID_W).astype(F32)
    col = (t % GRID_W).astype(F32)
    inv = ROPE_BASE ** (-jnp.arange(n_freq, dtype=F32) / n_freq)
    ar, ac = row[:, None] * inv, col[:, None] * inv
    cos = jnp.concatenate([jnp.cos(ar), jnp.cos(ar), jnp.cos(ac), jnp.cos(ac)], axis=-1)
    sin = jnp.concatenate([-jnp.sin(ar), jnp.sin(ar), -jnp.sin(ac), jnp.sin(ac)], axis=-1)
    pad = LANE - rot
    if pad:
        cos = jnp.concatenate([cos, jnp.ones((length, pad), F32)], axis=-1)
        sin = jnp.concatenate([sin, jnp.zeros((length, pad), F32)], axis=-1)
    return cos, sin


def _rope(x, cos, sin, blk):
    lane = lax.broadcasted_iota(jnp.int32, (1, LANE), 1)
    lower = (lane & blk) == 0
    partner = jnp.where(lower, pltpu.roll(x, LANE - blk, 1), pltpu.roll(x, blk, 1))
    return x * cos + partner * sin


def _softmax_pv(scores, values, extra_logit=None):
    m = scores[0].max(axis=-1, keepdims=True)
    for s in scores[1:]:
        m = jnp.maximum(m, s.max(axis=-1, keepdims=True))
    if extra_logit is not None:
        m = jnp.maximum(m, extra_logit)
    den = None
    out = None
    for s, v in zip(scores, values):
        p = jnp.exp(s - m)
        d = p.sum(axis=-1, keepdims=True)
        o = _dot(p.astype(BF16), v)
        den = d if den is None else den + d
        out = o if out is None else out + o
    if extra_logit is not None:
        den = den + jnp.exp(extra_logit - m)
    return out / den


def _ctx_gqa_kernel(sink_ref, q_ref, k_ref, v_ref, o_ref):
    g = pl.program_id(1)
    scale = HEAD_DIM ** -0.5
    k = k_ref[...].astype(BF16)
    v = v_ref[...].astype(BF16)
    for r in range(WIN_GROUP):
        q = q_ref[:, r * HEAD_DIM:(r + 1) * HEAD_DIM]
        s = _dot_t(q, k) * scale
        o = _softmax_pv([s], [v], sink_ref[g * WIN_GROUP + r])
        o_ref[:, r * HEAD_DIM:(r + 1) * HEAD_DIM] = o.astype(o_ref.dtype)


def _ctx_gqa_attention(uq3, k4, v4, sink, n_prompt, length):
    qw = WIN_GROUP * HEAD_DIM
    q0 = 0
    kv_spec = pl.BlockSpec((None, None, length, HEAD_DIM), lambda b, g: (b, g, 0, 0))
    return pl.pallas_call(
        _ctx_gqa_kernel,
        grid=(n_prompt, WIN_KV_HEADS),
        in_specs=[pl.BlockSpec(memory_space=pltpu.SMEM),
                  pl.BlockSpec((None, length, qw), lambda b, g: (b, 0, q0 + g)),
                  kv_spec, kv_spec],
        out_specs=pl.BlockSpec((None, length, qw), lambda b, g: (b, 0, g)),
        out_shape=jax.ShapeDtypeStruct((n_prompt, length, WIN_Q_WIDTH), BF16),
        compiler_params=_params(("arbitrary", "arbitrary"), 40),
        name="ctx_gqa_attention",
    )(sink, uq3, k4, v4)


def _lat_window_kernel(sink_ref, q_ref, k_ref, v_ref, kc_ref, vc_ref, cos_ref, sin_ref, o_ref, *, length):
    g = pl.program_id(1)
    scale = HEAD_DIM ** -0.5
    blk = WIN_RADIUS
    nb = length // blk
    k = _rope(k_ref[...], cos_ref[...], sin_ref[...], HEAD_DIM // 4).astype(BF16)
    v = v_ref[...].astype(BF16)
    kc = kc_ref[...].astype(BF16)
    vc = vc_ref[...].astype(BF16)
    rows = WIN_GROUP * blk
    row = lax.broadcasted_iota(jnp.int32, (rows, 1), 0)
    sk = jnp.zeros((rows, 1), F32)
    for r in range(WIN_GROUP):
        sk = jnp.where(jnp.logical_and(row >= r * blk, row < (r + 1) * blk), sink_ref[g * WIN_GROUP + r], sk)
    qoff = row & (blk - 1)
    for n in range(nb):
        lo = max(0, n - 1) * blk
        hi = min(nb, n + 2) * blk
        cos = cos_ref[n * blk:(n + 1) * blk, :]
        sin = sin_ref[n * blk:(n + 1) * blk, :]
        q = jnp.concatenate(
            [_rope(q_ref[n * blk:(n + 1) * blk, r * HEAD_DIM:(r + 1) * HEAD_DIM].astype(F32), cos, sin, HEAD_DIM // 4)
             for r in range(WIN_GROUP)], axis=0).astype(BF16)
        dist = (n * blk + qoff) - (lo + lax.broadcasted_iota(jnp.int32, (1, hi - lo), 1))
        visible = jnp.logical_and(dist <= WIN_RADIUS, dist >= -WIN_RADIUS)
        s_loc = jnp.where(visible, _dot_t(q, k[lo:hi]) * scale, NEG)
        s_ctx = _dot_t(q, kc) * scale
        o = _softmax_pv([s_loc, s_ctx], [v[lo:hi], vc], sk)
        for r in range(WIN_GROUP):
            o_ref[n * blk:(n + 1) * blk, r * HEAD_DIM:(r + 1) * HEAD_DIM] = o[r * blk:(r + 1) * blk].astype(o_ref.dtype)


def _lat_window_attention(uq3, seq0, k4, v4, k_ctx, v_ctx, sink, cos, sin):
    n_lat, _, length, _ = k4.shape
    qw = WIN_GROUP * HEAD_DIM
    q0 = 0
    lc = k_ctx.shape[2]
    kv_spec = pl.BlockSpec((None, None, length, HEAD_DIM), lambda b, g: (b, g, 0, 0))
    ctx_spec = pl.BlockSpec((None, None, lc, HEAD_DIM), lambda b, g: (b, g, 0, 0))
    tab_spec = pl.BlockSpec((length, LANE), lambda b, g: (0, 0))
    return pl.pallas_call(
        functools.partial(_lat_window_kernel, length=length),
        grid=(n_lat, WIN_KV_HEADS),
        in_specs=[pl.BlockSpec(memory_space=pltpu.SMEM),
                  pl.BlockSpec((None, length, qw), lambda b, g: (seq0 + b, 0, q0 + g)),
                  kv_spec, kv_spec, ctx_spec, ctx_spec, tab_spec, tab_spec],
        out_specs=pl.BlockSpec((None, length, qw), lambda b, g: (b, 0, g)),
        out_shape=jax.ShapeDtypeStruct((n_lat, length, WIN_Q_WIDTH), BF16),
        compiler_params=_params(("arbitrary", "arbitrary"), 48),
        name="latent_window_attention",
    )(sink, uq3, k4, v4, k_ctx, v_ctx, cos, sin)


def _ctx_odd_kernel(q_ref, kv_ref, kpe_ref, qd_ref, kd_ref, vd_ref, o_ref):
    kw = MLA_NOPE + LANE
    kpe = kpe_ref[...].astype(BF16)
    scale_c = (MLA_NOPE + MLA_ROPE) ** -0.5
    for h in range(MLA_HEADS):
        q = q_ref[:, h * kw:(h + 1) * kw]
        k = jnp.concatenate([kv_ref[:, h * kw:h * kw + MLA_NOPE], kpe], axis=1)
        v = kv_ref[:, h * kw + MLA_NOPE:(h + 1) * kw]
        o = _softmax_pv([_dot_t(q, k) * scale_c], [v])
        o_ref[:, h * MLA_V:(h + 1) * MLA_V] = o.astype(o_ref.dtype)
    scale_d = HEAD_DIM ** -0.5
    base = MLA_HEADS * MLA_V
    for h in range(NA_HEADS):
        q = qd_ref[:, h * HEAD_DIM:(h + 1) * HEAD_DIM]
        o = _softmax_pv([_dot_t(q, kd_ref[h].astype(BF16)) * scale_d], [vd_ref[h].astype(BF16)])
        o_ref[:, base + h * HEAD_DIM:base + (h + 1) * HEAD_DIM] = o.astype(o_ref.dtype)


def _ctx_odd_attention(q3, kv3, kpe3, qd3, kd4, vd4, n_prompt, length):
    qn = q3.shape[-1]
    head_spec = pl.BlockSpec((None, NA_HEADS, length, HEAD_DIM), lambda b: (b, 0, 0, 0))
    return pl.pallas_call(
        _ctx_odd_kernel,
        grid=(n_prompt,),
        in_specs=[pl.BlockSpec((None, length, qn), lambda b: (b, 0, 0)),
                  pl.BlockSpec((None, length, qn), lambda b: (b, 0, 0)),
                  pl.BlockSpec((None, length, LANE), lambda b: (b, 0, 0)),
                  pl.BlockSpec((None, length, NA_WIDTH), lambda b: (b, 0, 0)),
                  head_spec, head_spec],
        out_specs=pl.BlockSpec((None, length, MLA_HEADS * MLA_V + NA_WIDTH), lambda b: (b, 0, 0)),
        out_shape=jax.ShapeDtypeStruct((n_prompt, length, MLA_HEADS * MLA_V + NA_WIDTH), BF16),
        compiler_params=_params(("arbitrary",), 48),
        name="ctx_odd_attention",
    )(q3, kv3, kpe3, qd3, kd4, vd4)


def _lat_mla_kernel(q_ref, kv_ref, kpe_ref, kvc_ref, kpec_ref, cos_ref, sin_ref, o_ref, *, length):
    scale = (MLA_NOPE + MLA_ROPE) ** -0.5
    rb = MLA_ROPE // 4
    kpe = _rope(kpe_ref[...], cos_ref[...], sin_ref[...], rb).astype(BF16)
    k_lat = jnp.concatenate([kv_ref[:, 0:MLA_NOPE], kpe], axis=1)
    v_lat = kv_ref[:, MLA_NOPE:]
    k_ctx = jnp.concatenate([kvc_ref[:, 0:MLA_NOPE], kpec_ref[...].astype(BF16)], axis=1)
    v_ctx = kvc_ref[:, MLA_NOPE:]
    qb = 256
    for n in range(length // qb):
        rows = slice(n * qb, (n + 1) * qb)
        q_pe = _rope(q_ref[rows, MLA_NOPE:].astype(F32), cos_ref[rows, :], sin_ref[rows, :], rb).astype(BF16)
        q = jnp.concatenate([q_ref[rows, 0:MLA_NOPE], q_pe], axis=1)
        o = _softmax_pv([_dot_t(q, k_lat) * scale, _dot_t(q, k_ctx) * scale], [v_lat, v_ctx])
        o_ref[rows, :] = o.astype(o_ref.dtype)


def _lat_mla_attention(q3, kv3, kvc3, kpe3, kpe_ctx, seq0, n_lat, length, cos, sin):
    kw = MLA_NOPE + LANE
    lc = kpe_ctx.shape[1]
    return pl.pallas_call(
        functools.partial(_lat_mla_kernel, length=length),
        grid=(n_lat, MLA_HEADS),
        in_specs=[pl.BlockSpec((None, length, kw), lambda b, h: (seq0 + b, 0, h)),
                  pl.BlockSpec((None, length, kw), lambda b, h: (seq0 + b, 0, h)),
                  pl.BlockSpec((None, length, LANE), lambda b, h: (seq0 + b, 0, 0)),
                  pl.BlockSpec((None, lc, kw), lambda b, h: (b, 0, h)),
                  pl.BlockSpec((None, lc, LANE), lambda b, h: (b, 0, 0)),
                  pl.BlockSpec((length, LANE), lambda b, h: (0, 0)),
                  pl.BlockSpec((length, LANE), lambda b, h: (0, 0))],
        out_specs=pl.BlockSpec((None, length, MLA_V), lambda b, h: (b, 0, h)),
        out_shape=jax.ShapeDtypeStruct((n_lat, length, MLA_HEADS * MLA_V), BF16),
        compiler_params=_params(("arbitrary", "arbitrary"), 48),
        name="latent_mla_attention",
    )(q3, kv3, kpe3, kvc3, kpe_ctx, cos, sin)


def _na_row_start(r, rows):
    kr = min(NA_ROWS, rows)
    return min(max(r - kr // 2, 0), rows - kr)


def _lat_na_kernel(q_ref, k_ref, v_ref, kc_ref, vc_ref, bias_ref, o_ref, *, length):
    scale = HEAD_DIM ** -0.5
    rows = length // GRID_W
    kr = min(NA_ROWS, rows)
    k = k_ref[...].astype(BF16)
    v = v_ref[...].astype(BF16)
    kc = kc_ref[...].astype(BF16)
    vc = vc_ref[...].astype(BF16)
    for r in range(rows):
        r0 = _na_row_start(r, rows) * GRID_W
        q = q_ref[r * GRID_W:(r + 1) * GRID_W, :]
        s_nb = _dot_t(q, k[r0:r0 + kr * GRID_W]) * scale + bias_ref[r]
        s_ctx = _dot_t(q, kc) * scale
        o = _softmax_pv([s_nb, s_ctx], [v[r0:r0 + kr * GRID_W], vc])
        o_ref[r * GRID_W:(r + 1) * GRID_W, :] = o.astype(o_ref.dtype)


def _na_bias(rpb, length):
    rows = length // GRID_W
    kr = min(NA_ROWS, rows)
    col = jnp.arange(GRID_W)
    c_start = jnp.clip(col - NA_COLS // 2, 0, GRID_W - NA_COLS)
    col_valid = (col[None, :] >= c_start[:, None]) & (col[None, :] < c_start[:, None] + NA_COLS)
    off_c = jnp.clip(col[None, :] - col[:, None], -(NA_COLS - 1), NA_COLS - 1) + NA_COLS - 1
    onehot = (off_c[:, :, None] == jnp.arange(2 * NA_COLS - 1)[None, None, :]).astype(F32)
    table = jnp.einsum('hdj,qkj->hdqk', rpb.astype(F32), onehot, precision=lax.Precision.HIGHEST)
    table = jnp.where(col_valid[None, None], table, NEG)
    per_row = []
    for r in range(rows):
        r0 = _na_row_start(r, rows)
        per_row.append(jnp.concatenate([table[:, r0 + j - r + NA_ROWS - 1] for j in range(kr)], axis=-1))
    return jnp.stack(per_row, axis=1)


def _lat_na_attention(qd3, seq0, k4, v4, k_ctx, v_ctx, bias):
    n_lat, _, length, _ = k4.shape
    rows = length // GRID_W
    kr = min(NA_ROWS, rows)
    lc = k_ctx.shape[2]
    kv_spec = pl.BlockSpec((None, None, length, HEAD_DIM), lambda h, b: (b, h, 0, 0))
    ctx_spec = pl.BlockSpec((None, None, lc, HEAD_DIM), lambda h, b: (b, h, 0, 0))
    return pl.pallas_call(
        functools.partial(_lat_na_kernel, length=length),
        grid=(NA_HEADS, n_lat),
        in_specs=[pl.BlockSpec((None, length, HEAD_DIM), lambda h, b: (seq0 + b, 0, h)),
                  kv_spec, kv_spec, ctx_spec, ctx_spec,
                  pl.BlockSpec((None, rows, GRID_W, kr * GRID_W), lambda h, b: (h, 0, 0, 0))],
        out_specs=pl.BlockSpec((None, length, HEAD_DIM), lambda h, b: (b, 0, h)),
        out_shape=jax.ShapeDtypeStruct((n_lat, length, NA_WIDTH), BF16),
        compiler_params=_params(("arbitrary", "arbitrary"), 48),
        name="latent_neighborhood_attention",
    )(qd3, k4, v4, k_ctx, v_ctx, bias)


def kernel(x_prompt, x_sample, cache_l0_k, cache_l0_v, state_l0_re, state_l0_im, cache_l1_ckv, cache_l1_kpe, cache_l1_k, cache_l1_v, c, c_ctx, l0_ada_w, l0_ada_b, l0_norm1, l0_norm2, l0_w_in, l0_lambda_re, l0_lambda_im, l0_log_dt, l0_b_re, l0_b_im, l0_c_re, l0_c_im, l0_d_skip, l0_w_glu, l0_b_glu, l0_sink, l0_w_out, l0_ffn_w_up, l0_ffn_conv_w, l0_ffn_conv_b, l0_ffn_w_down, l1_ada_w, l1_ada_b, l1_norm1, l1_norm2, l1_w_in, l1_q_norm, l1_kv_norm, l1_w_uq, l1_w_ukv, l1_rpb, l1_w_out, l1_ffn_w_up, l1_ffn_conv_w, l1_ffn_conv_b, l1_ffn_w_down, final_norm):
    n_p, len_p, d = x_prompt.shape
    n_s, len_s, _ = x_sample.shape
    rows_p, rows_s = n_p * len_p, n_s * len_s
    t = rows_p + rows_s
    assert d == D_MODEL and len_s == ROW_TILE and ROW_TILE % len_p == 0 and n_s + 1 <= MAX_CONDS
    assert rows_p % (2 * ROW_TILE) == 0 and rows_s % (2 * ROW_TILE) == 0
    assert len_p % S5_CHUNK == 0 and len_s % (2 * WIN_RADIUS) == 0 and len_s % GRID_W == 0
    seqs_p_units = t // len_p
    seqs_s_units = t // len_s
    seq0_s = rows_p // len_s
    groups = dict(n_prompt_rows=rows_p, lat_len=len_s)
    half = D_MODEL // 2

    cond = jnp.zeros((MAX_CONDS, d), F32).at[0].set(c_ctx).at[1:1 + n_s].set(c)
    mod0 = _modulation(cond, l0_ada_w, l0_ada_b)
    mod1 = _modulation(cond, l1_ada_w, l1_ada_b)

    x_parts = [x_prompt.reshape(rows_p, d), x_sample.reshape(rows_s, d)]

    def ffn(x, mod, norm2, w_up, conv_w, conv_b, w_down):
        h = _norm_mod([x], norm2, mod, 3, 4, **groups)
        act = _ffn_up(h, w_up, conv_w, conv_b, rows_p, len_p, len_s)
        return _matmul_residual([[(act, 0)]], w_down.astype(BF16), [x], mod, 5, 256, name="ffn_down",
                                single_buffer_x=True, **groups)

    h = _norm_mod(x_parts, l0_norm1, mod0, 0, 1, **groups)
    u = _matmul(h, l0_w_in, F32, 512, "l0_in_proj_u", 0, S5_WIDTH)
    q0 = _matmul(h, l0_w_in, BF16, 512, "l0_in_proj_q", S5_WIDTH, WIN_Q_WIDTH)
    kcol = S5_WIDTH + WIN_Q_WIDTH
    vcol = kcol + WIN_KV_WIDTH
    out_k0 = _matmul_heads(h, l0_w_in, kcol, WIN_KV_HEADS, 0, rows_p, len_p, "l0_in_proj_k_ctx")
    out_v0 = _matmul_heads(h, l0_w_in, vcol, WIN_KV_HEADS, 0, rows_p, len_p, "l0_in_proj_v_ctx")
    k0_s = _matmul_heads(h, l0_w_in, kcol, WIN_KV_HEADS, rows_p, rows_s, len_s, "l0_in_proj_k_lat")
    v0_s = _matmul_heads(h, l0_w_in, vcol, WIN_KV_HEADS, rows_p, rows_s, len_s, "l0_in_proj_v_lat")

    assert rows_p % rows_s == 0
    sel = _s5_lane_permutation()
    sb, te, a16 = _s5_operators(l0_lambda_re, l0_lambda_im, l0_log_dt, l0_b_re, l0_b_im, l0_c_re, l0_c_im, l0_d_skip)
    nblk, gps, pw = S5_WIDTH // LANE, GROUPS_PER_STEP, 2 * S5_STATE

    def planes(cols):
        b = cols.shape[1]
        c5 = cols.reshape(nblk, gps, b, 2, pw).transpose(0, 2, 3, 1, 4)
        return c5.reshape(nblk, b, 2 * STATE_W)

    def state_cols(s):
        return s.astype(F32).transpose(2, 0, 1, 3).reshape(S5_GROUPS, s.shape[0], pw)

    np8, ns8 = -(-n_p // SUBLANE) * SUBLANE, -(-n_s // SUBLANE) * SUBLANE
    a_planes = planes(a16)
    h0_lat = planes(jnp.concatenate([state_cols(state_l0_re), state_cols(state_l0_im)], axis=-1))
    h0_s = jnp.pad(h0_lat, ((0, 0), (0, ns8 - n_s), (0, 0)))
    h0_p = jnp.zeros((nblk, np8, 2 * STATE_W), F32)
    y_p, hfin = _s5_stream(u, 0, n_p, len_p, sel, sb, te, a_planes, h0_p)
    y_s, _ = _s5_stream(u, rows_p // rows_s, n_s, len_s, sel, sb, te, a_planes, h0_s)
    a_out = _glu([y_p, y_s], l0_w_glu, l0_b_glu, 512, rows_p)

    def state_out(plane):
        return (plane[:, :n_p].reshape(nblk, n_p, gps, 2, S5_STATE).transpose(1, 3, 0, 2, 4)
                .reshape(n_p, 2, S5_GROUPS, S5_STATE))

    out_sre = state_out(hfin[:, :, :STATE_W])
    out_sim = state_out(hfin[:, :, STATE_W:])

    cos_e, sin_e = _rope_tables(len_s, HEAD_DIM)
    o_p = _ctx_gqa_attention(q0.reshape(seqs_p_units, len_p, -1), out_k0, out_v0, l0_sink, n_p, len_p)
    o_s = _lat_window_attention(q0.reshape(seqs_s_units, len_s, -1), seq0_s, k0_s, v0_s,
                                cache_l0_k, cache_l0_v, l0_sink, cos_e, sin_e)
    x = _matmul_residual([[(a_out, 0)], [(o_p.reshape(rows_p, half), 0), (o_s.reshape(rows_s, half), 0)]],
                         l0_w_out, x_parts, mod0, 2, 512, name="l0_out_proj", single_buffer_x=True, **groups)
    x = ffn(x, mod0, l0_norm2, l0_ffn_w_up, l0_ffn_conv_w, l0_ffn_conv_b, l0_ffn_w_down)

    s1, s2 = MLA_Q_LORA + MLA_KV_LORA, MLA_Q_LORA + MLA_KV_LORA + MLA_ROPE
    w_lora = l1_w_in[:, :s1].astype(BF16)
    w_dkv = l1_w_in[:, s2:].astype(BF16)
    w_kpe = jnp.pad(l1_w_in[:, s1:s2], ((0, 0), (0, LANE - MLA_ROPE))).astype(BF16)
    w_uq = l1_w_uq.reshape(MLA_Q_LORA, MLA_HEADS, MLA_NOPE + MLA_ROPE)
    w_uq = jnp.pad(w_uq, ((0, 0), (0, 0), (0, LANE - MLA_ROPE))).reshape(MLA_Q_LORA, -1).astype(BF16)

    h = _norm_mod([x], l1_norm1, mod1, 0, 1, **groups)
    cqkv = _matmul(h, w_lora, F32, 512, "l1_in_proj_lora")
    kpe = _matmul(h, w_kpe, F32, LANE, "l1_in_proj_kpe")
    qd = _matmul(h, w_dkv, BF16, 512, "l1_in_proj_qd", 0, NA_WIDTH)
    out_k1 = _matmul_heads(h, w_dkv, NA_WIDTH, NA_HEADS, 0, rows_p, len_p, "l1_in_proj_kd_ctx")
    out_v1 = _matmul_heads(h, w_dkv, 2 * NA_WIDTH, NA_HEADS, 0, rows_p, len_p, "l1_in_proj_vd_ctx")
    k1_s = _matmul_heads(h, w_dkv, NA_WIDTH, NA_HEADS, rows_p, rows_s, len_s, "l1_in_proj_kd_lat")
    v1_s = _matmul_heads(h, w_dkv, 2 * NA_WIDTH, NA_HEADS, rows_p, rows_s, len_s, "l1_in_proj_vd_lat")
    cqn = _rmsnorm_cols(cqkv, l1_q_norm, 0, MLA_Q_LORA, BF16)
    ckvn = _rmsnorm_cols(cqkv, l1_kv_norm, MLA_Q_LORA // MLA_KV_LORA, MLA_KV_LORA, F32)
    q_all = _matmul(cqn, w_uq, BF16, 512, "l1_q_up")
    kv_all = _matmul(ckvn, l1_w_ukv, BF16, 512, "l1_kv_up")
    lc = cache_l1_ckv.shape[1]
    kv_ctx = _matmul(cache_l1_ckv.reshape(-1, MLA_KV_LORA), l1_w_ukv, BF16, 512, "l1_kv_up_ctx")

    out_ckv = ckvn[:rows_p].reshape(n_p, len_p, MLA_KV_LORA)
    out_kpe = kpe[:rows_p, :MLA_ROPE].reshape(n_p, len_p, MLA_ROPE)

    o_p = _ctx_odd_attention(q_all.reshape(seqs_p_units, len_p, -1),
                             kv_all.reshape(seqs_p_units, len_p, -1),
                             kpe.reshape(seqs_p_units, len_p, LANE),
                             qd.reshape(seqs_p_units, len_p, -1), out_k1, out_v1, n_p, len_p)
    cos_o, sin_o = _rope_tables(len_s, MLA_ROPE)
    kpe_ctx = jnp.pad(cache_l1_kpe, ((0, 0), (0, 0), (0, LANE - MLA_ROPE)))
    oc_s = _lat_mla_attention(q_all.reshape(seqs_s_units, len_s, -1),
                              kv_all.reshape(seqs_s_units, len_s, -1),
                              kv_ctx.reshape(n_s, lc, -1),
                              kpe.reshape(seqs_s_units, len_s, LANE), kpe_ctx,
                              seq0_s, n_s, len_s, cos_o, sin_o)
    od_s = _lat_na_attention(qd.reshape(seqs_s_units, len_s, -1), seq0_s, k1_s, v1_s,
                             cache_l1_k, cache_l1_v, _na_bias(l1_rpb, len_s))
    o_p2 = o_p.reshape(rows_p, 2 * half)
    x = _matmul_residual([[(o_p2, 0), (oc_s.reshape(rows_s, half), 0)],
                          [(o_p2, 1), (od_s.reshape(rows_s, half), 0)]],
                         l1_w_out, [x], mod1, 2, 512, name="l1_out_proj", single_buffer_x=True, **groups)
    x = ffn(x, mod1, l1_norm2, l1_ffn_w_up, l1_ffn_conv_w, l1_ffn_conv_b, l1_ffn_w_down)

    y_prompt = _rmsnorm_cols(x, final_norm, 0, d, F32, 0, rows_p).reshape(n_p, len_p, d)
    y_sample = _rmsnorm_cols(x, final_norm, 0, d, F32, rows_p, rows_s).reshape(n_s, len_s, d)
    return (y_prompt, y_sample, out_k0, out_v0, out_sre, out_sim, out_ckv, out_kpe, out_k1, out_v1)
```

```python
import functools
import math

import jax
import jax.numpy as jnp
from jax import lax
from jax.experimental import pallas as pl
from jax.experimental.pallas import tpu as pltpu

F32 = jnp.float32
BF16 = jnp.bfloat16

D_MODEL = 4096
GRID_W = 64
HEAD_DIM = 128
S5_WIDTH = 2048
S5_GROUP = 16
S5_GROUPS = S5_WIDTH // S5_GROUP
S5_STATE = 64
S5_CHUNK = 16
WIN_HEADS = 16
WIN_KV_HEADS = 4
WIN_GROUP = WIN_HEADS // WIN_KV_HEADS
WIN_RADIUS = 128
WIN_Q_WIDTH = WIN_HEADS * HEAD_DIM
WIN_KV_WIDTH = WIN_KV_HEADS * HEAD_DIM
MLA_HEADS = 16
MLA_Q_LORA = 1024
MLA_KV_LORA = 512
MLA_NOPE = 128
MLA_ROPE = 64
MLA_V = 128
NA_HEADS = 16
NA_ROWS = 8
NA_COLS = 16
NA_WIDTH = NA_HEADS * HEAD_DIM
D_FF = 11008
ROPE_BASE = 10000.0
EPS = 1e-6
NEG = -1e30

LANE = 128
SUBLANE = 8
MAX_CONDS = 8
ROW_TILE = 1024


def _params(sem, vmem_mb):
    return pltpu.CompilerParams(dimension_semantics=sem, vmem_limit_bytes=vmem_mb * 1024 * 1024)


def _cond_index(i, bm, n_prompt_rows, lat_len):
    first = n_prompt_rows // bm
    per = lat_len // bm
    return jnp.where(i < first, 0, 1 + (i - first) // per)


def _dot(a, b):
    return jnp.dot(a, b, preferred_element_type=F32)


def _dot_t(a, b):
    return lax.dot_general(a, b, (((1,), (1,)), ((), ())), preferred_element_type=F32)


def _sigmoid(x):
    return 1.0 / (1.0 + jnp.exp(-x))


def _row_specs(parts, bm, width, col_fn, n_prompt_blocks, single_buffer=False):
    mode = dict(pipeline_mode=pl.Buffered(1)) if single_buffer else {}
    if len(parts) == 1:
        cb = parts[0][1]
        return [pl.BlockSpec((bm, width), lambda i, j: (i, col_fn(j, cb)), **mode)]
    cb_p, cb_s = parts[0][1], parts[1][1]
    last_p = n_prompt_blocks - 1
    return [pl.BlockSpec((bm, width), lambda i, j: (jnp.minimum(i, last_p), col_fn(j, cb_p)), **mode),
            pl.BlockSpec((bm, width), lambda i, j: (jnp.maximum(i - n_prompt_blocks, 0), col_fn(j, cb_s)), **mode)]


def _by_stream(i, n_prompt_blocks, operands, body):
    if all(len(o) == 1 for o in operands):
        body([o[0] for o in operands])
        return

    @pl.when(i < n_prompt_blocks)
    def _():
        body([o[0] for o in operands])

    @pl.when(i >= n_prompt_blocks)
    def _():
        body([o[-1] for o in operands])


def _mod_kernel(c_ref, w_ref, b_ref, o_ref):
    c = c_ref[...]
    s = c * _sigmoid(c)
    o_ref[...] = _dot(s.astype(BF16), w_ref[...].astype(BF16)) + b_ref[...]


def _modulation(cond, w, b):
    d, n = w.shape
    bn = 512
    out = pl.pallas_call(
        _mod_kernel,
        grid=(n // bn,),
        in_specs=[pl.BlockSpec((MAX_CONDS, d), lambda j: (0, 0)),
                  pl.BlockSpec((d, bn), lambda j: (0, j)),
                  pl.BlockSpec((1, bn), lambda j: (0, j))],
        out_specs=pl.BlockSpec((MAX_CONDS, bn), lambda j: (0, j)),
        out_shape=jax.ShapeDtypeStruct((MAX_CONDS, n), F32),
        compiler_params=_params(("arbitrary",), 40),
        name="modulation",
    )(cond, w, b.reshape(1, n))
    return out.reshape(MAX_CONDS, 1, n)


def _norm_mod_kernel(*refs, n_x, n_prompt_blocks):
    x_refs, (g_ref, sh_ref, sc_ref, o_ref) = refs[:n_x], refs[n_x:]

    def body(r):
        x = r[0][...]
        ms = jnp.mean(x * x, axis=-1, keepdims=True)
        y = x * lax.rsqrt(ms + EPS) * g_ref[...]
        o_ref[...] = (y * (1.0 + sc_ref[...]) + sh_ref[...]).astype(o_ref.dtype)

    _by_stream(pl.program_id(0), n_prompt_blocks, [x_refs], body)


def _norm_mod(x_parts, gain, mod, shift_slot, scale_slot, n_prompt_rows, lat_len):
    d = D_MODEL
    bm = 256
    t = sum(x.shape[0] for x in x_parts)
    npb = n_prompt_rows // bm
    cidx = functools.partial(_cond_index, bm=bm, n_prompt_rows=n_prompt_rows, lat_len=lat_len)
    specs = _row_specs([(x, 0) for x in x_parts], bm, d, lambda j, cb: 0, npb)
    to1d = lambda spec: pl.BlockSpec(spec.block_shape, lambda i, f=spec.index_map: f(i, 0))
    return pl.pallas_call(
        functools.partial(_norm_mod_kernel, n_x=len(x_parts), n_prompt_blocks=npb),
        grid=(t // bm,),
        in_specs=[to1d(s) for s in specs] + [
            pl.BlockSpec((1, d), lambda i: (0, 0)),
            pl.BlockSpec((None, 1, d), lambda i: (cidx(i), 0, shift_slot)),
            pl.BlockSpec((None, 1, d), lambda i: (cidx(i), 0, scale_slot))],
        out_specs=pl.BlockSpec((bm, d), lambda i: (i, 0)),
        out_shape=jax.ShapeDtypeStruct((t, d), BF16),
        compiler_params=_params(("arbitrary",), 40),
        name="norm_mod",
    )(*x_parts, gain.reshape(1, d), mod, mod)


def _rmsnorm_kernel(x_ref, g_ref, o_ref):
    x = x_ref[...].astype(F32)
    ms = jnp.mean(x * x, axis=-1, keepdims=True)
    o_ref[...] = (x * lax.rsqrt(ms + EPS) * g_ref[...]).astype(o_ref.dtype)


def _rmsnorm_cols(x, gain, col_block, width, out_dtype, row0=0, rows=None):
    bm = 256
    rows = x.shape[0] if rows is None else rows
    rb0 = row0 // bm
    return pl.pallas_call(
        _rmsnorm_kernel,
        grid=(rows // bm,),
        in_specs=[pl.BlockSpec((bm, width), lambda i: (i + rb0, col_block)),
                  pl.BlockSpec((1, width), lambda i: (0, 0))],
        out_specs=pl.BlockSpec((bm, width), lambda i: (i, 0)),
        out_shape=jax.ShapeDtypeStruct((rows, width), out_dtype),
        compiler_params=_params(("arbitrary",), 40),
        name="rmsnorm",
    )(x, gain.reshape(1, width))


def _mm_kernel(x_ref, w_ref, o_ref):
    o_ref[...] = _dot(x_ref[...].astype(BF16), w_ref[...].astype(BF16)).astype(o_ref.dtype)


def _matmul(x, w, out_dtype, bn, name, col0=0, ncols=None):
    m, k = x.shape
    ncols = w.shape[1] - col0 if ncols is None else ncols
    bm = min(ROW_TILE, m)
    assert m % bm == 0 and ncols % bn == 0 and col0 % bn == 0
    cb0 = col0 // bn
    return pl.pallas_call(
        _mm_kernel,
        grid=(m // bm, ncols // bn),
        in_specs=[pl.BlockSpec((bm, k), lambda i, j: (i, 0)),
                  pl.BlockSpec((k, bn), lambda i, j: (0, cb0 + j))],
        out_specs=pl.BlockSpec((bm, bn), lambda i, j: (i, j)),
        out_shape=jax.ShapeDtypeStruct((m, ncols), out_dtype),
        compiler_params=_params(("arbitrary", "arbitrary"), 56),
        name=name,
    )(x, w)


def _mm_heads_kernel(x_ref, w_ref, o_ref, *, seqs, seq_len, heads):
    acc = _dot(x_ref[...], w_ref[...].astype(BF16))
    for b in range(seqs):
        for hh in range(heads):
            o_ref[b, hh] = acc[b * seq_len:(b + 1) * seq_len, hh * HEAD_DIM:(hh + 1) * HEAD_DIM]


def _matmul_heads(x, w, col0, heads, row0, rows, seq_len, name):
    k = x.shape[1]
    bm = ROW_TILE
    hb = 4
    bn = hb * HEAD_DIM
    assert rows % bm == 0 and row0 % bm == 0 and bm % seq_len == 0 and heads % hb == 0 and col0 % bn == 0
    seqs = bm // seq_len
    rb0, cb0 = row0 // bm, col0 // bn
    return pl.pallas_call(
        functools.partial(_mm_heads_kernel, seqs=seqs, seq_len=seq_len, heads=hb),
        grid=(rows // bm, heads // hb),
        in_specs=[pl.BlockSpec((bm, k), lambda i, j: (rb0 + i, 0)),
                  pl.BlockSpec((k, bn), lambda i, j: (0, cb0 + j))],
        out_specs=pl.BlockSpec((seqs, hb, seq_len, HEAD_DIM), lambda i, j: (i, j, 0, 0)),
        out_shape=jax.ShapeDtypeStruct((rows // seq_len, heads, seq_len, HEAD_DIM), F32),
        compiler_params=_params(("arbitrary", "arbitrary"), 56),
        name=name,
    )(x, w)


def _mm_res_kernel(*refs, n_lhs, n_parts, n_res, n_prompt_blocks):
    pos = 0
    lhs = []
    for n in n_lhs:
        lhs.append(refs[pos:pos + n])
        pos += n
    w_refs = refs[pos:pos + n_parts]
    pos += n_parts
    res = refs[pos:pos + n_res]
    pos += n_res
    gate_ref, o_ref = refs[pos], refs[pos + 1]

    def body(r):
        acc = None
        for x_ref, w_ref in zip(r[:n_parts], w_refs):
            part = _dot(x_ref[...], w_ref[...].astype(BF16))
            acc = part if acc is None else acc + part
        o_ref[...] = r[n_parts][...] + gate_ref[...] * acc

    _by_stream(pl.program_id(0), n_prompt_blocks, lhs + [res], body)


def _matmul_residual(lhs_parts, w, res_parts, mod, gate_slot, bn, n_prompt_rows, lat_len, name,
                     single_buffer_x=False, vmem_mb=56):
    bm = ROW_TILE
    n = w.shape[1]
    kq = w.shape[0] // len(lhs_parts)
    t = sum(r.shape[0] for r in res_parts)
    npb = n_prompt_rows // bm
    cidx = functools.partial(_cond_index, bm=bm, n_prompt_rows=n_prompt_rows, lat_len=lat_len)
    gate_col0 = gate_slot * (D_MODEL // bn)
    in_specs, args = [], []
    for parts in lhs_parts:
        in_specs += _row_specs(parts, bm, kq, lambda j, cb: cb, npb, single_buffer=single_buffer_x)
        args += [a for a, _ in parts]
    for q in range(len(lhs_parts)):
        in_specs.append(pl.BlockSpec((kq, bn), lambda i, j, q=q: (q, j)))
        args.append(w)
    in_specs += _row_specs([(r, 0) for r in res_parts], bm, bn, lambda j, cb: j, npb)
    args += list(res_parts)
    in_specs.append(pl.BlockSpec((None, 1, bn), lambda i, j: (cidx(i), 0, gate_col0 + j)))
    args.append(mod)
    body = functools.partial(_mm_res_kernel, n_lhs=tuple(len(p) for p in lhs_parts), n_parts=len(lhs_parts),
                             n_res=len(res_parts), n_prompt_blocks=npb)
    return pl.pallas_call(
        body,
        grid=(t // bm, n // bn),
        in_specs=in_specs,
        out_specs=pl.BlockSpec((bm, bn), lambda i, j: (i, j)),
        out_shape=jax.ShapeDtypeStruct((t, n), F32),
        compiler_params=_params(("arbitrary", "arbitrary"), vmem_mb),
        name=name,
    )(*args)


def _glu_kernel(*refs, n_y, n_prompt_blocks):
    y_refs, yt_refs = refs[:n_y], refs[n_y:2 * n_y]
    w_ref, b_ref, o_ref = refs[2 * n_y:]

    def body(r):
        z = _dot(r[0][...].astype(BF16), w_ref[...].astype(BF16)) + b_ref[...]
        o_ref[...] = (r[1][...] * _sigmoid(z)).astype(o_ref.dtype)

    _by_stream(pl.program_id(0), n_prompt_blocks, [y_refs, yt_refs], body)


def _glu(y_parts, w, b, bn, n_prompt_rows):
    k, n = w.shape
    bm = ROW_TILE
    m = sum(y.shape[0] for y in y_parts)
    npb = n_prompt_rows // bm
    parts = [(y, 0) for y in y_parts]
    return pl.pallas_call(
        functools.partial(_glu_kernel, n_y=len(y_parts), n_prompt_blocks=npb),
        grid=(m // bm, n // bn),
        in_specs=(_row_specs(parts, bm, k, lambda j, cb: 0, npb, single_buffer=True)
                  + _row_specs(parts, bm, bn, lambda j, cb: j, npb)
                  + [pl.BlockSpec((k, bn), lambda i, j: (0, j)),
                     pl.BlockSpec((1, bn), lambda i, j: (0, j))]),
        out_specs=pl.BlockSpec((bm, bn), lambda i, j: (i, j)),
        out_shape=jax.ShapeDtypeStruct((m, n), BF16),
        compiler_params=_params(("arbitrary", "arbitrary"), 56),
        name="s5_glu",
    )(*y_parts, *y_parts, w, b.reshape(1, n))


def _ffn_up_kernel(x_ref, wg_ref, wv_ref, cwg_ref, cwv_ref, cbg_ref, cbv_ref, o_ref, *,
                   sub, n_sub, prompt_blocks, prompt_len, lat_len):
    i = pl.program_id(0)
    period = jnp.where(i < prompt_blocks, prompt_len, lat_len)
    pos = lax.broadcasted_iota(jnp.int32, (sub, 1), 0) & (period - 1)
    first = pos == 0
    last = pos == period - 1
    wg = wg_ref[...].astype(BF16)
    wv = wv_ref[...].astype(BF16)

    def conv(u, cw_ref, cb_ref):
        prev = jnp.where(first, 0.0, pltpu.roll(u, 1, 0))
        nxt = jnp.where(last, 0.0, pltpu.roll(u, sub - 1, 0))
        return prev * cw_ref[0:1, :] + u * cw_ref[1:2, :] + nxt * cw_ref[2:3, :] + cb_ref[...]

    for s in range(n_sub):
        x = x_ref[s * sub:(s + 1) * sub, :]
        g = conv(_dot(x, wg), cwg_ref, cbg_ref)
        v = conv(_dot(x, wv), cwv_ref, cbv_ref)
        o_ref[s * sub:(s + 1) * sub, :] = (g * _sigmoid(g) * v).astype(o_ref.dtype)


def _ffn_up(h, w_up, conv_w, conv_b, n_prompt_rows, prompt_len, lat_len):
    t, d = h.shape
    sub, n_sub, bn = lat_len, 2, 256
    bm = sub * n_sub
    assert t % bm == 0 and n_prompt_rows % bm == 0 and sub % prompt_len == 0
    assert prompt_len & (prompt_len - 1) == 0 and lat_len & (lat_len - 1) == 0
    nj = D_FF // bn
    conv_b = conv_b.reshape(1, 2 * D_FF)
    body = functools.partial(_ffn_up_kernel, sub=sub, n_sub=n_sub, prompt_blocks=n_prompt_rows // bm,
                             prompt_len=prompt_len, lat_len=lat_len)
    return pl.pallas_call(
        body,
        grid=(t // bm, nj),
        in_specs=[pl.BlockSpec((bm, d), lambda i, j: (i, 0), pipeline_mode=pl.Buffered(1)),
                  pl.BlockSpec((d, bn), lambda i, j: (0, j)),
                  pl.BlockSpec((d, bn), lambda i, j: (0, nj + j)),
                  pl.BlockSpec((3, bn), lambda i, j: (0, j)),
                  pl.BlockSpec((3, bn), lambda i, j: (0, nj + j)),
                  pl.BlockSpec((1, bn), lambda i, j: (0, j)),
                  pl.BlockSpec((1, bn), lambda i, j: (0, nj + j))],
        out_specs=pl.BlockSpec((bm, bn), lambda i, j: (i, j)),
        out_shape=jax.ShapeDtypeStruct((t, D_FF), BF16),
        compiler_params=_params(("arbitrary", "arbitrary"), 56),
        name="ffn_up_conv",
    )(h, w_up, w_up, conv_w, conv_w, conv_b, conv_b)


def _s5_operators(lam_re, lam_im, log_dt, b_re, b_im, c_re, c_im, d_skip):
    q, g, p, c = S5_CHUNK, S5_GROUPS, S5_STATE, S5_GROUP
    dt = jnp.exp(log_dt)[None, :, :, None]
    tau = jnp.arange(q + 1, dtype=F32)[:, None, None, None]
    mag = jnp.exp(lam_re[None] * dt * tau)
    ang = lam_im[None] * dt * tau
    pw_re, pw_im = mag * jnp.cos(ang), mag * jnp.sin(ang)
    dt1 = jnp.exp(log_dt)[:, :, None]
    m1 = jnp.exp(lam_re * dt1)
    ab_re, ab_im = m1 * jnp.cos(lam_im * dt1), m1 * jnp.sin(lam_im * dt1)
    den = lam_re * lam_re + lam_im * lam_im
    nr, ni = ab_re - 1.0, ab_im
    f_re = ((nr * lam_re + ni * lam_im) / den)[..., None]
    f_im = ((ni * lam_re - nr * lam_im) / den)[..., None]
    bb_re = f_re * b_re - f_im * b_im
    bb_im = f_re * b_im + f_im * b_re
    ct_re, ct_im = jnp.swapaxes(c_re, -1, -2), jnp.swapaxes(c_im, -1, -2)

    def power_times_c(fwd_taus, bwd_taus):
        pr = jnp.stack([pw_re[fwd_taus, 0], pw_re[bwd_taus, 1]], axis=0).transpose(0, 2, 3, 1)
        pi = jnp.stack([pw_im[fwd_taus, 0], pw_im[bwd_taus, 1]], axis=0).transpose(0, 2, 3, 1)
        r_re = pr[..., None] * ct_re[:, :, :, None, :] - pi[..., None] * ct_im[:, :, :, None, :]
        r_im = pr[..., None] * ct_im[:, :, :, None, :] + pi[..., None] * ct_re[:, :, :, None, :]
        return r_re.reshape(2, g, p, q * c), r_im.reshape(2, g, p, q * c)

    lags = jnp.arange(q)
    rk_re, rk_im = power_times_c(lags, lags[::-1])
    diag = jnp.tile(d_skip[:, :, None] * jnp.eye(c, dtype=F32)[None], (1, 1, q))
    t_op = _s5_toeplitz(rk_re, rk_im, bb_re, bb_im, diag)
    re_re, re_im = power_times_c(lags + 1, q - lags)
    zero = jnp.zeros_like(re_re[0])
    e_op = jnp.concatenate([re_re[0], zero, -re_im[0], zero, zero, re_re[1], zero, -re_im[1]], axis=1)
    def s_op(taus, d):
        pr, pi = pw_re[taus, d][:, :, None, :], pw_im[taus, d][:, :, None, :]
        br, bi = jnp.swapaxes(bb_re[d], -1, -2)[None], jnp.swapaxes(bb_im[d], -1, -2)[None]
        s_re = (pr * br - pi * bi).transpose(1, 0, 2, 3).reshape(g, q * c, p)
        s_im = (pr * bi + pi * br).transpose(1, 0, 2, 3).reshape(g, q * c, p)
        return s_re, s_im

    sf_re, sf_im = s_op(lags[::-1], 0)
    sb_re, sb_im = s_op(lags, 1)
    sb = jnp.concatenate([sf_re, sb_re, sf_im, sb_im], axis=2)
    a16 = jnp.concatenate([pw_re[q, 0], pw_re[q, 1], pw_im[q, 0], pw_im[q, 1]], axis=-1)[:, None, :]
    return sb.astype(BF16), t_op, e_op.astype(BF16), a16


def _s5_toeplitz_kernel(rr_ref, ri_ref, br_ref, bi_ref, dg_ref, o_ref, k_scr):
    c, q = S5_GROUP, S5_CHUNK
    for d in range(2):
        for ci in range(c):
            prod = br_ref[d, :, ci:ci + 1] * rr_ref[d] - bi_ref[d, :, ci:ci + 1] * ri_ref[d]
            k_scr[d * c + ci:d * c + ci + 1, :] = jnp.sum(prod, axis=0, keepdims=True)
    kf = k_scr[0:c, :]
    kb = k_scr[c:2 * c, :]
    lane = lax.broadcasted_iota(jnp.int32, (c, q * c), 1)
    dg = dg_ref[...]
    for s in range(q):
        f = kf if s == 0 else pltpu.roll(kf, c * s, 1)
        b = kb if s == q - 1 else pltpu.roll(kb, q * c - c * (q - 1 - s), 1)
        piece = jnp.where(lane >= c * s, f, 0.0) + jnp.where(lane < c * (s + 1), b, 0.0)
        piece = piece + jnp.where(jnp.logical_and(lane >= c * s, lane < c * (s + 1)), dg, 0.0)
        o_ref[s * c:(s + 1) * c, :] = piece.astype(o_ref.dtype)


def _s5_toeplitz(rk_re, rk_im, bb_re, bb_im, diag):
    g, p, c, qc = S5_GROUPS, S5_STATE, S5_GROUP, S5_CHUNK * S5_GROUP
    r_spec = pl.BlockSpec((2, None, p, qc), lambda i: (0, i, 0, 0))
    b_spec = pl.BlockSpec((2, None, p, c), lambda i: (0, i, 0, 0))
    return pl.pallas_call(
        _s5_toeplitz_kernel,
        grid=(g,),
        in_specs=[r_spec, r_spec, b_spec, b_spec, pl.BlockSpec((None, c, qc), lambda i: (i, 0, 0))],
        out_specs=pl.BlockSpec((None, qc, qc), lambda i: (i, 0, 0)),
        out_shape=jax.ShapeDtypeStruct((g, qc, qc), BF16),
        scratch_shapes=[pltpu.VMEM((2 * c, qc), F32)],
        compiler_params=_params(("arbitrary",), 32),
        name="s5_toeplitz",
    )(rk_re, rk_im, bb_re, bb_im, diag)


def _gelu_tanh(x):
    return 0.5 * x * (1.0 + jnp.tanh(math.sqrt(2.0 / math.pi) * (x + 0.044715 * (x * x * x))))


GROUPS_PER_STEP = LANE // S5_GROUP
STATE_W = GROUPS_PER_STEP * 2 * S5_STATE


def _s5_lane_permutation():
    j = jnp.arange(S5_CHUNK * LANE)
    s, gl, c = j // LANE, (j % LANE) // S5_GROUP, j % S5_GROUP
    k = gl * (S5_CHUNK * S5_GROUP) + s * S5_GROUP + c
    return (k[:, None] == jnp.arange(S5_CHUNK * LANE)[None, :]).astype(BF16)


def _s5_kernel(u_ref, sel_ref, sb_ref, t_ref, e_ref, a_ref, h0_ref, y_ref, hfin_ref, xg, w_scr, h_scr, *,
               batch, nb, nc, seq_len):
    qc = S5_CHUNK * S5_GROUP
    if batch < nb:
        xg[...] = jnp.zeros(xg.shape, BF16)
    for ch in range(nc):
        for s in range(S5_CHUNK):
            xg[ch * nb:ch * nb + batch, s * LANE:(s + 1) * LANE] = (
                u_ref[pl.ds(ch * S5_CHUNK + s, batch, stride=seq_len), :].astype(BF16))
    sel = sel_ref[...]
    ucat = _dot(xg[...], sel).astype(BF16)
    for gl in range(GROUPS_PER_STEP):
        w = _dot(ucat[:, gl * qc:(gl + 1) * qc], sb_ref[gl])
        w_scr[:, gl * LANE:(gl + 1) * LANE] = w[:, 0:LANE]
        w_scr[:, STATE_W + gl * LANE:STATE_W + (gl + 1) * LANE] = w[:, LANE:2 * LANE]
    ar = a_ref[0:1, 0:STATE_W]
    ai = a_ref[0:1, STATE_W:2 * STATE_W]
    fwd_lane = (lax.broadcasted_iota(jnp.int32, (1, STATE_W), 1) & S5_STATE) == 0
    hr = h0_ref[:, 0:STATE_W]
    hi = h0_ref[:, STATE_W:2 * STATE_W]
    for i in range(nc):
        ri = i * nb
        rj = (nc - 1 - i) * nb
        h_scr[ri:ri + nb, 0:STATE_W] = hr
        h_scr[ri:ri + nb, STATE_W:2 * STATE_W] = hi
        h_scr[rj:rj + nb, 2 * STATE_W:3 * STATE_W] = hr
        h_scr[rj:rj + nb, 3 * STATE_W:4 * STATE_W] = hi
        wr = jnp.where(fwd_lane, w_scr[ri:ri + nb, 0:STATE_W], w_scr[rj:rj + nb, 0:STATE_W])
        wi = jnp.where(fwd_lane, w_scr[ri:ri + nb, STATE_W:2 * STATE_W], w_scr[rj:rj + nb, STATE_W:2 * STATE_W])
        hr, hi = ar * hr - ai * hi + wr, ar * hi + ai * hr + wi
    hfin_ref[:, 0:STATE_W] = hr
    hfin_ref[:, STATE_W:2 * STATE_W] = hi
    for gl in range(GROUPS_PER_STEP):
        hcat = jnp.concatenate([h_scr[:, k * STATE_W + gl * LANE:k * STATE_W + (gl + 1) * LANE] for k in range(4)],
                               axis=1).astype(BF16)
        y = _dot(ucat[:, gl * qc:(gl + 1) * qc], t_ref[gl]) + _dot(hcat, e_ref[gl])
        xg[:, gl * qc:(gl + 1) * qc] = _gelu_tanh(y).astype(BF16)
    yp = _dot_t(xg[...], sel)
    for ch in range(nc):
        for s in range(S5_CHUNK):
            y_ref[pl.ds(ch * S5_CHUNK + s, batch, stride=seq_len), :] = (
                yp[ch * nb:ch * nb + batch, s * LANE:(s + 1) * LANE])


def _s5_stream(u, row_block, batch, seq_len, sel, sb, t_op, e_op, a_planes, h0):
    n_rows = batch * seq_len
    nb = -(-batch // SUBLANE) * SUBLANE
    nc = seq_len // S5_CHUNK
    rows = nb * nc
    nblk = S5_WIDTH // LANE
    qc = S5_CHUNK * S5_GROUP
    body = functools.partial(_s5_kernel, batch=batch, nb=nb, nc=nc, seq_len=seq_len)
    return pl.pallas_call(
        body,
        grid=(nblk,),
        in_specs=[pl.BlockSpec((n_rows, LANE), lambda g: (row_block, g), pipeline_mode=pl.Buffered(1)),
                  pl.BlockSpec((S5_CHUNK * LANE, S5_CHUNK * LANE), lambda g: (0, 0), pipeline_mode=pl.Buffered(1)),
                  pl.BlockSpec((GROUPS_PER_STEP, qc, 4 * S5_STATE), lambda g: (g, 0, 0)),
                  pl.BlockSpec((GROUPS_PER_STEP, qc, qc), lambda g: (g, 0, 0)),
                  pl.BlockSpec((GROUPS_PER_STEP, 8 * S5_STATE, qc), lambda g: (g, 0, 0)),
                  pl.BlockSpec((None, 1, 2 * STATE_W), lambda g: (g, 0, 0)),
                  pl.BlockSpec((None, nb, 2 * STATE_W), lambda g: (g, 0, 0))],
        out_specs=[pl.BlockSpec((n_rows, LANE), lambda g: (0, g)),
                   pl.BlockSpec((None, nb, 2 * STATE_W), lambda g: (g, 0, 0))],
        out_shape=[jax.ShapeDtypeStruct((n_rows, S5_WIDTH), F32),
                   jax.ShapeDtypeStruct((nblk, nb, 2 * STATE_W), F32)],
        scratch_shapes=[pltpu.VMEM((rows, S5_CHUNK * LANE), BF16),
                        pltpu.VMEM((rows, 2 * STATE_W), F32),
                        pltpu.VMEM((rows, 4 * STATE_W), F32)],
        compiler_params=_params(("arbitrary",), 56),
        name="s5_chunked",
    )(u, sel, sb, t_op, e_op, a_planes, h0)


def _rope_tables(length, rot):
    n_freq = rot // 4
    t = jnp.arange(length)
    row = (t // GRID_W).astype(F32)
    col = (t % GRID_W).astype(F32)
    inv = ROPE_BASE ** (-jnp.arange(n_freq, dtype=F32) / n_freq)
    ar, ac = row[:, None] * inv, col[:, None] * inv
    cos = jnp.concatenate([jnp.cos(ar), jnp.cos(ar), jnp.cos(ac), jnp.cos(ac)], axis=-1)
    sin = jnp.concatenate([-jnp.sin(ar), jnp.sin(ar), -jnp.sin(ac), jnp.sin(ac)], axis=-1)
    pad = LANE - rot
    if pad:
        cos = jnp.concatenate([cos, jnp.ones((length, pad), F32)], axis=-1)
        sin = jnp.concatenate([sin, jnp.zeros((length, pad), F32)], axis=-1)
    return cos, sin


def _rope(x, cos, sin, blk):
    lane = lax.broadcasted_iota(jnp.int32, (1, LANE), 1)
    lower = (lane & blk) == 0
    partner = jnp.where(lower, pltpu.roll(x, LANE - blk, 1), pltpu.roll(x, blk, 1))
    return x * cos + partner * sin


def _softmax_pv(scores, values, extra_logit=None):
    m = scores[0].max(axis=-1, keepdims=True)
    for s in scores[1:]:
        m = jnp.maximum(m, s.max(axis=-1, keepdims=True))
    if extra_logit is not None:
        m = jnp.maximum(m, extra_logit)
    den = None
    out = None
    for s, v in zip(scores, values):
        p = jnp.exp(s - m)
        d = p.sum(axis=-1, keepdims=True)
        o = _dot(p.astype(BF16), v)
        den = d if den is None else den + d
        out = o if out is None else out + o
    if extra_logit is not None:
        den = den + jnp.exp(extra_logit - m)
    return out / den


def _ctx_gqa_kernel(sink_ref, q_ref, k_ref, v_ref, o_ref):
    g = pl.program_id(1)
    scale = HEAD_DIM ** -0.5
    k = k_ref[...].astype(BF16)
    v = v_ref[...].astype(BF16)
    for r in range(WIN_GROUP):
        q = q_ref[:, r * HEAD_DIM:(r + 1) * HEAD_DIM]
        s = _dot_t(q, k) * scale
        o = _softmax_pv([s], [v], sink_ref[g * WIN_GROUP + r])
        o_ref[:, r * HEAD_DIM:(r + 1) * HEAD_DIM] = o.astype(o_ref.dtype)


def _ctx_gqa_attention(uq3, k4, v4, sink, n_prompt, length):
    qw = WIN_GROUP * HEAD_DIM
    q0 = 0
    kv_spec = pl.BlockSpec((None, None, length, HEAD_DIM), lambda b, g: (b, g, 0, 0))
    return pl.pallas_call(
        _ctx_gqa_kernel,
        grid=(n_prompt, WIN_KV_HEADS),
        in_specs=[pl.BlockSpec(memory_space=pltpu.SMEM),
                  pl.BlockSpec((None, length, qw), lambda b, g: (b, 0, q0 + g)),
                  kv_spec, kv_spec],
        out_specs=pl.BlockSpec((None, length, qw), lambda b, g: (b, 0, g)),
        out_shape=jax.ShapeDtypeStruct((n_prompt, length, WIN_Q_WIDTH), BF16),
        compiler_params=_params(("arbitrary", "arbitrary"), 40),
        name="ctx_gqa_attention",
    )(sink, uq3, k4, v4)


def _lat_window_kernel(sink_ref, q_ref, k_ref, v_ref, kc_ref, vc_ref, cos_ref, sin_ref, o_ref, *, length):
    g = pl.program_id(1)
    scale = HEAD_DIM ** -0.5
    blk = WIN_RADIUS
    nb = length // blk
    k = _rope(k_ref[...], cos_ref[...], sin_ref[...], HEAD_DIM // 4).astype(BF16)
    v = v_ref[...].astype(BF16)
    kc = kc_ref[...].astype(BF16)
    vc = vc_ref[...].astype(BF16)
    rows = WIN_GROUP * blk
    row = lax.broadcasted_iota(jnp.int32, (rows, 1), 0)
    sk = jnp.zeros((rows, 1), F32)
    for r in range(WIN_GROUP):
        sk = jnp.where(jnp.logical_and(row >= r * blk, row < (r + 1) * blk), sink_ref[g * WIN_GROUP + r], sk)
    qoff = row & (blk - 1)
    for n in range(nb):
        lo = max(0, n - 1) * blk
        hi = min(nb, n + 2) * blk
        cos = cos_ref[n * blk:(n + 1) * blk, :]
        sin = sin_ref[n * blk:(n + 1) * blk, :]
        q = jnp.concatenate(
            [_rope(q_ref[n * blk:(n + 1) * blk, r * HEAD_DIM:(r + 1) * HEAD_DIM].astype(F32), cos, sin, HEAD_DIM // 4)
             for r in range(WIN_GROUP)], axis=0).astype(BF16)
        dist = (n * blk + qoff) - (lo + lax.broadcasted_iota(jnp.int32, (1, hi - lo), 1))
        visible = jnp.logical_and(dist <= WIN_RADIUS, dist >= -WIN_RADIUS)
        s_loc = jnp.where(visible, _dot_t(q, k[lo:hi]) * scale, NEG)
        s_ctx = _dot_t(q, kc) * scale
        o = _softmax_pv([s_loc, s_ctx], [v[lo:hi], vc], sk)
        for r in range(WIN_GROUP):
            o_ref[n * blk:(n + 1) * blk, r * HEAD_DIM:(r + 1) * HEAD_DIM] = o[r * blk:(r + 1) * blk].astype(o_ref.dtype)


def _lat_window_attention(uq3, seq0, k4, v4, k_ctx, v_ctx, sink, cos, sin):
    n_lat, _, length, _ = k4.shape
    qw = WIN_GROUP * HEAD_DIM
    q0 = 0
    lc = k_ctx.shape[2]
    kv_spec = pl.BlockSpec((None, None, length, HEAD_DIM), lambda b, g: (b, g, 0, 0))
    ctx_spec = pl.BlockSpec((None, None, lc, HEAD_DIM), lambda b, g: (b, g, 0, 0))
    tab_spec = pl.BlockSpec((length, LANE), lambda b, g: (0, 0))
    return pl.pallas_call(
        functools.partial(_lat_window_kernel, length=length),
        grid=(n_lat, WIN_KV_HEADS),
        in_specs=[pl.BlockSpec(memory_space=pltpu.SMEM),
                  pl.BlockSpec((None, length, qw), lambda b, g: (seq0 + b, 0, q0 + g)),
                  kv_spec, kv_spec, ctx_spec, ctx_spec, tab_spec, tab_spec],
        out_specs=pl.BlockSpec((None, length, qw), lambda b, g: (b, 0, g)),
        out_shape=jax.ShapeDtypeStruct((n_lat, length, WIN_Q_WIDTH), BF16),
        compiler_params=_params(("arbitrary", "arbitrary"), 48),
        name="latent_window_attention",
    )(sink, uq3, k4, v4, k_ctx, v_ctx, cos, sin)


def _ctx_odd_kernel(q_ref, kv_ref, kpe_ref, qd_ref, kd_ref, vd_ref, o_ref):
    kw = MLA_NOPE + LANE
    kpe = kpe_ref[...].astype(BF16)
    scale_c = (MLA_NOPE + MLA_ROPE) ** -0.5
    for h in range(MLA_HEADS):
        q = q_ref[:, h * kw:(h + 1) * kw]
        k = jnp.concatenate([kv_ref[:, h * kw:h * kw + MLA_NOPE], kpe], axis=1)
        v = kv_ref[:, h * kw + MLA_NOPE:(h + 1) * kw]
        o = _softmax_pv([_dot_t(q, k) * scale_c], [v])
        o_ref[:, h * MLA_V:(h + 1) * MLA_V] = o.astype(o_ref.dtype)
    scale_d = HEAD_DIM ** -0.5
    base = MLA_HEADS * MLA_V
    for h in range(NA_HEADS):
        q = qd_ref[:, h * HEAD_DIM:(h + 1) * HEAD_DIM]
        o = _softmax_pv([_dot_t(q, kd_ref[h].astype(BF16)) * scale_d], [vd_ref[h].astype(BF16)])
        o_ref[:, base + h * HEAD_DIM:base + (h + 1) * HEAD_DIM] = o.astype(o_ref.dtype)


def _ctx_odd_attention(q3, kv3, kpe3, qd3, kd4, vd4, n_prompt, length):
    qn = q3.shape[-1]
    head_spec = pl.BlockSpec((None, NA_HEADS, length, HEAD_DIM), lambda b: (b, 0, 0, 0))
    return pl.pallas_call(
        _ctx_odd_kernel,
        grid=(n_prompt,),
        in_specs=[pl.BlockSpec((None, length, qn), lambda b: (b, 0, 0)),
                  pl.BlockSpec((None, length, qn), lambda b: (b, 0, 0)),
                  pl.BlockSpec((None, length, LANE), lambda b: (b, 0, 0)),
                  pl.BlockSpec((None, length, NA_WIDTH), lambda b: (b, 0, 0)),
                  head_spec, head_spec],
        out_specs=pl.BlockSpec((None, length, MLA_HEADS * MLA_V + NA_WIDTH), lambda b: (b, 0, 0)),
        out_shape=jax.ShapeDtypeStruct((n_prompt, length, MLA_HEADS * MLA_V + NA_WIDTH), BF16),
        compiler_params=_params(("arbitrary",), 48),
        name="ctx_odd_attention",
    )(q3, kv3, kpe3, qd3, kd4, vd4)


def _lat_mla_kernel(q_ref, kv_ref, kpe_ref, kvc_ref, kpec_ref, cos_ref, sin_ref, o_ref, *, length):
    scale = (MLA_NOPE + MLA_ROPE) ** -0.5
    rb = MLA_ROPE // 4
    kpe = _rope(kpe_ref[...], cos_ref[...], sin_ref[...], rb).astype(BF16)
    k_lat = jnp.concatenate([kv_ref[:, 0:MLA_NOPE], kpe], axis=1)
    v_lat = kv_ref[:, MLA_NOPE:]
    k_ctx = jnp.concatenate([kvc_ref[:, 0:MLA_NOPE], kpec_ref[...].astype(BF16)], axis=1)
    v_ctx = kvc_ref[:, MLA_NOPE:]
    qb = 256
    for n in range(length // qb):
        rows = slice(n * qb, (n + 1) * qb)
        q_pe = _rope(q_ref[rows, MLA_NOPE:].astype(F32), cos_ref[rows, :], sin_ref[rows, :], rb).astype(BF16)
        q = jnp.concatenate([q_ref[rows, 0:MLA_NOPE], q_pe], axis=1)
        o = _softmax_pv([_dot_t(q, k_lat) * scale, _dot_t(q, k_ctx) * scale], [v_lat, v_ctx])
        o_ref[rows, :] = o.astype(o_ref.dtype)


def _lat_mla_attention(q3, kv3, kvc3, kpe3, kpe_ctx, seq0, n_lat, length, cos, sin):
    kw = MLA_NOPE + LANE
    lc = kpe_ctx.shape[1]
    return pl.pallas_call(
        functools.partial(_lat_mla_kernel, length=length),
        grid=(n_lat, MLA_HEADS),
        in_specs=[pl.BlockSpec((None, length, kw), lambda b, h: (seq0 + b, 0, h)),
                  pl.BlockSpec((None, length, kw), lambda b, h: (seq0 + b, 0, h)),
                  pl.BlockSpec((None, length, LANE), lambda b, h: (seq0 + b, 0, 0)),
                  pl.BlockSpec((None, lc, kw), lambda b, h: (b, 0, h)),
                  pl.BlockSpec((None, lc, LANE), lambda b, h: (b, 0, 0)),
                  pl.BlockSpec((length, LANE), lambda b, h: (0, 0)),
                  pl.BlockSpec((length, LANE), lambda b, h: (0, 0))],
        out_specs=pl.BlockSpec((None, length, MLA_V), lambda b, h: (b, 0, h)),
        out_shape=jax.ShapeDtypeStruct((n_lat, length, MLA_HEADS * MLA_V), BF16),
        compiler_params=_params(("arbitrary", "arbitrary"), 48),
        name="latent_mla_attention",
    )(q3, kv3, kpe3, kvc3, kpe_ctx, cos, sin)


def _na_row_start(r, rows):
    kr = min(NA_ROWS, rows)
    return min(max(r - kr // 2, 0), rows - kr)


def _lat_na_kernel(q_ref, k_ref, v_ref, kc_ref, vc_ref, bias_ref, o_ref, *, length):
    scale = HEAD_DIM ** -0.5
    rows = length // GRID_W
    kr = min(NA_ROWS, rows)
    k = k_ref[...].astype(BF16)
    v = v_ref[...].astype(BF16)
    kc = kc_ref[...].astype(BF16)
    vc = vc_ref[...].astype(BF16)
    r = 0
    while r < rows:
        r_end = r + 1
        while r_end < rows and _na_row_start(r_end, rows) == _na_row_start(r, rows):
            r_end += 1
        r0 = _na_row_start(r, rows) * GRID_W
        nq = (r_end - r) * GRID_W
        q = q_ref[r * GRID_W:r_end * GRID_W, :]
        bias = bias_ref[r:r_end].reshape(nq, kr * GRID_W)
        s_nb = _dot_t(q, k[r0:r0 + kr * GRID_W]) * scale + bias
        s_ctx = _dot_t(q, kc) * scale
        o = _softmax_pv([s_nb, s_ctx], [v[r0:r0 + kr * GRID_W], vc])
        o_ref[r * GRID_W:r_end * GRID_W, :] = o.astype(o_ref.dtype)
        r = r_end


def _na_bias(rpb, length):
    rows = length // GRID_W
    kr = min(NA_ROWS, rows)
    col = jnp.arange(GRID_W)
    c_start = jnp.clip(col - NA_COLS // 2, 0, GRID_W - NA_COLS)
    col_valid = (col[None, :] >= c_start[:, None]) & (col[None, :] < c_start[:, None] + NA_COLS)
    off_c = jnp.clip(col[None, :] - col[:, None], -(NA_COLS - 1), NA_COLS - 1) + NA_COLS - 1
    onehot = (off_c[:, :, None] == jnp.arange(2 * NA_COLS - 1)[None, None, :]).astype(F32)
    table = jnp.einsum('hdj,qkj->hdqk', rpb.astype(F32), onehot, precision=lax.Precision.HIGHEST)
    table = jnp.where(col_valid[None, None], table, NEG)
    per_row = []
    for r in range(rows):
        r0 = _na_row_start(r, rows)
        per_row.append(jnp.concatenate([table[:, r0 + j - r + NA_ROWS - 1] for j in range(kr)], axis=-1))
    return jnp.stack(per_row, axis=1)


def _lat_na_attention(qd3, seq0, k4, v4, k_ctx, v_ctx, bias):
    n_lat, _, length, _ = k4.shape
    rows = length // GRID_W
    kr = min(NA_ROWS, rows)
    lc = k_ctx.shape[2]
    kv_spec = pl.BlockSpec((None, None, length, HEAD_DIM), lambda h, b: (b, h, 0, 0))
    ctx_spec = pl.BlockSpec((None, None, lc, HEAD_DIM), lambda h, b: (b, h, 0, 0))
    return pl.pallas_call(
        functools.partial(_lat_na_kernel, length=length),
        grid=(NA_HEADS, n_lat),
        in_specs=[pl.BlockSpec((None, length, HEAD_DIM), lambda h, b: (seq0 + b, 0, h)),
                  kv_spec, kv_spec, ctx_spec, ctx_spec,
                  pl.BlockSpec((None, rows, GRID_W, kr * GRID_W), lambda h, b: (h, 0, 0, 0))],
        out_specs=pl.BlockSpec((None, length, HEAD_DIM), lambda h, b: (b, 0, h)),
        out_shape=jax.ShapeDtypeStruct((n_lat, length, NA_WIDTH), BF16),
        compiler_params=_params(("arbitrary", "arbitrary"), 48),
        name="latent_neighborhood_attention",
    )(qd3, k4, v4, k_ctx, v_ctx, bias)


def kernel(x_prompt, x_sample, cache_l0_k, cache_l0_v, state_l0_re, state_l0_im, cache_l1_ckv, cache_l1_kpe, cache_l1_k, cache_l1_v, c, c_ctx, l0_ada_w, l0_ada_b, l0_norm1, l0_norm2, l0_w_in, l0_lambda_re, l0_lambda_im, l0_log_dt, l0_b_re, l0_b_im, l0_c_re, l0_c_im, l0_d_skip, l0_w_glu, l0_b_glu, l0_sink, l0_w_out, l0_ffn_w_up, l0_ffn_conv_w, l0_ffn_conv_b, l0_ffn_w_down, l1_ada_w, l1_ada_b, l1_norm1, l1_norm2, l1_w_in, l1_q_norm, l1_kv_norm, l1_w_uq, l1_w_ukv, l1_rpb, l1_w_out, l1_ffn_w_up, l1_ffn_conv_w, l1_ffn_conv_b, l1_ffn_w_down, final_norm):
    n_p, len_p, d = x_prompt.shape
    n_s, len_s, _ = x_sample.shape
    rows_p, rows_s = n_p * len_p, n_s * len_s
    t = rows_p + rows_s
    assert d == D_MODEL and len_s == ROW_TILE and ROW_TILE % len_p == 0 and n_s + 1 <= MAX_CONDS
    assert rows_p % (2 * ROW_TILE) == 0 and rows_s % (2 * ROW_TILE) == 0
    assert len_p % S5_CHUNK == 0 and len_s % (2 * WIN_RADIUS) == 0 and len_s % GRID_W == 0
    seqs_p_units = t // len_p
    seqs_s_units = t // len_s
    seq0_s = rows_p // len_s
    groups = dict(n_prompt_rows=rows_p, lat_len=len_s)
    half = D_MODEL // 2

    cond = jnp.zeros((MAX_CONDS, d), F32).at[0].set(c_ctx).at[1:1 + n_s].set(c)
    mod0 = _modulation(cond, l0_ada_w, l0_ada_b)
    mod1 = _modulation(cond, l1_ada_w, l1_ada_b)

    x_parts = [x_prompt.reshape(rows_p, d), x_sample.reshape(rows_s, d)]

    def ffn(x, mod, norm2, w_up, conv_w, conv_b, w_down):
        h = _norm_mod([x], norm2, mod, 3, 4, **groups)
        act = _ffn_up(h, w_up, conv_w, conv_b, rows_p, len_p, len_s)
        return _matmul_residual([[(act, 0)]], w_down.astype(BF16), [x], mod, 5, 256, name="ffn_down",
                                single_buffer_x=True, **groups)

    h = _norm_mod(x_parts, l0_norm1, mod0, 0, 1, **groups)
    u = _matmul(h, l0_w_in, F32, 512, "l0_in_proj_u", 0, S5_WIDTH)
    q0 = _matmul(h, l0_w_in, BF16, 512, "l0_in_proj_q", S5_WIDTH, WIN_Q_WIDTH)
    kcol = S5_WIDTH + WIN_Q_WIDTH
    vcol = kcol + WIN_KV_WIDTH
    out_k0 = _matmul_heads(h, l0_w_in, kcol, WIN_KV_HEADS, 0, rows_p, len_p, "l0_in_proj_k_ctx")
    out_v0 = _matmul_heads(h, l0_w_in, vcol, WIN_KV_HEADS, 0, rows_p, len_p, "l0_in_proj_v_ctx")
    k0_s = _matmul_heads(h, l0_w_in, kcol, WIN_KV_HEADS, rows_p, rows_s, len_s, "l0_in_proj_k_lat")
    v0_s = _matmul_heads(h, l0_w_in, vcol, WIN_KV_HEADS, rows_p, rows_s, len_s, "l0_in_proj_v_lat")

    assert rows_p % rows_s == 0
    sel = _s5_lane_permutation()
    sb, t_op, e_op, a16 = _s5_operators(l0_lambda_re, l0_lambda_im, l0_log_dt, l0_b_re, l0_b_im, l0_c_re, l0_c_im, l0_d_skip)
    nblk, gps, pw = S5_WIDTH // LANE, GROUPS_PER_STEP, 2 * S5_STATE

    def planes(cols):
        b = cols.shape[1]
        c5 = cols.reshape(nblk, gps, b, 2, pw).transpose(0, 2, 3, 1, 4)
        return c5.reshape(nblk, b, 2 * STATE_W)

    def state_cols(s):
        return s.astype(F32).transpose(2, 0, 1, 3).reshape(S5_GROUPS, s.shape[0], pw)

    np8, ns8 = -(-n_p // SUBLANE) * SUBLANE, -(-n_s // SUBLANE) * SUBLANE
    a_planes = planes(a16)
    h0_lat = planes(jnp.concatenate([state_cols(state_l0_re), state_cols(state_l0_im)], axis=-1))
    h0_s = jnp.pad(h0_lat, ((0, 0), (0, ns8 - n_s), (0, 0)))
    h0_p = jnp.zeros((nblk, np8, 2 * STATE_W), F32)
    y_p, hfin = _s5_stream(u, 0, n_p, len_p, sel, sb, t_op, e_op, a_planes, h0_p)
    y_s, _ = _s5_stream(u, rows_p // rows_s, n_s, len_s, sel, sb, t_op, e_op, a_planes, h0_s)
    a_out = _glu([y_p, y_s], l0_w_glu, l0_b_glu, 512, rows_p)

    def state_out(plane):
        return (plane[:, :n_p].reshape(nblk, n_p, gps, 2, S5_STATE).transpose(1, 3, 0, 2, 4)
                .reshape(n_p, 2, S5_GROUPS, S5_STATE))

    out_sre = state_out(hfin[:, :, :STATE_W])
    out_sim = state_out(hfin[:, :, STATE_W:])

    cos_e, sin_e = _rope_tables(len_s, HEAD_DIM)
    o_p = _ctx_gqa_attention(q0.reshape(seqs_p_units, len_p, -1), out_k0, out_v0, l0_sink, n_p, len_p)
    o_s = _lat_window_attention(q0.reshape(seqs_s_units, len_s, -1), seq0_s, k0_s, v0_s,
                                cache_l0_k, cache_l0_v, l0_sink, cos_e, sin_e)
    x = _matmul_residual([[(a_out, 0)], [(o_p.reshape(rows_p, half), 0), (o_s.reshape(rows_s, half), 0)]],
                         l0_w_out.astype(BF16), x_parts, mod0, 2, 512, name="l0_out_proj", single_buffer_x=True, **groups)
    x = ffn(x, mod0, l0_norm2, l0_ffn_w_up, l0_ffn_conv_w, l0_ffn_conv_b, l0_ffn_w_down)

    s1, s2 = MLA_Q_LORA + MLA_KV_LORA, MLA_Q_LORA + MLA_KV_LORA + MLA_ROPE
    w_lora = l1_w_in[:, :s1].astype(BF16)
    w_dkv = l1_w_in[:, s2:].astype(BF16)
    w_kpe = jnp.pad(l1_w_in[:, s1:s2], ((0, 0), (0, LANE - MLA_ROPE))).astype(BF16)
    w_uq = l1_w_uq.reshape(MLA_Q_LORA, MLA_HEADS, MLA_NOPE + MLA_ROPE)
    w_uq = jnp.pad(w_uq, ((0, 0), (0, 0), (0, LANE - MLA_ROPE))).reshape(MLA_Q_LORA, -1).astype(BF16)

    h = _norm_mod([x], l1_norm1, mod1, 0, 1, **groups)
    cqkv = _matmul(h, w_lora, F32, 512, "l1_in_proj_lora")
    kpe = _matmul(h, w_kpe, F32, LANE, "l1_in_proj_kpe")
    qd = _matmul(h, w_dkv, BF16, 512, "l1_in_proj_qd", 0, NA_WIDTH)
    out_k1 = _matmul_heads(h, w_dkv, NA_WIDTH, NA_HEADS, 0, rows_p, len_p, "l1_in_proj_kd_ctx")
    out_v1 = _matmul_heads(h, w_dkv, 2 * NA_WIDTH, NA_HEADS, 0, rows_p, len_p, "l1_in_proj_vd_ctx")
    k1_s = _matmul_heads(h, w_dkv, NA_WIDTH, NA_HEADS, rows_p, rows_s, len_s, "l1_in_proj_kd_lat")
    v1_s = _matmul_heads(h, w_dkv, 2 * NA_WIDTH, NA_HEADS, rows_p, rows_s, len_s, "l1_in_proj_vd_lat")
    cqn = _rmsnorm_cols(cqkv, l1_q_norm, 0, MLA_Q_LORA, BF16)
    ckvn = _rmsnorm_cols(cqkv, l1_kv_norm, MLA_Q_LORA // MLA_KV_LORA, MLA_KV_LORA, F32)
    q_all = _matmul(cqn, w_uq, BF16, 512, "l1_q_up")
    kv_all = _matmul(ckvn, l1_w_ukv, BF16, 512, "l1_kv_up")
    lc = cache_l1_ckv.shape[1]
    kv_ctx = _matmul(cache_l1_ckv.reshape(-1, MLA_KV_LORA), l1_w_ukv, BF16, 512, "l1_kv_up_ctx")

    out_ckv = ckvn[:rows_p].reshape(n_p, len_p, MLA_KV_LORA)
    out_kpe = kpe[:rows_p, :MLA_ROPE].reshape(n_p, len_p, MLA_ROPE)

    o_p = _ctx_odd_attention(q_all.reshape(seqs_p_units, len_p, -1),
                             kv_all.reshape(seqs_p_units, len_p, -1),
                             kpe.reshape(seqs_p_units, len_p, LANE),
                             qd.reshape(seqs_p_units, len_p, -1), out_k1, out_v1, n_p, len_p)
    cos_o, sin_o = _rope_tables(len_s, MLA_ROPE)
    kpe_ctx = jnp.pad(cache_l1_kpe, ((0, 0), (0, 0), (0, LANE - MLA_ROPE)))
    oc_s = _lat_mla_attention(q_all.reshape(seqs_s_units, len_s, -1),
                              kv_all.reshape(seqs_s_units, len_s, -1),
                              kv_ctx.reshape(n_s, lc, -1),
                              kpe.reshape(seqs_s_units, len_s, LANE), kpe_ctx,
                              seq0_s, n_s, len_s, cos_o, sin_o)
    od_s = _lat_na_attention(qd.reshape(seqs_s_units, len_s, -1), seq0_s, k1_s, v1_s,
                             cache_l1_k, cache_l1_v, _na_bias(l1_rpb, len_s))
    o_p2 = o_p.reshape(rows_p, 2 * half)
    x = _matmul_residual([[(o_p2, 0), (oc_s.reshape(rows_s, half), 0)],
                          [(o_p2, 1), (od_s.reshape(rows_s, half), 0)]],
                         l1_w_out.astype(BF16), [x], mod1, 2, 512, name="l1_out_proj", single_buffer_x=True, **groups)
    x = ffn(x, mod1, l1_norm2, l1_ffn_w_up, l1_ffn_conv_w, l1_ffn_conv_b, l1_ffn_w_down)

    y_prompt = _rmsnorm_cols(x, final_norm, 0, d, F32, 0, rows_p).reshape(n_p, len_p, d)
    y_sample = _rmsnorm_cols(x, final_norm, 0, d, F32, rows_p, rows_s).reshape(n_s, len_s, d)
    return (y_prompt, y_sample, out_k0, out_v0, out_sre, out_sim, out_ckv, out_kpe, out_k1, out_v1)
```

```python
import functools
import math

import jax
import jax.numpy as jnp
from jax import lax
from jax.experimental import pallas as pl
from jax.experimental.pallas import tpu as pltpu

F32 = jnp.float32
BF16 = jnp.bfloat16

D_MODEL = 4096
GRID_W = 64
HEAD_DIM = 128
S5_WIDTH = 2048
S5_GROUP = 16
S5_GROUPS = S5_WIDTH // S5_GROUP
S5_STATE = 64
S5_CHUNK = 16
WIN_HEADS = 16
WIN_KV_HEADS = 4
WIN_GROUP = WIN_HEADS // WIN_KV_HEADS
WIN_RADIUS = 128
WIN_Q_WIDTH = WIN_HEADS * HEAD_DIM
WIN_KV_WIDTH = WIN_KV_HEADS * HEAD_DIM
MLA_HEADS = 16
MLA_Q_LORA = 1024
MLA_KV_LORA = 512
MLA_NOPE = 128
MLA_ROPE = 64
MLA_V = 128
NA_HEADS = 16
NA_ROWS = 8
NA_COLS = 16
NA_WIDTH = NA_HEADS * HEAD_DIM
D_FF = 11008
ROPE_BASE = 10000.0
EPS = 1e-6
NEG = -1e30

LANE = 128
SUBLANE = 8
MAX_CONDS = 8
ROW_TILE = 1024


def _params(sem, vmem_mb):
    return pltpu.CompilerParams(dimension_semantics=sem, vmem_limit_bytes=vmem_mb * 1024 * 1024)


def _cond_index(i, bm, n_prompt_rows, lat_len):
    first = n_prompt_rows // bm
    per = lat_len // bm
    return jnp.where(i < first, 0, 1 + (i - first) // per)


def _dot(a, b):
    return jnp.dot(a, b, preferred_element_type=F32)


def _dot_t(a, b):
    return lax.dot_general(a, b, (((1,), (1,)), ((), ())), preferred_element_type=F32)


def _sigmoid(x):
    return 1.0 / (1.0 + jnp.exp(-x))


def _row_specs(parts, bm, width, col_fn, n_prompt_blocks, single_buffer=False):
    mode = dict(pipeline_mode=pl.Buffered(1)) if single_buffer else {}
    if len(parts) == 1:
        cb = parts[0][1]
        return [pl.BlockSpec((bm, width), lambda i, j: (i, col_fn(j, cb)), **mode)]
    cb_p, cb_s = parts[0][1], parts[1][1]
    last_p = n_prompt_blocks - 1
    return [pl.BlockSpec((bm, width), lambda i, j: (jnp.minimum(i, last_p), col_fn(j, cb_p)), **mode),
            pl.BlockSpec((bm, width), lambda i, j: (jnp.maximum(i - n_prompt_blocks, 0), col_fn(j, cb_s)), **mode)]


def _by_stream(i, n_prompt_blocks, operands, body):
    if all(len(o) == 1 for o in operands):
        body([o[0] for o in operands])
        return

    @pl.when(i < n_prompt_blocks)
    def _():
        body([o[0] for o in operands])

    @pl.when(i >= n_prompt_blocks)
    def _():
        body([o[-1] for o in operands])


def _mod_kernel(c_ref, w_ref, b_ref, o_ref):
    c = c_ref[...]
    s = c * _sigmoid(c)
    o_ref[...] = _dot(s.astype(BF16), w_ref[...].astype(BF16)) + b_ref[...]


def _modulation(cond, w, b):
    d, n = w.shape
    bn = 512
    out = pl.pallas_call(
        _mod_kernel,
        grid=(n // bn,),
        in_specs=[pl.BlockSpec((MAX_CONDS, d), lambda j: (0, 0)),
                  pl.BlockSpec((d, bn), lambda j: (0, j)),
                  pl.BlockSpec((1, bn), lambda j: (0, j))],
        out_specs=pl.BlockSpec((MAX_CONDS, bn), lambda j: (0, j)),
        out_shape=jax.ShapeDtypeStruct((MAX_CONDS, n), F32),
        compiler_params=_params(("arbitrary",), 40),
        name="modulation",
    )(cond, w, b.reshape(1, n))
    return out.reshape(MAX_CONDS, 1, n)


def _norm_mod_kernel(*refs, n_x, n_prompt_blocks):
    x_refs, (g_ref, sh_ref, sc_ref, o_ref) = refs[:n_x], refs[n_x:]

    def body(r):
        x = r[0][...]
        ms = jnp.mean(x * x, axis=-1, keepdims=True)
        y = x * lax.rsqrt(ms + EPS) * g_ref[...]
        o_ref[...] = (y * (1.0 + sc_ref[...]) + sh_ref[...]).astype(o_ref.dtype)

    _by_stream(pl.program_id(0), n_prompt_blocks, [x_refs], body)


def _norm_mod(x_parts, gain, mod, shift_slot, scale_slot, n_prompt_rows, lat_len):
    d = D_MODEL
    bm = 256
    t = sum(x.shape[0] for x in x_parts)
    npb = n_prompt_rows // bm
    cidx = functools.partial(_cond_index, bm=bm, n_prompt_rows=n_prompt_rows, lat_len=lat_len)
    specs = _row_specs([(x, 0) for x in x_parts], bm, d, lambda j, cb: 0, npb)
    to1d = lambda spec: pl.BlockSpec(spec.block_shape, lambda i, f=spec.index_map: f(i, 0))
    return pl.pallas_call(
        functools.partial(_norm_mod_kernel, n_x=len(x_parts), n_prompt_blocks=npb),
        grid=(t // bm,),
        in_specs=[to1d(s) for s in specs] + [
            pl.BlockSpec((1, d), lambda i: (0, 0)),
            pl.BlockSpec((None, 1, d), lambda i: (cidx(i), 0, shift_slot)),
            pl.BlockSpec((None, 1, d), lambda i: (cidx(i), 0, scale_slot))],
        out_specs=pl.BlockSpec((bm, d), lambda i: (i, 0)),
        out_shape=jax.ShapeDtypeStruct((t, d), BF16),
        compiler_params=_params(("arbitrary",), 40),
        name="norm_mod",
    )(*x_parts, gain.reshape(1, d), mod, mod)


def _rmsnorm_kernel(x_ref, g_ref, o_ref):
    x = x_ref[...].astype(F32)
    ms = jnp.mean(x * x, axis=-1, keepdims=True)
    o_ref[...] = (x * lax.rsqrt(ms + EPS) * g_ref[...]).astype(o_ref.dtype)


def _rmsnorm_cols(x, gain, col_block, width, out_dtype, row0=0, rows=None):
    bm = 256
    rows = x.shape[0] if rows is None else rows
    rb0 = row0 // bm
    return pl.pallas_call(
        _rmsnorm_kernel,
        grid=(rows // bm,),
        in_specs=[pl.BlockSpec((bm, width), lambda i: (i + rb0, col_block)),
                  pl.BlockSpec((1, width), lambda i: (0, 0))],
        out_specs=pl.BlockSpec((bm, width), lambda i: (i, 0)),
        out_shape=jax.ShapeDtypeStruct((rows, width), out_dtype),
        compiler_params=_params(("arbitrary",), 40),
        name="rmsnorm",
    )(x, gain.reshape(1, width))


def _mm_kernel(x_ref, w_ref, o_ref):
    o_ref[...] = _dot(x_ref[...].astype(BF16), w_ref[...].astype(BF16)).astype(o_ref.dtype)


def _matmul(x, w, out_dtype, bn, name, col0=0, ncols=None):
    m, k = x.shape
    ncols = w.shape[1] - col0 if ncols is None else ncols
    bm = min(ROW_TILE, m)
    assert m % bm == 0 and ncols % bn == 0 and col0 % bn == 0
    cb0 = col0 // bn
    return pl.pallas_call(
        _mm_kernel,
        grid=(m // bm, ncols // bn),
        in_specs=[pl.BlockSpec((bm, k), lambda i, j: (i, 0)),
                  pl.BlockSpec((k, bn), lambda i, j: (0, cb0 + j))],
        out_specs=pl.BlockSpec((bm, bn), lambda i, j: (i, j)),
        out_shape=jax.ShapeDtypeStruct((m, ncols), out_dtype),
        compiler_params=_params(("arbitrary", "arbitrary"), 56),
        name=name,
    )(x, w)


def _mm_heads_kernel(x_ref, w_ref, o_ref, *, seqs, seq_len, heads):
    acc = _dot(x_ref[...], w_ref[...].astype(BF16))
    for b in range(seqs):
        for hh in range(heads):
            o_ref[b, hh] = acc[b * seq_len:(b + 1) * seq_len, hh * HEAD_DIM:(hh + 1) * HEAD_DIM]


def _matmul_heads(x, w, col0, heads, row0, rows, seq_len, name):
    k = x.shape[1]
    bm = ROW_TILE
    hb = 4
    bn = hb * HEAD_DIM
    assert rows % bm == 0 and row0 % bm == 0 and bm % seq_len == 0 and heads % hb == 0 and col0 % bn == 0
    seqs = bm // seq_len
    rb0, cb0 = row0 // bm, col0 // bn
    return pl.pallas_call(
        functools.partial(_mm_heads_kernel, seqs=seqs, seq_len=seq_len, heads=hb),
        grid=(rows // bm, heads // hb),
        in_specs=[pl.BlockSpec((bm, k), lambda i, j: (rb0 + i, 0)),
                  pl.BlockSpec((k, bn), lambda i, j: (0, cb0 + j))],
        out_specs=pl.BlockSpec((seqs, hb, seq_len, HEAD_DIM), lambda i, j: (i, j, 0, 0)),
        out_shape=jax.ShapeDtypeStruct((rows // seq_len, heads, seq_len, HEAD_DIM), F32),
        compiler_params=_params(("arbitrary", "arbitrary"), 56),
        name=name,
    )(x, w)


def _mm_res_kernel(*refs, n_lhs, n_parts, n_res, n_prompt_blocks):
    pos = 0
    lhs = []
    for n in n_lhs:
        lhs.append(refs[pos:pos + n])
        pos += n
    w_refs = refs[pos:pos + n_parts]
    pos += n_parts
    res = refs[pos:pos + n_res]
    pos += n_res
    gate_ref, o_ref = refs[pos], refs[pos + 1]

    def body(r):
        acc = None
        for x_ref, w_ref in zip(r[:n_parts], w_refs):
            part = _dot(x_ref[...], w_ref[...].astype(BF16))
            acc = part if acc is None else acc + part
        o_ref[...] = r[n_parts][...] + gate_ref[...] * acc

    _by_stream(pl.program_id(0), n_prompt_blocks, lhs + [res], body)


def _matmul_residual(lhs_parts, w, res_parts, mod, gate_slot, bn, n_prompt_rows, lat_len, name,
                     single_buffer_x=False, vmem_mb=56):
    bm = ROW_TILE
    n = w.shape[1]
    kq = w.shape[0] // len(lhs_parts)
    t = sum(r.shape[0] for r in res_parts)
    npb = n_prompt_rows // bm
    cidx = functools.partial(_cond_index, bm=bm, n_prompt_rows=n_prompt_rows, lat_len=lat_len)
    gate_col0 = gate_slot * (D_MODEL // bn)
    in_specs, args = [], []
    for parts in lhs_parts:
        in_specs += _row_specs(parts, bm, kq, lambda j, cb: cb, npb, single_buffer=single_buffer_x)
        args += [a for a, _ in parts]
    for q in range(len(lhs_parts)):
        in_specs.append(pl.BlockSpec((kq, bn), lambda i, j, q=q: (q, j)))
        args.append(w)
    in_specs += _row_specs([(r, 0) for r in res_parts], bm, bn, lambda j, cb: j, npb)
    args += list(res_parts)
    in_specs.append(pl.BlockSpec((None, 1, bn), lambda i, j: (cidx(i), 0, gate_col0 + j)))
    args.append(mod)
    body = functools.partial(_mm_res_kernel, n_lhs=tuple(len(p) for p in lhs_parts), n_parts=len(lhs_parts),
                             n_res=len(res_parts), n_prompt_blocks=npb)
    return pl.pallas_call(
        body,
        grid=(t // bm, n // bn),
        in_specs=in_specs,
        out_specs=pl.BlockSpec((bm, bn), lambda i, j: (i, j)),
        out_shape=jax.ShapeDtypeStruct((t, n), F32),
        compiler_params=_params(("arbitrary", "arbitrary"), vmem_mb),
        name=name,
    )(*args)


def _glu_kernel(*refs, n_y, n_prompt_blocks):
    y_refs = refs[:n_y]
    w_ref, b_ref, o_ref = refs[n_y:]

    def body(r):
        y = r[0][...]
        z = _dot(y.astype(BF16), w_ref[...]) + b_ref[...]
        o_ref[...] = (y * _sigmoid(z)).astype(o_ref.dtype)

    _by_stream(pl.program_id(0), n_prompt_blocks, [y_refs], body)


def _glu(y_parts, w, b, n_prompt_rows):
    k, n = w.shape
    bm = ROW_TILE
    m = sum(y.shape[0] for y in y_parts)
    npb = n_prompt_rows // bm
    parts = [(y, 0) for y in y_parts]
    return pl.pallas_call(
        functools.partial(_glu_kernel, n_y=len(y_parts), n_prompt_blocks=npb),
        grid=(m // bm, 1),
        in_specs=(_row_specs(parts, bm, k, lambda j, cb: 0, npb, single_buffer=True)
                  + [pl.BlockSpec((k, n), lambda i, j: (0, 0), pipeline_mode=pl.Buffered(1)),
                     pl.BlockSpec((1, n), lambda i, j: (0, 0))]),
        out_specs=pl.BlockSpec((bm, n), lambda i, j: (i, 0)),
        out_shape=jax.ShapeDtypeStruct((m, n), BF16),
        compiler_params=_params(("arbitrary", "arbitrary"), 56),
        name="s5_glu",
    )(*y_parts, w.astype(BF16), b.reshape(1, n))


def _ffn_up_kernel(x_ref, wg_ref, wv_ref, cwg_ref, cwv_ref, cbg_ref, cbv_ref, o_ref, *,
                   sub, n_sub, prompt_blocks, prompt_len, lat_len):
    i = pl.program_id(0)
    row = lax.broadcasted_iota(jnp.int32, (sub, 1), 0)
    wg = wg_ref[...].astype(BF16)
    wv = wv_ref[...].astype(BF16)

    for s in range(n_sub):
        period = jnp.where(i * n_sub + s < prompt_blocks, prompt_len, lat_len)
        pos = row & (period - 1)
        first = pos == 0
        last = pos == period - 1

        def conv(u, cw_ref, cb_ref):
            prev = jnp.where(first, 0.0, pltpu.roll(u, 1, 0))
            nxt = jnp.where(last, 0.0, pltpu.roll(u, sub - 1, 0))
            return prev * cw_ref[0:1, :] + u * cw_ref[1:2, :] + nxt * cw_ref[2:3, :] + cb_ref[...]

        x = x_ref[s * sub:(s + 1) * sub, :]
        g = conv(_dot(x, wg), cwg_ref, cbg_ref)
        v = conv(_dot(x, wv), cwv_ref, cbv_ref)
        o_ref[s * sub:(s + 1) * sub, :] = (g * _sigmoid(g) * v).astype(o_ref.dtype)


FFN_SUB_BLOCKS = 3


def _ffn_up(h, w_up, conv_w, conv_b, n_prompt_rows, prompt_len, lat_len):
    t, d = h.shape
    sub, bn = lat_len, 256
    n_sub = FFN_SUB_BLOCKS if (t // sub) % FFN_SUB_BLOCKS == 0 else 2
    bm = sub * n_sub
    assert t % bm == 0 and n_prompt_rows % sub == 0 and sub % prompt_len == 0
    assert prompt_len & (prompt_len - 1) == 0 and lat_len & (lat_len - 1) == 0
    nj = D_FF // bn
    conv_b = conv_b.reshape(1, 2 * D_FF)
    body = functools.partial(_ffn_up_kernel, sub=sub, n_sub=n_sub, prompt_blocks=n_prompt_rows // sub,
                             prompt_len=prompt_len, lat_len=lat_len)
    return pl.pallas_call(
        body,
        grid=(t // bm, nj),
        in_specs=[pl.BlockSpec((bm, d), lambda i, j: (i, 0), pipeline_mode=pl.Buffered(1)),
                  pl.BlockSpec((d, bn), lambda i, j: (0, j)),
                  pl.BlockSpec((d, bn), lambda i, j: (0, nj + j)),
                  pl.BlockSpec((3, bn), lambda i, j: (0, j)),
                  pl.BlockSpec((3, bn), lambda i, j: (0, nj + j)),
                  pl.BlockSpec((1, bn), lambda i, j: (0, j)),
                  pl.BlockSpec((1, bn), lambda i, j: (0, nj + j))],
        out_specs=pl.BlockSpec((bm, bn), lambda i, j: (i, j)),
        out_shape=jax.ShapeDtypeStruct((t, D_FF), BF16),
        compiler_params=_params(("arbitrary", "arbitrary"), 56),
        name="ffn_up_conv",
    )(h, w_up, w_up, conv_w, conv_w, conv_b, conv_b)


def _s5_operators(lam_re, lam_im, log_dt, b_re, b_im, c_re, c_im, d_skip):
    q, g, p, c = S5_CHUNK, S5_GROUPS, S5_STATE, S5_GROUP
    dt = jnp.exp(log_dt)[None, :, :, None]
    tau = jnp.arange(q + 1, dtype=F32)[:, None, None, None]
    mag = jnp.exp(lam_re[None] * dt * tau)
    ang = lam_im[None] * dt * tau
    pw_re, pw_im = mag * jnp.cos(ang), mag * jnp.sin(ang)
    dt1 = jnp.exp(log_dt)[:, :, None]
    m1 = jnp.exp(lam_re * dt1)
    ab_re, ab_im = m1 * jnp.cos(lam_im * dt1), m1 * jnp.sin(lam_im * dt1)
    den = lam_re * lam_re + lam_im * lam_im
    nr, ni = ab_re - 1.0, ab_im
    f_re = ((nr * lam_re + ni * lam_im) / den)[..., None]
    f_im = ((ni * lam_re - nr * lam_im) / den)[..., None]
    bb_re = f_re * b_re - f_im * b_im
    bb_im = f_re * b_im + f_im * b_re
    ct_re, ct_im = jnp.swapaxes(c_re, -1, -2), jnp.swapaxes(c_im, -1, -2)

    def power_times_c(fwd_taus, bwd_taus):
        pr = jnp.stack([pw_re[fwd_taus, 0], pw_re[bwd_taus, 1]], axis=0).transpose(0, 2, 3, 1)
        pi = jnp.stack([pw_im[fwd_taus, 0], pw_im[bwd_taus, 1]], axis=0).transpose(0, 2, 3, 1)
        r_re = pr[..., None] * ct_re[:, :, :, None, :] - pi[..., None] * ct_im[:, :, :, None, :]
        r_im = pr[..., None] * ct_im[:, :, :, None, :] + pi[..., None] * ct_re[:, :, :, None, :]
        return r_re.reshape(2, g, p, q * c), r_im.reshape(2, g, p, q * c)

    lags = jnp.arange(q)
    rk_re, rk_im = power_times_c(lags, lags[::-1])
    diag = jnp.tile(d_skip[:, :, None] * jnp.eye(c, dtype=F32)[None], (1, 1, q))
    t_op = _s5_toeplitz(rk_re, rk_im, bb_re, bb_im, diag)
    re_re, re_im = power_times_c(lags + 1, q - lags)
    zero = jnp.zeros_like(re_re[0])
    e_op = jnp.concatenate([re_re[0], zero, -re_im[0], zero, zero, re_re[1], zero, -re_im[1]], axis=1)
    def s_op(taus, d):
        pr, pi = pw_re[taus, d][:, :, None, :], pw_im[taus, d][:, :, None, :]
        br, bi = jnp.swapaxes(bb_re[d], -1, -2)[None], jnp.swapaxes(bb_im[d], -1, -2)[None]
        s_re = (pr * br - pi * bi).transpose(1, 0, 2, 3).reshape(g, q * c, p)
        s_im = (pr * bi + pi * br).transpose(1, 0, 2, 3).reshape(g, q * c, p)
        return s_re, s_im

    sf_re, sf_im = s_op(lags[::-1], 0)
    sb_re, sb_im = s_op(lags, 1)
    sb = jnp.concatenate([sf_re, sb_re, sf_im, sb_im], axis=2)
    a16 = jnp.concatenate([pw_re[q, 0], pw_re[q, 1], pw_im[q, 0], pw_im[q, 1]], axis=-1)[:, None, :]
    return sb.astype(BF16), t_op, e_op.astype(BF16), a16


def _s5_toeplitz_kernel(rr_ref, ri_ref, br_ref, bi_ref, dg_ref, o_ref, k_scr):
    c, q = S5_GROUP, S5_CHUNK
    for d in range(2):
        for ci in range(c):
            prod = br_ref[d, :, ci:ci + 1] * rr_ref[d] - bi_ref[d, :, ci:ci + 1] * ri_ref[d]
            k_scr[d * c + ci:d * c + ci + 1, :] = jnp.sum(prod, axis=0, keepdims=True)
    kf = k_scr[0:c, :]
    kb = k_scr[c:2 * c, :]
    lane = lax.broadcasted_iota(jnp.int32, (c, q * c), 1)
    dg = dg_ref[...]
    for s in range(q):
        f = kf if s == 0 else pltpu.roll(kf, c * s, 1)
        b = kb if s == q - 1 else pltpu.roll(kb, q * c - c * (q - 1 - s), 1)
        piece = jnp.where(lane >= c * s, f, 0.0) + jnp.where(lane < c * (s + 1), b, 0.0)
        piece = piece + jnp.where(jnp.logical_and(lane >= c * s, lane < c * (s + 1)), dg, 0.0)
        o_ref[s * c:(s + 1) * c, :] = piece.astype(o_ref.dtype)


def _s5_toeplitz(rk_re, rk_im, bb_re, bb_im, diag):
    g, p, c, qc = S5_GROUPS, S5_STATE, S5_GROUP, S5_CHUNK * S5_GROUP
    r_spec = pl.BlockSpec((2, None, p, qc), lambda i: (0, i, 0, 0))
    b_spec = pl.BlockSpec((2, None, p, c), lambda i: (0, i, 0, 0))
    return pl.pallas_call(
        _s5_toeplitz_kernel,
        grid=(g,),
        in_specs=[r_spec, r_spec, b_spec, b_spec, pl.BlockSpec((None, c, qc), lambda i: (i, 0, 0))],
        out_specs=pl.BlockSpec((None, qc, qc), lambda i: (i, 0, 0)),
        out_shape=jax.ShapeDtypeStruct((g, qc, qc), BF16),
        scratch_shapes=[pltpu.VMEM((2 * c, qc), F32)],
        compiler_params=_params(("arbitrary",), 32),
        name="s5_toeplitz",
    )(rk_re, rk_im, bb_re, bb_im, diag)


def _gelu_tanh(x):
    return 0.5 * x * (1.0 + jnp.tanh(math.sqrt(2.0 / math.pi) * (x + 0.044715 * (x * x * x))))


GROUPS_PER_STEP = LANE // S5_GROUP
STATE_W = GROUPS_PER_STEP * 2 * S5_STATE


def _s5_lane_permutation():
    j = jnp.arange(S5_CHUNK * LANE)
    s, gl, c = j // LANE, (j % LANE) // S5_GROUP, j % S5_GROUP
    k = gl * (S5_CHUNK * S5_GROUP) + s * S5_GROUP + c
    return (k[:, None] == jnp.arange(S5_CHUNK * LANE)[None, :]).astype(BF16)


def _s5_kernel(u_ref, sel_ref, sb_ref, t_ref, e_ref, a_ref, h0_ref, y_ref, hfin_ref, xg, w_scr, h_scr, *,
               batch, nb, nc, seq_len):
    qc = S5_CHUNK * S5_GROUP
    if batch < nb:
        xg[...] = jnp.zeros(xg.shape, BF16)
    for ch in range(nc):
        for s in range(S5_CHUNK):
            xg[ch * nb:ch * nb + batch, s * LANE:(s + 1) * LANE] = (
                u_ref[pl.ds(ch * S5_CHUNK + s, batch, stride=seq_len), :].astype(BF16))
    sel = sel_ref[...]
    ucat = _dot(xg[...], sel).astype(BF16)
    for gl in range(GROUPS_PER_STEP):
        w = _dot(ucat[:, gl * qc:(gl + 1) * qc], sb_ref[gl])
        w_scr[:, gl * LANE:(gl + 1) * LANE] = w[:, 0:LANE]
        w_scr[:, STATE_W + gl * LANE:STATE_W + (gl + 1) * LANE] = w[:, LANE:2 * LANE]
    ar = a_ref[0:1, 0:STATE_W]
    ai = a_ref[0:1, STATE_W:2 * STATE_W]
    fwd_lane = (lax.broadcasted_iota(jnp.int32, (1, STATE_W), 1) & S5_STATE) == 0
    hr = h0_ref[:, 0:STATE_W]
    hi = h0_ref[:, STATE_W:2 * STATE_W]
    for i in range(nc):
        ri = i * nb
        rj = (nc - 1 - i) * nb
        h_scr[ri:ri + nb, 0:STATE_W] = hr
        h_scr[ri:ri + nb, STATE_W:2 * STATE_W] = hi
        h_scr[rj:rj + nb, 2 * STATE_W:3 * STATE_W] = hr
        h_scr[rj:rj + nb, 3 * STATE_W:4 * STATE_W] = hi
        wr = jnp.where(fwd_lane, w_scr[ri:ri + nb, 0:STATE_W], w_scr[rj:rj + nb, 0:STATE_W])
        wi = jnp.where(fwd_lane, w_scr[ri:ri + nb, STATE_W:2 * STATE_W], w_scr[rj:rj + nb, STATE_W:2 * STATE_W])
        hr, hi = ar * hr - ai * hi + wr, ar * hi + ai * hr + wi
    hfin_ref[:, 0:STATE_W] = hr
    hfin_ref[:, STATE_W:2 * STATE_W] = hi
    for gl in range(GROUPS_PER_STEP):
        hcat = jnp.concatenate([h_scr[:, k * STATE_W + gl * LANE:k * STATE_W + (gl + 1) * LANE] for k in range(4)],
                               axis=1).astype(BF16)
        y = _dot(ucat[:, gl * qc:(gl + 1) * qc], t_ref[gl]) + _dot(hcat, e_ref[gl])
        xg[:, gl * qc:(gl + 1) * qc] = _gelu_tanh(y).astype(BF16)
    yp = _dot_t(xg[...], sel)
    for ch in range(nc):
        for s in range(S5_CHUNK):
            y_ref[pl.ds(ch * S5_CHUNK + s, batch, stride=seq_len), :] = (
                yp[ch * nb:ch * nb + batch, s * LANE:(s + 1) * LANE])


def _s5_stream(u, row_block, batch, seq_len, sel, sb, t_op, e_op, a_planes, h0):
    n_rows = batch * seq_len
    nb = -(-batch // SUBLANE) * SUBLANE
    nc = seq_len // S5_CHUNK
    rows = nb * nc
    nblk = S5_WIDTH // LANE
    qc = S5_CHUNK * S5_GROUP
    body = functools.partial(_s5_kernel, batch=batch, nb=nb, nc=nc, seq_len=seq_len)
    return pl.pallas_call(
        body,
        grid=(nblk,),
        in_specs=[pl.BlockSpec((n_rows, LANE), lambda g: (row_block, g), pipeline_mode=pl.Buffered(1)),
                  pl.BlockSpec((S5_CHUNK * LANE, S5_CHUNK * LANE), lambda g: (0, 0), pipeline_mode=pl.Buffered(1)),
                  pl.BlockSpec((GROUPS_PER_STEP, qc, 4 * S5_STATE), lambda g: (g, 0, 0)),
                  pl.BlockSpec((GROUPS_PER_STEP, qc, qc), lambda g: (g, 0, 0)),
                  pl.BlockSpec((GROUPS_PER_STEP, 8 * S5_STATE, qc), lambda g: (g, 0, 0)),
                  pl.BlockSpec((None, 1, 2 * STATE_W), lambda g: (g, 0, 0)),
                  pl.BlockSpec((None, nb, 2 * STATE_W), lambda g: (g, 0, 0))],
        out_specs=[pl.BlockSpec((n_rows, LANE), lambda g: (0, g)),
                   pl.BlockSpec((None, nb, 2 * STATE_W), lambda g: (g, 0, 0))],
        out_shape=[jax.ShapeDtypeStruct((n_rows, S5_WIDTH), F32),
                   jax.ShapeDtypeStruct((nblk, nb, 2 * STATE_W), F32)],
        scratch_shapes=[pltpu.VMEM((rows, S5_CHUNK * LANE), BF16),
                        pltpu.VMEM((rows, 2 * STATE_W), F32),
                        pltpu.VMEM((rows, 4 * STATE_W), F32)],
        compiler_params=_params(("arbitrary",), 56),
        name="s5_chunked",
    )(u, sel, sb, t_op, e_op, a_planes, h0)


def _rope_tables(length, rot):
    n_freq = rot // 4
    t = jnp.arange(length)
    row = (t // GRID_W).astype(F32)
    col = (t % GRID_W).astype(F32)
    inv = ROPE_BASE ** (-jnp.arange(n_freq, dtype=F32) / n_freq)
    ar, ac = row[:, None] * inv, col[:, None] * inv
    cos = jnp.concatenate([jnp.cos(ar), jnp.cos(ar), jnp.cos(ac), jnp.cos(ac)], axis=-1)
    sin = jnp.concatenate([-jnp.sin(ar), jnp.sin(ar), -jnp.sin(ac), jnp.sin(ac)], axis=-1)
    pad = LANE - rot
    if pad:
        cos = jnp.concatenate([cos, jnp.ones((length, pad), F32)], axis=-1)
        sin = jnp.concatenate([sin, jnp.zeros((length, pad), F32)], axis=-1)
    return cos, sin


def _rope(x, cos, sin, blk):
    lane = lax.broadcasted_iota(jnp.int32, (1, LANE), 1)
    lower = (lane & blk) == 0
    partner = jnp.where(lower, pltpu.roll(x, LANE - blk, 1), pltpu.roll(x, blk, 1))
    return x * cos + partner * sin


def _softmax_pv(scores, values, extra_logit=None):
    m = scores[0].max(axis=-1, keepdims=True)
    for s in scores[1:]:
        m = jnp.maximum(m, s.max(axis=-1, keepdims=True))
    if extra_logit is not None:
        m = jnp.maximum(m, extra_logit)
    den = None
    out = None
    for s, v in zip(scores, values):
        p = jnp.exp(s - m)
        d = p.sum(axis=-1, keepdims=True)
        o = _dot(p.astype(BF16), v)
        den = d if den is None else den + d
        out = o if out is None else out + o
    if extra_logit is not None:
        den = den + jnp.exp(extra_logit - m)
    return out / den


def _ctx_gqa_kernel(sink_ref, q_ref, k_ref, v_ref, o_ref):
    g = pl.program_id(1)
    scale = HEAD_DIM ** -0.5
    k = k_ref[...].astype(BF16)
    v = v_ref[...].astype(BF16)
    for r in range(WIN_GROUP):
        q = q_ref[:, r * HEAD_DIM:(r + 1) * HEAD_DIM]
        s = _dot_t(q, k) * scale
        o = _softmax_pv([s], [v], sink_ref[g * WIN_GROUP + r])
        o_ref[:, r * HEAD_DIM:(r + 1) * HEAD_DIM] = o.astype(o_ref.dtype)


def _ctx_gqa_attention(uq3, k4, v4, sink, n_prompt, length):
    qw = WIN_GROUP * HEAD_DIM
    q0 = 0
    kv_spec = pl.BlockSpec((None, None, length, HEAD_DIM), lambda b, g: (b, g, 0, 0))
    return pl.pallas_call(
        _ctx_gqa_kernel,
        grid=(n_prompt, WIN_KV_HEADS),
        in_specs=[pl.BlockSpec(memory_space=pltpu.SMEM),
                  pl.BlockSpec((None, length, qw), lambda b, g: (b, 0, q0 + g)),
                  kv_spec, kv_spec],
        out_specs=pl.BlockSpec((None, length, qw), lambda b, g: (b, 0, g)),
        out_shape=jax.ShapeDtypeStruct((n_prompt, length, WIN_Q_WIDTH), BF16),
        compiler_params=_params(("arbitrary", "arbitrary"), 40),
        name="ctx_gqa_attention",
    )(sink, uq3, k4, v4)


def _lat_window_kernel(sink_ref, q_ref, k_ref, v_ref, kc_ref, vc_ref, cos_ref, sin_ref, o_ref, *, length):
    g = pl.program_id(1)
    scale = HEAD_DIM ** -0.5
    blk = WIN_RADIUS
    nb = length // blk
    k = _rope(k_ref[...], cos_ref[...], sin_ref[...], HEAD_DIM // 4).astype(BF16)
    v = v_ref[...].astype(BF16)
    kc = kc_ref[...].astype(BF16)
    vc = vc_ref[...].astype(BF16)
    rows = WIN_GROUP * blk
    row = lax.broadcasted_iota(jnp.int32, (rows, 1), 0)
    sk = jnp.zeros((rows, 1), F32)
    for r in range(WIN_GROUP):
        sk = jnp.where(jnp.logical_and(row >= r * blk, row < (r + 1) * blk), sink_ref[g * WIN_GROUP + r], sk)
    qoff = row & (blk - 1)
    for n in range(nb):
        lo = max(0, n - 1) * blk
        hi = min(nb, n + 2) * blk
        cos = cos_ref[n * blk:(n + 1) * blk, :]
        sin = sin_ref[n * blk:(n + 1) * blk, :]
        q = jnp.concatenate(
            [_rope(q_ref[n * blk:(n + 1) * blk, r * HEAD_DIM:(r + 1) * HEAD_DIM].astype(F32), cos, sin, HEAD_DIM // 4)
             for r in range(WIN_GROUP)], axis=0).astype(BF16)
        dist = (n * blk + qoff) - (lo + lax.broadcasted_iota(jnp.int32, (1, hi - lo), 1))
        visible = jnp.logical_and(dist <= WIN_RADIUS, dist >= -WIN_RADIUS)
        s_loc = jnp.where(visible, _dot_t(q, k[lo:hi]) * scale, NEG)
        s_ctx = _dot_t(q, kc) * scale
        o = _softmax_pv([s_loc, s_ctx], [v[lo:hi], vc], sk)
        for r in range(WIN_GROUP):
            o_ref[n * blk:(n + 1) * blk, r * HEAD_DIM:(r + 1) * HEAD_DIM] = o[r * blk:(r + 1) * blk].astype(o_ref.dtype)


def _lat_window_attention(uq3, seq0, k4, v4, k_ctx, v_ctx, sink, cos, sin):
    n_lat, _, length, _ = k4.shape
    qw = WIN_GROUP * HEAD_DIM
    q0 = 0
    lc = k_ctx.shape[2]
    kv_spec = pl.BlockSpec((None, None, length, HEAD_DIM), lambda b, g: (b, g, 0, 0))
    ctx_spec = pl.BlockSpec((None, None, lc, HEAD_DIM), lambda b, g: (b, g, 0, 0))
    tab_spec = pl.BlockSpec((length, LANE), lambda b, g: (0, 0))
    return pl.pallas_call(
        functools.partial(_lat_window_kernel, length=length),
        grid=(n_lat, WIN_KV_HEADS),
        in_specs=[pl.BlockSpec(memory_space=pltpu.SMEM),
                  pl.BlockSpec((None, length, qw), lambda b, g: (seq0 + b, 0, q0 + g)),
                  kv_spec, kv_spec, ctx_spec, ctx_spec, tab_spec, tab_spec],
        out_specs=pl.BlockSpec((None, length, qw), lambda b, g: (b, 0, g)),
        out_shape=jax.ShapeDtypeStruct((n_lat, length, WIN_Q_WIDTH), BF16),
        compiler_params=_params(("arbitrary", "arbitrary"), 48),
        name="latent_window_attention",
    )(sink, uq3, k4, v4, k_ctx, v_ctx, cos, sin)


def _ctx_odd_kernel(q_ref, kv_ref, kpe_ref, qd_ref, kd_ref, vd_ref, o_ref):
    kw = MLA_NOPE + LANE
    kpe = kpe_ref[...].astype(BF16)
    scale_c = (MLA_NOPE + MLA_ROPE) ** -0.5
    for h in range(MLA_HEADS):
        q = q_ref[:, h * kw:(h + 1) * kw]
        k = jnp.concatenate([kv_ref[:, h * kw:h * kw + MLA_NOPE], kpe], axis=1)
        v = kv_ref[:, h * kw + MLA_NOPE:(h + 1) * kw]
        o = _softmax_pv([_dot_t(q, k) * scale_c], [v])
        o_ref[:, h * MLA_V:(h + 1) * MLA_V] = o.astype(o_ref.dtype)
    scale_d = HEAD_DIM ** -0.5
    base = MLA_HEADS * MLA_V
    for h in range(NA_HEADS):
        q = qd_ref[:, h * HEAD_DIM:(h + 1) * HEAD_DIM]
        o = _softmax_pv([_dot_t(q, kd_ref[h].astype(BF16)) * scale_d], [vd_ref[h].astype(BF16)])
        o_ref[:, base + h * HEAD_DIM:base + (h + 1) * HEAD_DIM] = o.astype(o_ref.dtype)


def _ctx_odd_attention(q3, kv3, kpe3, qd3, kd4, vd4, n_prompt, length):
    qn = q3.shape[-1]
    head_spec = pl.BlockSpec((None, NA_HEADS, length, HEAD_DIM), lambda b: (b, 0, 0, 0))
    return pl.pallas_call(
        _ctx_odd_kernel,
        grid=(n_prompt,),
        in_specs=[pl.BlockSpec((None, length, qn), lambda b: (b, 0, 0)),
                  pl.BlockSpec((None, length, qn), lambda b: (b, 0, 0)),
                  pl.BlockSpec((None, length, LANE), lambda b: (b, 0, 0)),
                  pl.BlockSpec((None, length, NA_WIDTH), lambda b: (b, 0, 0)),
                  head_spec, head_spec],
        out_specs=pl.BlockSpec((None, length, MLA_HEADS * MLA_V + NA_WIDTH), lambda b: (b, 0, 0)),
        out_shape=jax.ShapeDtypeStruct((n_prompt, length, MLA_HEADS * MLA_V + NA_WIDTH), BF16),
        compiler_params=_params(("arbitrary",), 48),
        name="ctx_odd_attention",
    )(q3, kv3, kpe3, qd3, kd4, vd4)


def _lat_mla_kernel(q_ref, kv_ref, kpe_ref, kvc_ref, kpec_ref, cos_ref, sin_ref, o_ref, *, length):
    scale = (MLA_NOPE + MLA_ROPE) ** -0.5
    rb = MLA_ROPE // 4
    kpe = _rope(kpe_ref[...], cos_ref[...], sin_ref[...], rb).astype(BF16)
    k_lat = jnp.concatenate([kv_ref[:, 0:MLA_NOPE], kpe], axis=1)
    v_lat = kv_ref[:, MLA_NOPE:]
    k_ctx = jnp.concatenate([kvc_ref[:, 0:MLA_NOPE], kpec_ref[...].astype(BF16)], axis=1)
    v_ctx = kvc_ref[:, MLA_NOPE:]
    qb = 256
    for n in range(length // qb):
        rows = slice(n * qb, (n + 1) * qb)
        q_pe = _rope(q_ref[rows, MLA_NOPE:].astype(F32), cos_ref[rows, :], sin_ref[rows, :], rb).astype(BF16)
        q = jnp.concatenate([q_ref[rows, 0:MLA_NOPE], q_pe], axis=1)
        o = _softmax_pv([_dot_t(q, k_lat) * scale, _dot_t(q, k_ctx) * scale], [v_lat, v_ctx])
        o_ref[rows, :] = o.astype(o_ref.dtype)


def _lat_mla_attention(q3, kv3, kvc3, kpe3, kpe_ctx, seq0, n_lat, length, cos, sin):
    kw = MLA_NOPE + LANE
    lc = kpe_ctx.shape[1]
    return pl.pallas_call(
        functools.partial(_lat_mla_kernel, length=length),
        grid=(n_lat, MLA_HEADS),
        in_specs=[pl.BlockSpec((None, length, kw), lambda b, h: (seq0 + b, 0, h)),
                  pl.BlockSpec((None, length, kw), lambda b, h: (seq0 + b, 0, h)),
                  pl.BlockSpec((None, length, LANE), lambda b, h: (seq0 + b, 0, 0)),
                  pl.BlockSpec((None, lc, kw), lambda b, h: (b, 0, h)),
                  pl.BlockSpec((None, lc, LANE), lambda b, h: (b, 0, 0)),
                  pl.BlockSpec((length, LANE), lambda b, h: (0, 0)),
                  pl.BlockSpec((length, LANE), lambda b, h: (0, 0))],
        out_specs=pl.BlockSpec((None, length, MLA_V), lambda b, h: (b, 0, h)),
        out_shape=jax.ShapeDtypeStruct((n_lat, length, MLA_HEADS * MLA_V), BF16),
        compiler_params=_params(("arbitrary", "arbitrary"), 48),
        name="latent_mla_attention",
    )(q3, kv3, kpe3, kvc3, kpe_ctx, cos, sin)


def _na_row_start(r, rows):
    kr = min(NA_ROWS, rows)
    return min(max(r - kr // 2, 0), rows - kr)


def _lat_na_kernel(q_ref, k_ref, v_ref, kc_ref, vc_ref, bias_ref, o_ref, *, length):
    scale = HEAD_DIM ** -0.5
    rows = length // GRID_W
    kr = min(NA_ROWS, rows)
    k = k_ref[...].astype(BF16)
    v = v_ref[...].astype(BF16)
    kc = kc_ref[...].astype(BF16)
    vc = vc_ref[...].astype(BF16)
    r = 0
    while r < rows:
        r_end = r + 1
        while r_end < rows and _na_row_start(r_end, rows) == _na_row_start(r, rows):
            r_end += 1
        r0 = _na_row_start(r, rows) * GRID_W
        nq = (r_end - r) * GRID_W
        q = q_ref[r * GRID_W:r_end * GRID_W, :]
        bias = bias_ref[r:r_end].reshape(nq, kr * GRID_W)
        s_nb = _dot_t(q, k[r0:r0 + kr * GRID_W]) * scale + bias
        s_ctx = _dot_t(q, kc) * scale
        o = _softmax_pv([s_nb, s_ctx], [v[r0:r0 + kr * GRID_W], vc])
        o_ref[r * GRID_W:r_end * GRID_W, :] = o.astype(o_ref.dtype)
        r = r_end


def _na_bias(rpb, length):
    rows = length // GRID_W
    kr = min(NA_ROWS, rows)
    col = jnp.arange(GRID_W)
    c_start = jnp.clip(col - NA_COLS // 2, 0, GRID_W - NA_COLS)
    col_valid = (col[None, :] >= c_start[:, None]) & (col[None, :] < c_start[:, None] + NA_COLS)
    off_c = jnp.clip(col[None, :] - col[:, None], -(NA_COLS - 1), NA_COLS - 1) + NA_COLS - 1
    onehot = (off_c[:, :, None] == jnp.arange(2 * NA_COLS - 1)[None, None, :]).astype(F32)
    table = jnp.einsum('hdj,qkj->hdqk', rpb.astype(F32), onehot, precision=lax.Precision.HIGHEST)
    table = jnp.where(col_valid[None, None], table, NEG)
    per_row = []
    for r in range(rows):
        r0 = _na_row_start(r, rows)
        per_row.append(jnp.concatenate([table[:, r0 + j - r + NA_ROWS - 1] for j in range(kr)], axis=-1))
    return jnp.stack(per_row, axis=1)


def _lat_na_attention(qd3, seq0, k4, v4, k_ctx, v_ctx, bias):
    n_lat, _, length, _ = k4.shape
    rows = length // GRID_W
    kr = min(NA_ROWS, rows)
    lc = k_ctx.shape[2]
    kv_spec = pl.BlockSpec((None, None, length, HEAD_DIM), lambda h, b: (b, h, 0, 0))
    ctx_spec = pl.BlockSpec((None, None, lc, HEAD_DIM), lambda h, b: (b, h, 0, 0))
    return pl.pallas_call(
        functools.partial(_lat_na_kernel, length=length),
        grid=(NA_HEADS, n_lat),
        in_specs=[pl.BlockSpec((None, length, HEAD_DIM), lambda h, b: (seq0 + b, 0, h)),
                  kv_spec, kv_spec, ctx_spec, ctx_spec,
                  pl.BlockSpec((None, rows, GRID_W, kr * GRID_W), lambda h, b: (h, 0, 0, 0))],
        out_specs=pl.BlockSpec((None, length, HEAD_DIM), lambda h, b: (b, 0, h)),
        out_shape=jax.ShapeDtypeStruct((n_lat, length, NA_WIDTH), BF16),
        compiler_params=_params(("arbitrary", "arbitrary"), 48),
        name="latent_neighborhood_attention",
    )(qd3, k4, v4, k_ctx, v_ctx, bias)


def kernel(x_prompt, x_sample, cache_l0_k, cache_l0_v, state_l0_re, state_l0_im, cache_l1_ckv, cache_l1_kpe, cache_l1_k, cache_l1_v, c, c_ctx, l0_ada_w, l0_ada_b, l0_norm1, l0_norm2, l0_w_in, l0_lambda_re, l0_lambda_im, l0_log_dt, l0_b_re, l0_b_im, l0_c_re, l0_c_im, l0_d_skip, l0_w_glu, l0_b_glu, l0_sink, l0_w_out, l0_ffn_w_up, l0_ffn_conv_w, l0_ffn_conv_b, l0_ffn_w_down, l1_ada_w, l1_ada_b, l1_norm1, l1_norm2, l1_w_in, l1_q_norm, l1_kv_norm, l1_w_uq, l1_w_ukv, l1_rpb, l1_w_out, l1_ffn_w_up, l1_ffn_conv_w, l1_ffn_conv_b, l1_ffn_w_down, final_norm):
    n_p, len_p, d = x_prompt.shape
    n_s, len_s, _ = x_sample.shape
    rows_p, rows_s = n_p * len_p, n_s * len_s
    t = rows_p + rows_s
    assert d == D_MODEL and len_s == ROW_TILE and ROW_TILE % len_p == 0 and n_s + 1 <= MAX_CONDS
    assert rows_p % ROW_TILE == 0
    assert len_p % S5_CHUNK == 0 and len_s % (2 * WIN_RADIUS) == 0 and len_s % GRID_W == 0
    seqs_p_units = t // len_p
    seqs_s_units = t // len_s
    seq0_s = rows_p // len_s
    groups = dict(n_prompt_rows=rows_p, lat_len=len_s)
    half = D_MODEL // 2

    cond = jnp.zeros((MAX_CONDS, d), F32).at[0].set(c_ctx).at[1:1 + n_s].set(c)
    mod0 = _modulation(cond, l0_ada_w, l0_ada_b)
    mod1 = _modulation(cond, l1_ada_w, l1_ada_b)

    x_parts = [x_prompt.reshape(rows_p, d), x_sample.reshape(rows_s, d)]

    def ffn(x, mod, norm2, w_up, conv_w, conv_b, w_down):
        h = _norm_mod([x], norm2, mod, 3, 4, **groups)
        act = _ffn_up(h, w_up, conv_w, conv_b, rows_p, len_p, len_s)
        return _matmul_residual([[(act, 0)]], w_down.astype(BF16), [x], mod, 5, 256, name="ffn_down",
                                single_buffer_x=True, **groups)

    h = _norm_mod(x_parts, l0_norm1, mod0, 0, 1, **groups)
    u = _matmul(h, l0_w_in, F32, 512, "l0_in_proj_u", 0, S5_WIDTH)
    q0 = _matmul(h, l0_w_in, BF16, 512, "l0_in_proj_q", S5_WIDTH, WIN_Q_WIDTH)
    kcol = S5_WIDTH + WIN_Q_WIDTH
    vcol = kcol + WIN_KV_WIDTH
    out_k0 = _matmul_heads(h, l0_w_in, kcol, WIN_KV_HEADS, 0, rows_p, len_p, "l0_in_proj_k_ctx")
    out_v0 = _matmul_heads(h, l0_w_in, vcol, WIN_KV_HEADS, 0, rows_p, len_p, "l0_in_proj_v_ctx")
    k0_s = _matmul_heads(h, l0_w_in, kcol, WIN_KV_HEADS, rows_p, rows_s, len_s, "l0_in_proj_k_lat")
    v0_s = _matmul_heads(h, l0_w_in, vcol, WIN_KV_HEADS, rows_p, rows_s, len_s, "l0_in_proj_v_lat")

    assert rows_p % rows_s == 0
    sel = _s5_lane_permutation()
    sb, t_op, e_op, a16 = _s5_operators(l0_lambda_re, l0_lambda_im, l0_log_dt, l0_b_re, l0_b_im, l0_c_re, l0_c_im, l0_d_skip)
    nblk, gps, pw = S5_WIDTH // LANE, GROUPS_PER_STEP, 2 * S5_STATE

    def planes(cols):
        b = cols.shape[1]
        c5 = cols.reshape(nblk, gps, b, 2, pw).transpose(0, 2, 3, 1, 4)
        return c5.reshape(nblk, b, 2 * STATE_W)

    def state_cols(s):
        return s.astype(F32).transpose(2, 0, 1, 3).reshape(S5_GROUPS, s.shape[0], pw)

    np8, ns8 = -(-n_p // SUBLANE) * SUBLANE, -(-n_s // SUBLANE) * SUBLANE
    a_planes = planes(a16)
    h0_lat = planes(jnp.concatenate([state_cols(state_l0_re), state_cols(state_l0_im)], axis=-1))
    h0_s = jnp.pad(h0_lat, ((0, 0), (0, ns8 - n_s), (0, 0)))
    h0_p = jnp.zeros((nblk, np8, 2 * STATE_W), F32)
    y_p, hfin = _s5_stream(u, 0, n_p, len_p, sel, sb, t_op, e_op, a_planes, h0_p)
    y_s, _ = _s5_stream(u, rows_p // rows_s, n_s, len_s, sel, sb, t_op, e_op, a_planes, h0_s)
    a_out = _glu([y_p, y_s], l0_w_glu, l0_b_glu, rows_p)

    def state_out(plane):
        return (plane[:, :n_p].reshape(nblk, n_p, gps, 2, S5_STATE).transpose(1, 3, 0, 2, 4)
                .reshape(n_p, 2, S5_GROUPS, S5_STATE))

    out_sre = state_out(hfin[:, :, :STATE_W])
    out_sim = state_out(hfin[:, :, STATE_W:])

    cos_e, sin_e = _rope_tables(len_s, HEAD_DIM)
    o_p = _ctx_gqa_attention(q0.reshape(seqs_p_units, len_p, -1), out_k0, out_v0, l0_sink, n_p, len_p)
    o_s = _lat_window_attention(q0.reshape(seqs_s_units, len_s, -1), seq0_s, k0_s, v0_s,
                                cache_l0_k, cache_l0_v, l0_sink, cos_e, sin_e)
    x = _matmul_residual([[(a_out, 0)], [(o_p.reshape(rows_p, half), 0), (o_s.reshape(rows_s, half), 0)]],
                         l0_w_out.astype(BF16), x_parts, mod0, 2, 512, name="l0_out_proj", **groups)
    x = ffn(x, mod0, l0_norm2, l0_ffn_w_up, l0_ffn_conv_w, l0_ffn_conv_b, l0_ffn_w_down)

    s1, s2 = MLA_Q_LORA + MLA_KV_LORA, MLA_Q_LORA + MLA_KV_LORA + MLA_ROPE
    w_lora = l1_w_in[:, :s1].astype(BF16)
    w_dkv = l1_w_in[:, s2:].astype(BF16)
    w_kpe = jnp.pad(l1_w_in[:, s1:s2], ((0, 0), (0, LANE - MLA_ROPE))).astype(BF16)
    w_uq = l1_w_uq.reshape(MLA_Q_LORA, MLA_HEADS, MLA_NOPE + MLA_ROPE)
    w_uq = jnp.pad(w_uq, ((0, 0), (0, 0), (0, LANE - MLA_ROPE))).reshape(MLA_Q_LORA, -1).astype(BF16)

    h = _norm_mod([x], l1_norm1, mod1, 0, 1, **groups)
    cqkv = _matmul(h, w_lora, F32, 512, "l1_in_proj_lora")
    kpe = _matmul(h, w_kpe, F32, LANE, "l1_in_proj_kpe")
    qd = _matmul(h, w_dkv, BF16, 512, "l1_in_proj_qd", 0, NA_WIDTH)
    out_k1 = _matmul_heads(h, w_dkv, NA_WIDTH, NA_HEADS, 0, rows_p, len_p, "l1_in_proj_kd_ctx")
    out_v1 = _matmul_heads(h, w_dkv, 2 * NA_WIDTH, NA_HEADS, 0, rows_p, len_p, "l1_in_proj_vd_ctx")
    k1_s = _matmul_heads(h, w_dkv, NA_WIDTH, NA_HEADS, rows_p, rows_s, len_s, "l1_in_proj_kd_lat")
    v1_s = _matmul_heads(h, w_dkv, 2 * NA_WIDTH, NA_HEADS, rows_p, rows_s, len_s, "l1_in_proj_vd_lat")
    cqn = _rmsnorm_cols(cqkv, l1_q_norm, 0, MLA_Q_LORA, BF16)
    ckvn = _rmsnorm_cols(cqkv, l1_kv_norm, MLA_Q_LORA // MLA_KV_LORA, MLA_KV_LORA, F32)
    q_all = _matmul(cqn, w_uq, BF16, 2048, "l1_q_up")
    w_ukv = l1_w_ukv.astype(BF16)
    kv_all = _matmul(ckvn, w_ukv, BF16, 2048, "l1_kv_up")
    lc = cache_l1_ckv.shape[1]
    kv_ctx = _matmul(cache_l1_ckv.reshape(-1, MLA_KV_LORA), w_ukv, BF16, 2048, "l1_kv_up_ctx")

    out_ckv = ckvn[:rows_p].reshape(n_p, len_p, MLA_KV_LORA)
    out_kpe = kpe[:rows_p, :MLA_ROPE].reshape(n_p, len_p, MLA_ROPE)

    o_p = _ctx_odd_attention(q_all.reshape(seqs_p_units, len_p, -1),
                             kv_all.reshape(seqs_p_units, len_p, -1),
                             kpe.reshape(seqs_p_units, len_p, LANE),
                             qd.reshape(seqs_p_units, len_p, -1), out_k1, out_v1, n_p, len_p)
    cos_o, sin_o = _rope_tables(len_s, MLA_ROPE)
    kpe_ctx = jnp.pad(cache_l1_kpe, ((0, 0), (0, 0), (0, LANE - MLA_ROPE)))
    oc_s = _lat_mla_attention(q_all.reshape(seqs_s_units, len_s, -1),
                              kv_all.reshape(seqs_s_units, len_s, -1),
                              kv_ctx.reshape(n_s, lc, -1),
                              kpe.reshape(seqs_s_units, len_s, LANE), kpe_ctx,
                              seq0_s, n_s, len_s, cos_o, sin_o)
    od_s = _lat_na_attention(qd.reshape(seqs_s_units, len_s, -1), seq0_s, k1_s, v1_s,
                             cache_l1_k, cache_l1_v, _na_bias(l1_rpb, len_s))
    o_p2 = o_p.reshape(rows_p, 2 * half)
    x = _matmul_residual([[(o_p2, 0), (oc_s.reshape(rows_s, half), 0)],
                          [(o_p2, 1), (od_s.reshape(rows_s, half), 0)]],
                         l1_w_out.astype(BF16), [x], mod1, 2, 512, name="l1_out_proj", **groups)
    x = ffn(x, mod1, l1_norm2, l1_ffn_w_up, l1_ffn_conv_w, l1_ffn_conv_b, l1_ffn_w_down)

    y_prompt = _rmsnorm_cols(x, final_norm, 0, d, F32, 0, rows_p).reshape(n_p, len_p, d)
    y_sample = _rmsnorm_cols(x, final_norm, 0, d, F32, rows_p, rows_s).reshape(n_s, len_s, d)
    return (y_prompt, y_sample, out_k0, out_v0, out_sre, out_sim, out_ckv, out_kpe, out_k1, out_v1)
```

```python
import functools
import math

import jax
import jax.numpy as jnp
from jax import lax
from jax.experimental import pallas as pl
from jax.experimental.pallas import tpu as pltpu

F32 = jnp.float32
BF16 = jnp.bfloat16

D_MODEL = 4096
GRID_W = 64
HEAD_DIM = 128
S5_WIDTH = 2048
S5_GROUP = 16
S5_GROUPS = S5_WIDTH // S5_GROUP
S5_STATE = 64
S5_CHUNK = 16
WIN_HEADS = 16
WIN_KV_HEADS = 4
WIN_GROUP = WIN_HEADS // WIN_KV_HEADS
WIN_RADIUS = 128
WIN_Q_WIDTH = WIN_HEADS * HEAD_DIM
WIN_KV_WIDTH = WIN_KV_HEADS * HEAD_DIM
MLA_HEADS = 16
MLA_Q_LORA = 1024
MLA_KV_LORA = 512
MLA_NOPE = 128
MLA_ROPE = 64
MLA_V = 128
NA_HEADS = 16
NA_ROWS = 8
NA_COLS = 16
NA_WIDTH = NA_HEADS * HEAD_DIM
D_FF = 11008
ROPE_BASE = 10000.0
EPS = 1e-6
NEG = -1e30

LANE = 128
SUBLANE = 8
MAX_CONDS = 8
ROW_TILE = 1024


def _params(sem, vmem_mb):
    return pltpu.CompilerParams(dimension_semantics=sem, vmem_limit_bytes=vmem_mb * 1024 * 1024)


def _cond_index(i, bm, n_prompt_rows, lat_len):
    first = n_prompt_rows // bm
    per = lat_len // bm
    return jnp.where(i < first, 0, 1 + (i - first) // per)


def _dot(a, b):
    return jnp.dot(a, b, preferred_element_type=F32)


def _dot_t(a, b):
    return lax.dot_general(a, b, (((1,), (1,)), ((), ())), preferred_element_type=F32)


def _sigmoid(x):
    return 1.0 / (1.0 + jnp.exp(-x))


def _row_specs(parts, bm, width, col_fn, n_prompt_blocks, single_buffer=False):
    mode = dict(pipeline_mode=pl.Buffered(1)) if single_buffer else {}
    if len(parts) == 1:
        cb = parts[0][1]
        return [pl.BlockSpec((bm, width), lambda i, j: (i, col_fn(j, cb)), **mode)]
    cb_p, cb_s = parts[0][1], parts[1][1]
    last_p = n_prompt_blocks - 1
    return [pl.BlockSpec((bm, width), lambda i, j: (jnp.minimum(i, last_p), col_fn(j, cb_p)), **mode),
            pl.BlockSpec((bm, width), lambda i, j: (jnp.maximum(i - n_prompt_blocks, 0), col_fn(j, cb_s)), **mode)]


def _by_stream(i, n_prompt_blocks, operands, body):
    if all(len(o) == 1 for o in operands):
        body([o[0] for o in operands])
        return

    @pl.when(i < n_prompt_blocks)
    def _():
        body([o[0] for o in operands])

    @pl.when(i >= n_prompt_blocks)
    def _():
        body([o[-1] for o in operands])


def _mod_kernel(c_ref, w_ref, b_ref, o_ref):
    c = c_ref[...]
    s = c * _sigmoid(c)
    o_ref[...] = _dot(s.astype(BF16), w_ref[...].astype(BF16)) + b_ref[...]


def _modulation(cond, w, b):
    d, n = w.shape
    bn = 512
    out = pl.pallas_call(
        _mod_kernel,
        grid=(n // bn,),
        in_specs=[pl.BlockSpec((MAX_CONDS, d), lambda j: (0, 0)),
                  pl.BlockSpec((d, bn), lambda j: (0, j)),
                  pl.BlockSpec((1, bn), lambda j: (0, j))],
        out_specs=pl.BlockSpec((MAX_CONDS, bn), lambda j: (0, j)),
        out_shape=jax.ShapeDtypeStruct((MAX_CONDS, n), F32),
        compiler_params=_params(("arbitrary",), 40),
        name="modulation",
    )(cond, w, b.reshape(1, n))
    return out.reshape(MAX_CONDS, 1, n)


def _norm_mod_kernel(*refs, n_x, n_prompt_blocks):
    x_refs, (g_ref, sh_ref, sc_ref, o_ref) = refs[:n_x], refs[n_x:]

    def body(r):
        x = r[0][...]
        ms = jnp.mean(x * x, axis=-1, keepdims=True)
        y = x * lax.rsqrt(ms + EPS) * g_ref[...]
        o_ref[...] = (y * (1.0 + sc_ref[...]) + sh_ref[...]).astype(o_ref.dtype)

    _by_stream(pl.program_id(0), n_prompt_blocks, [x_refs], body)


def _norm_mod(x_parts, gain, mod, shift_slot, scale_slot, n_prompt_rows, lat_len):
    d = D_MODEL
    bm = 256
    t = sum(x.shape[0] for x in x_parts)
    npb = n_prompt_rows // bm
    cidx = functools.partial(_cond_index, bm=bm, n_prompt_rows=n_prompt_rows, lat_len=lat_len)
    specs = _row_specs([(x, 0) for x in x_parts], bm, d, lambda j, cb: 0, npb)
    to1d = lambda spec: pl.BlockSpec(spec.block_shape, lambda i, f=spec.index_map: f(i, 0))
    return pl.pallas_call(
        functools.partial(_norm_mod_kernel, n_x=len(x_parts), n_prompt_blocks=npb),
        grid=(t // bm,),
        in_specs=[to1d(s) for s in specs] + [
            pl.BlockSpec((1, d), lambda i: (0, 0)),
            pl.BlockSpec((None, 1, d), lambda i: (cidx(i), 0, shift_slot)),
            pl.BlockSpec((None, 1, d), lambda i: (cidx(i), 0, scale_slot))],
        out_specs=pl.BlockSpec((bm, d), lambda i: (i, 0)),
        out_shape=jax.ShapeDtypeStruct((t, d), BF16),
        compiler_params=_params(("arbitrary",), 40),
        name="norm_mod",
    )(*x_parts, gain.reshape(1, d), mod, mod)


def _rmsnorm_kernel(x_ref, g_ref, o_ref):
    x = x_ref[...].astype(F32)
    ms = jnp.mean(x * x, axis=-1, keepdims=True)
    o_ref[...] = (x * lax.rsqrt(ms + EPS) * g_ref[...]).astype(o_ref.dtype)


def _rmsnorm_cols(x, gain, col_block, width, out_dtype, row0=0, rows=None):
    bm = 256
    rows = x.shape[0] if rows is None else rows
    rb0 = row0 // bm
    return pl.pallas_call(
        _rmsnorm_kernel,
        grid=(rows // bm,),
        in_specs=[pl.BlockSpec((bm, width), lambda i: (i + rb0, col_block)),
                  pl.BlockSpec((1, width), lambda i: (0, 0))],
        out_specs=pl.BlockSpec((bm, width), lambda i: (i, 0)),
        out_shape=jax.ShapeDtypeStruct((rows, width), out_dtype),
        compiler_params=_params(("arbitrary",), 40),
        name="rmsnorm",
    )(x, gain.reshape(1, width))


def _mm_kernel(x_ref, w_ref, o_ref):
    o_ref[...] = _dot(x_ref[...].astype(BF16), w_ref[...].astype(BF16)).astype(o_ref.dtype)


def _matmul(x, w, out_dtype, bn, name, col0=0, ncols=None):
    m, k = x.shape
    ncols = w.shape[1] - col0 if ncols is None else ncols
    bm = min(ROW_TILE, m)
    assert m % bm == 0 and ncols % bn == 0 and col0 % bn == 0
    cb0 = col0 // bn
    return pl.pallas_call(
        _mm_kernel,
        grid=(m // bm, ncols // bn),
        in_specs=[pl.BlockSpec((bm, k), lambda i, j: (i, 0)),
                  pl.BlockSpec((k, bn), lambda i, j: (0, cb0 + j))],
        out_specs=pl.BlockSpec((bm, bn), lambda i, j: (i, j)),
        out_shape=jax.ShapeDtypeStruct((m, ncols), out_dtype),
        compiler_params=_params(("arbitrary", "arbitrary"), 56),
        name=name,
    )(x, w)


def _mm_heads_kernel(x_ref, w_ref, o_ref, *, seqs, seq_len, heads):
    acc = _dot(x_ref[...], w_ref[...].astype(BF16))
    for b in range(seqs):
        for hh in range(heads):
            o_ref[b, hh] = acc[b * seq_len:(b + 1) * seq_len, hh * HEAD_DIM:(hh + 1) * HEAD_DIM]


def _matmul_heads(x, w, col0, heads, row0, rows, seq_len, name):
    k = x.shape[1]
    bm = ROW_TILE
    hb = 4
    bn = hb * HEAD_DIM
    assert rows % bm == 0 and row0 % bm == 0 and bm % seq_len == 0 and heads % hb == 0 and col0 % bn == 0
    seqs = bm // seq_len
    rb0, cb0 = row0 // bm, col0 // bn
    return pl.pallas_call(
        functools.partial(_mm_heads_kernel, seqs=seqs, seq_len=seq_len, heads=hb),
        grid=(rows // bm, heads // hb),
        in_specs=[pl.BlockSpec((bm, k), lambda i, j: (rb0 + i, 0)),
                  pl.BlockSpec((k, bn), lambda i, j: (0, cb0 + j))],
        out_specs=pl.BlockSpec((seqs, hb, seq_len, HEAD_DIM), lambda i, j: (i, j, 0, 0)),
        out_shape=jax.ShapeDtypeStruct((rows // seq_len, heads, seq_len, HEAD_DIM), F32),
        compiler_params=_params(("arbitrary", "arbitrary"), 56),
        name=name,
    )(x, w)


def _mm_res_kernel(*refs, n_lhs, n_parts, n_res, n_prompt_blocks):
    pos = 0
    lhs = []
    for n in n_lhs:
        lhs.append(refs[pos:pos + n])
        pos += n
    w_refs = refs[pos:pos + n_parts]
    pos += n_parts
    res = refs[pos:pos + n_res]
    pos += n_res
    gate_ref, o_ref = refs[pos], refs[pos + 1]

    def body(r):
        acc = None
        for x_ref, w_ref in zip(r[:n_parts], w_refs):
            part = _dot(x_ref[...], w_ref[...].astype(BF16))
            acc = part if acc is None else acc + part
        o_ref[...] = r[n_parts][...] + gate_ref[...] * acc

    _by_stream(pl.program_id(0), n_prompt_blocks, lhs + [res], body)


def _matmul_residual(lhs_parts, w, res_parts, mod, gate_slot, bn, n_prompt_rows, lat_len, name,
                     single_buffer_x=False, vmem_mb=56, bm=ROW_TILE):
    n = w.shape[1]
    kq = w.shape[0] // len(lhs_parts)
    t = sum(r.shape[0] for r in res_parts)
    npb = n_prompt_rows // bm
    cidx = functools.partial(_cond_index, bm=bm, n_prompt_rows=n_prompt_rows, lat_len=lat_len)
    gate_col0 = gate_slot * (D_MODEL // bn)
    in_specs, args = [], []
    for parts in lhs_parts:
        in_specs += _row_specs(parts, bm, kq, lambda j, cb: cb, npb, single_buffer=single_buffer_x)
        args += [a for a, _ in parts]
    for q in range(len(lhs_parts)):
        in_specs.append(pl.BlockSpec((kq, bn), lambda i, j, q=q: (q, j)))
        args.append(w)
    in_specs += _row_specs([(r, 0) for r in res_parts], bm, bn, lambda j, cb: j, npb)
    args += list(res_parts)
    in_specs.append(pl.BlockSpec((None, 1, bn), lambda i, j: (cidx(i), 0, gate_col0 + j)))
    args.append(mod)
    body = functools.partial(_mm_res_kernel, n_lhs=tuple(len(p) for p in lhs_parts), n_parts=len(lhs_parts),
                             n_res=len(res_parts), n_prompt_blocks=npb)
    return pl.pallas_call(
        body,
        grid=(t // bm, n // bn),
        in_specs=in_specs,
        out_specs=pl.BlockSpec((bm, bn), lambda i, j: (i, j)),
        out_shape=jax.ShapeDtypeStruct((t, n), F32),
        compiler_params=_params(("arbitrary", "arbitrary"), vmem_mb),
        name=name,
    )(*args)


def _glu_kernel(*refs, n_y, n_prompt_blocks):
    y_refs = refs[:n_y]
    w_ref, b_ref, o_ref = refs[n_y:]

    def body(r):
        y = r[0][...]
        z = _dot(y.astype(BF16), w_ref[...]) + b_ref[...]
        o_ref[...] = (y * _sigmoid(z)).astype(o_ref.dtype)

    _by_stream(pl.program_id(0), n_prompt_blocks, [y_refs], body)


def _glu(y_parts, w, b, n_prompt_rows):
    k, n = w.shape
    bm = ROW_TILE
    m = sum(y.shape[0] for y in y_parts)
    npb = n_prompt_rows // bm
    parts = [(y, 0) for y in y_parts]
    return pl.pallas_call(
        functools.partial(_glu_kernel, n_y=len(y_parts), n_prompt_blocks=npb),
        grid=(m // bm, 1),
        in_specs=(_row_specs(parts, bm, k, lambda j, cb: 0, npb)
                  + [pl.BlockSpec((k, n), lambda i, j: (0, 0), pipeline_mode=pl.Buffered(1)),
                     pl.BlockSpec((1, n), lambda i, j: (0, 0))]),
        out_specs=pl.BlockSpec((bm, n), lambda i, j: (i, 0)),
        out_shape=jax.ShapeDtypeStruct((m, n), BF16),
        compiler_params=_params(("arbitrary", "arbitrary"), 56),
        name="s5_glu",
    )(*y_parts, w.astype(BF16), b.reshape(1, n))


def _ffn_up_kernel(x_ref, wg_ref, wv_ref, cwg_ref, cwv_ref, cbg_ref, cbv_ref, o_ref, *,
                   sub, n_sub, prompt_blocks, prompt_len, lat_len):
    i = pl.program_id(0)
    row = lax.broadcasted_iota(jnp.int32, (sub, 1), 0)
    wg = wg_ref[...].astype(BF16)
    wv = wv_ref[...].astype(BF16)

    for s in range(n_sub):
        period = jnp.where(i * n_sub + s < prompt_blocks, prompt_len, lat_len)
        pos = row & (period - 1)
        first = pos == 0
        last = pos == period - 1

        def conv(u, cw_ref, cb_ref):
            prev = jnp.where(first, 0.0, pltpu.roll(u, 1, 0))
            nxt = jnp.where(last, 0.0, pltpu.roll(u, sub - 1, 0))
            return prev * cw_ref[0:1, :] + u * cw_ref[1:2, :] + nxt * cw_ref[2:3, :] + cb_ref[...]

        x = x_ref[s * sub:(s + 1) * sub, :]
        g = conv(_dot(x, wg), cwg_ref, cbg_ref)
        v = conv(_dot(x, wv), cwv_ref, cbv_ref)
        o_ref[s * sub:(s + 1) * sub, :] = (g * _sigmoid(g) * v).astype(o_ref.dtype)


FFN_SUB_BLOCKS = 4


def _ffn_up(h, w_up, conv_w, conv_b, n_prompt_rows, prompt_len, lat_len):
    t, d = h.shape
    sub, bn = lat_len, 256
    n_sub = FFN_SUB_BLOCKS if (t // sub) % FFN_SUB_BLOCKS == 0 else 2
    bm = sub * n_sub
    assert t % bm == 0 and n_prompt_rows % sub == 0 and sub % prompt_len == 0
    assert prompt_len & (prompt_len - 1) == 0 and lat_len & (lat_len - 1) == 0
    nj = D_FF // bn
    conv_b = conv_b.reshape(1, 2 * D_FF)
    body = functools.partial(_ffn_up_kernel, sub=sub, n_sub=n_sub, prompt_blocks=n_prompt_rows // sub,
                             prompt_len=prompt_len, lat_len=lat_len)
    return pl.pallas_call(
        body,
        grid=(t // bm, nj),
        in_specs=[pl.BlockSpec((bm, d), lambda i, j: (i, 0), pipeline_mode=pl.Buffered(1)),
                  pl.BlockSpec((d, bn), lambda i, j: (0, j)),
                  pl.BlockSpec((d, bn), lambda i, j: (0, nj + j)),
                  pl.BlockSpec((3, bn), lambda i, j: (0, j)),
                  pl.BlockSpec((3, bn), lambda i, j: (0, nj + j)),
                  pl.BlockSpec((1, bn), lambda i, j: (0, j)),
                  pl.BlockSpec((1, bn), lambda i, j: (0, nj + j))],
        out_specs=pl.BlockSpec((bm, bn), lambda i, j: (i, j)),
        out_shape=jax.ShapeDtypeStruct((t, D_FF), BF16),
        compiler_params=_params(("arbitrary", "arbitrary"), 60),
        name="ffn_up_conv",
    )(h, w_up, w_up, conv_w, conv_w, conv_b, conv_b)


def _s5_operators(lam_re, lam_im, log_dt, b_re, b_im, c_re, c_im, d_skip):
    q, g, p, c = S5_CHUNK, S5_GROUPS, S5_STATE, S5_GROUP
    dt = jnp.exp(log_dt)[None, :, :, None]
    tau = jnp.arange(q + 1, dtype=F32)[:, None, None, None]
    mag = jnp.exp(lam_re[None] * dt * tau)
    ang = lam_im[None] * dt * tau
    pw_re, pw_im = mag * jnp.cos(ang), mag * jnp.sin(ang)
    dt1 = jnp.exp(log_dt)[:, :, None]
    m1 = jnp.exp(lam_re * dt1)
    ab_re, ab_im = m1 * jnp.cos(lam_im * dt1), m1 * jnp.sin(lam_im * dt1)
    den = lam_re * lam_re + lam_im * lam_im
    nr, ni = ab_re - 1.0, ab_im
    f_re = ((nr * lam_re + ni * lam_im) / den)[..., None]
    f_im = ((ni * lam_re - nr * lam_im) / den)[..., None]
    bb_re = f_re * b_re - f_im * b_im
    bb_im = f_re * b_im + f_im * b_re
    ct_re, ct_im = jnp.swapaxes(c_re, -1, -2), jnp.swapaxes(c_im, -1, -2)

    def power_times_c(fwd_taus, bwd_taus):
        pr = jnp.stack([pw_re[fwd_taus, 0], pw_re[bwd_taus, 1]], axis=0).transpose(0, 2, 3, 1)
        pi = jnp.stack([pw_im[fwd_taus, 0], pw_im[bwd_taus, 1]], axis=0).transpose(0, 2, 3, 1)
        r_re = pr[..., None] * ct_re[:, :, :, None, :] - pi[..., None] * ct_im[:, :, :, None, :]
        r_im = pr[..., None] * ct_im[:, :, :, None, :] + pi[..., None] * ct_re[:, :, :, None, :]
        return r_re.reshape(2, g, p, q * c), r_im.reshape(2, g, p, q * c)

    lags = jnp.arange(q)
    rk_re, rk_im = power_times_c(lags, lags[::-1])
    diag = jnp.tile(d_skip[:, :, None] * jnp.eye(c, dtype=F32)[None], (1, 1, q))
    t_op = _s5_toeplitz(rk_re, rk_im, bb_re, bb_im, diag)
    re_re, re_im = power_times_c(lags + 1, q - lags)
    zero = jnp.zeros_like(re_re[0])
    e_op = jnp.concatenate([re_re[0], zero, -re_im[0], zero, zero, re_re[1], zero, -re_im[1]], axis=1)
    def s_op(taus, d):
        pr, pi = pw_re[taus, d][:, :, None, :], pw_im[taus, d][:, :, None, :]
        br, bi = jnp.swapaxes(bb_re[d], -1, -2)[None], jnp.swapaxes(bb_im[d], -1, -2)[None]
        s_re = (pr * br - pi * bi).transpose(1, 0, 2, 3).reshape(g, q * c, p)
        s_im = (pr * bi + pi * br).transpose(1, 0, 2, 3).reshape(g, q * c, p)
        return s_re, s_im

    sf_re, sf_im = s_op(lags[::-1], 0)
    sb_re, sb_im = s_op(lags, 1)
    sb = jnp.concatenate([sf_re, sb_re, sf_im, sb_im], axis=2)
    a16 = jnp.concatenate([pw_re[q, 0], pw_re[q, 1], pw_im[q, 0], pw_im[q, 1]], axis=-1)[:, None, :]
    return sb.astype(BF16), t_op, e_op.astype(BF16), a16


def _s5_toeplitz_kernel(rr_ref, ri_ref, br_ref, bi_ref, dg_ref, o_ref, k_scr):
    c, q = S5_GROUP, S5_CHUNK
    for d in range(2):
        for ci in range(c):
            prod = br_ref[d, :, ci:ci + 1] * rr_ref[d] - bi_ref[d, :, ci:ci + 1] * ri_ref[d]
            k_scr[d * c + ci:d * c + ci + 1, :] = jnp.sum(prod, axis=0, keepdims=True)
    kf = k_scr[0:c, :]
    kb = k_scr[c:2 * c, :]
    lane = lax.broadcasted_iota(jnp.int32, (c, q * c), 1)
    dg = dg_ref[...]
    for s in range(q):
        f = kf if s == 0 else pltpu.roll(kf, c * s, 1)
        b = kb if s == q - 1 else pltpu.roll(kb, q * c - c * (q - 1 - s), 1)
        piece = jnp.where(lane >= c * s, f, 0.0) + jnp.where(lane < c * (s + 1), b, 0.0)
        piece = piece + jnp.where(jnp.logical_and(lane >= c * s, lane < c * (s + 1)), dg, 0.0)
        o_ref[s * c:(s + 1) * c, :] = piece.astype(o_ref.dtype)


def _s5_toeplitz(rk_re, rk_im, bb_re, bb_im, diag):
    g, p, c, qc = S5_GROUPS, S5_STATE, S5_GROUP, S5_CHUNK * S5_GROUP
    r_spec = pl.BlockSpec((2, None, p, qc), lambda i: (0, i, 0, 0))
    b_spec = pl.BlockSpec((2, None, p, c), lambda i: (0, i, 0, 0))
    return pl.pallas_call(
        _s5_toeplitz_kernel,
        grid=(g,),
        in_specs=[r_spec, r_spec, b_spec, b_spec, pl.BlockSpec((None, c, qc), lambda i: (i, 0, 0))],
        out_specs=pl.BlockSpec((None, qc, qc), lambda i: (i, 0, 0)),
        out_shape=jax.ShapeDtypeStruct((g, qc, qc), BF16),
        scratch_shapes=[pltpu.VMEM((2 * c, qc), F32)],
        compiler_params=_params(("arbitrary",), 32),
        name="s5_toeplitz",
    )(rk_re, rk_im, bb_re, bb_im, diag)


def _gelu_tanh(x):
    return 0.5 * x * (1.0 + jnp.tanh(math.sqrt(2.0 / math.pi) * (x + 0.044715 * (x * x * x))))


GROUPS_PER_STEP = LANE // S5_GROUP
STATE_W = GROUPS_PER_STEP * 2 * S5_STATE


def _s5_lane_permutation():
    j = jnp.arange(S5_CHUNK * LANE)
    s, gl, c = j // LANE, (j % LANE) // S5_GROUP, j % S5_GROUP
    k = gl * (S5_CHUNK * S5_GROUP) + s * S5_GROUP + c
    return (k[:, None] == jnp.arange(S5_CHUNK * LANE)[None, :]).astype(BF16)


def _s5_kernel(u_ref, sel_ref, sb_ref, t_ref, e_ref, a_ref, h0_ref, y_ref, hfin_ref, xg, w_scr, h_scr, *,
               batch, nb, nc, seq_len):
    qc = S5_CHUNK * S5_GROUP
    if batch < nb:
        xg[...] = jnp.zeros(xg.shape, BF16)
    for ch in range(nc):
        for s in range(S5_CHUNK):
            xg[ch * nb:ch * nb + batch, s * LANE:(s + 1) * LANE] = (
                u_ref[pl.ds(ch * S5_CHUNK + s, batch, stride=seq_len), :].astype(BF16))
    sel = sel_ref[...]
    ucat = _dot(xg[...], sel).astype(BF16)
    for gl in range(GROUPS_PER_STEP):
        w = _dot(ucat[:, gl * qc:(gl + 1) * qc], sb_ref[gl])
        w_scr[:, gl * LANE:(gl + 1) * LANE] = w[:, 0:LANE]
        w_scr[:, STATE_W + gl * LANE:STATE_W + (gl + 1) * LANE] = w[:, LANE:2 * LANE]
    ar = a_ref[0:1, 0:STATE_W]
    ai = a_ref[0:1, STATE_W:2 * STATE_W]
    fwd_lane = (lax.broadcasted_iota(jnp.int32, (1, STATE_W), 1) & S5_STATE) == 0
    hr = h0_ref[:, 0:STATE_W]
    hi = h0_ref[:, STATE_W:2 * STATE_W]
    for i in range(nc):
        ri = i * nb
        rj = (nc - 1 - i) * nb
        h_scr[ri:ri + nb, 0:STATE_W] = hr
        h_scr[ri:ri + nb, STATE_W:2 * STATE_W] = hi
        h_scr[rj:rj + nb, 2 * STATE_W:3 * STATE_W] = hr
        h_scr[rj:rj + nb, 3 * STATE_W:4 * STATE_W] = hi
        wr = jnp.where(fwd_lane, w_scr[ri:ri + nb, 0:STATE_W], w_scr[rj:rj + nb, 0:STATE_W])
        wi = jnp.where(fwd_lane, w_scr[ri:ri + nb, STATE_W:2 * STATE_W], w_scr[rj:rj + nb, STATE_W:2 * STATE_W])
        hr, hi = ar * hr - ai * hi + wr, ar * hi + ai * hr + wi
    hfin_ref[:, 0:STATE_W] = hr
    hfin_ref[:, STATE_W:2 * STATE_W] = hi
    for gl in range(GROUPS_PER_STEP):
        hcat = jnp.concatenate([h_scr[:, k * STATE_W + gl * LANE:k * STATE_W + (gl + 1) * LANE] for k in range(4)],
                               axis=1).astype(BF16)
        y = _dot(ucat[:, gl * qc:(gl + 1) * qc], t_ref[gl]) + _dot(hcat, e_ref[gl])
        xg[:, gl * qc:(gl + 1) * qc] = _gelu_tanh(y).astype(BF16)
    yp = _dot_t(xg[...], sel)
    for ch in range(nc):
        for s in range(S5_CHUNK):
            y_ref[pl.ds(ch * S5_CHUNK + s, batch, stride=seq_len), :] = (
                yp[ch * nb:ch * nb + batch, s * LANE:(s + 1) * LANE])


def _s5_stream(u, row_block, batch, seq_len, sel, sb, t_op, e_op, a_planes, h0):
    n_rows = batch * seq_len
    nb = -(-batch // SUBLANE) * SUBLANE
    nc = seq_len // S5_CHUNK
    rows = nb * nc
    nblk = S5_WIDTH // LANE
    qc = S5_CHUNK * S5_GROUP
    body = functools.partial(_s5_kernel, batch=batch, nb=nb, nc=nc, seq_len=seq_len)
    return pl.pallas_call(
        body,
        grid=(nblk,),
        in_specs=[pl.BlockSpec((n_rows, LANE), lambda g: (row_block, g)),
                  pl.BlockSpec((S5_CHUNK * LANE, S5_CHUNK * LANE), lambda g: (0, 0), pipeline_mode=pl.Buffered(1)),
                  pl.BlockSpec((GROUPS_PER_STEP, qc, 4 * S5_STATE), lambda g: (g, 0, 0)),
                  pl.BlockSpec((GROUPS_PER_STEP, qc, qc), lambda g: (g, 0, 0)),
                  pl.BlockSpec((GROUPS_PER_STEP, 8 * S5_STATE, qc), lambda g: (g, 0, 0)),
                  pl.BlockSpec((None, 1, 2 * STATE_W), lambda g: (g, 0, 0)),
                  pl.BlockSpec((None, nb, 2 * STATE_W), lambda g: (g, 0, 0))],
        out_specs=[pl.BlockSpec((n_rows, LANE), lambda g: (0, g)),
                   pl.BlockSpec((None, nb, 2 * STATE_W), lambda g: (g, 0, 0))],
        out_shape=[jax.ShapeDtypeStruct((n_rows, S5_WIDTH), F32),
                   jax.ShapeDtypeStruct((nblk, nb, 2 * STATE_W), F32)],
        scratch_shapes=[pltpu.VMEM((rows, S5_CHUNK * LANE), BF16),
                        pltpu.VMEM((rows, 2 * STATE_W), F32),
                        pltpu.VMEM((rows, 4 * STATE_W), F32)],
        compiler_params=_params(("arbitrary",), 56),
        name="s5_chunked",
    )(u, sel, sb, t_op, e_op, a_planes, h0)


def _rope_tables(length, rot):
    n_freq = rot // 4
    t = jnp.arange(length)
    row = (t // GRID_W).astype(F32)
    col = (t % GRID_W).astype(F32)
    inv = ROPE_BASE ** (-jnp.arange(n_freq, dtype=F32) / n_freq)
    ar, ac = row[:, None] * inv, col[:, None] * inv
    cos = jnp.concatenate([jnp.cos(ar), jnp.cos(ar), jnp.cos(ac), jnp.cos(ac)], axis=-1)
    sin = jnp.concatenate([-jnp.sin(ar), jnp.sin(ar), -jnp.sin(ac), jnp.sin(ac)], axis=-1)
    pad = LANE - rot
    if pad:
        cos = jnp.concatenate([cos, jnp.ones((length, pad), F32)], axis=-1)
        sin = jnp.concatenate([sin, jnp.zeros((length, pad), F32)], axis=-1)
    return cos, sin


def _rope(x, cos, sin, blk):
    lane = lax.broadcasted_iota(jnp.int32, (1, LANE), 1)
    lower = (lane & blk) == 0
    partner = jnp.where(lower, pltpu.roll(x, LANE - blk, 1), pltpu.roll(x, blk, 1))
    return x * cos + partner * sin


def _softmax_pv(scores, values, extra_logit=None):
    m = scores[0].max(axis=-1, keepdims=True)
    for s in scores[1:]:
        m = jnp.maximum(m, s.max(axis=-1, keepdims=True))
    if extra_logit is not None:
        m = jnp.maximum(m, extra_logit)
    den = None
    out = None
    for s, v in zip(scores, values):
        p = jnp.exp(s - m)
        d = p.sum(axis=-1, keepdims=True)
        o = _dot(p.astype(BF16), v)
        den = d if den is None else den + d
        out = o if out is None else out + o
    if extra_logit is not None:
        den = den + jnp.exp(extra_logit - m)
    return out / den


def _ctx_gqa_kernel(sink_ref, q_ref, k_ref, v_ref, o_ref):
    g = pl.program_id(1)
    scale = HEAD_DIM ** -0.5
    k = k_ref[...].astype(BF16)
    v = v_ref[...].astype(BF16)
    for r in range(WIN_GROUP):
        q = q_ref[:, r * HEAD_DIM:(r + 1) * HEAD_DIM]
        s = _dot_t(q, k) * scale
        o = _softmax_pv([s], [v], sink_ref[g * WIN_GROUP + r])
        o_ref[:, r * HEAD_DIM:(r + 1) * HEAD_DIM] = o.astype(o_ref.dtype)


def _ctx_gqa_attention(uq3, k4, v4, sink, n_prompt, length):
    qw = WIN_GROUP * HEAD_DIM
    q0 = 0
    kv_spec = pl.BlockSpec((None, None, length, HEAD_DIM), lambda b, g: (b, g, 0, 0))
    return pl.pallas_call(
        _ctx_gqa_kernel,
        grid=(n_prompt, WIN_KV_HEADS),
        in_specs=[pl.BlockSpec(memory_space=pltpu.SMEM),
                  pl.BlockSpec((None, length, qw), lambda b, g: (b, 0, q0 + g)),
                  kv_spec, kv_spec],
        out_specs=pl.BlockSpec((None, length, qw), lambda b, g: (b, 0, g)),
        out_shape=jax.ShapeDtypeStruct((n_prompt, length, WIN_Q_WIDTH), BF16),
        compiler_params=_params(("arbitrary", "arbitrary"), 40),
        name="ctx_gqa_attention",
    )(sink, uq3, k4, v4)


def _lat_window_kernel(sink_ref, q_ref, k_ref, v_ref, kc_ref, vc_ref, cos_ref, sin_ref, o_ref, *, length):
    g = pl.program_id(1)
    scale = HEAD_DIM ** -0.5
    blk = WIN_RADIUS
    nb = length // blk
    k = _rope(k_ref[...], cos_ref[...], sin_ref[...], HEAD_DIM // 4).astype(BF16)
    v = v_ref[...].astype(BF16)
    kc = kc_ref[...].astype(BF16)
    vc = vc_ref[...].astype(BF16)
    rows = WIN_GROUP * blk
    row = lax.broadcasted_iota(jnp.int32, (rows, 1), 0)
    sk = jnp.zeros((rows, 1), F32)
    for r in range(WIN_GROUP):
        sk = jnp.where(jnp.logical_and(row >= r * blk, row < (r + 1) * blk), sink_ref[g * WIN_GROUP + r], sk)
    qoff = row & (blk - 1)
    for n in range(nb):
        lo = max(0, n - 1) * blk
        hi = min(nb, n + 2) * blk
        cos = cos_ref[n * blk:(n + 1) * blk, :]
        sin = sin_ref[n * blk:(n + 1) * blk, :]
        q = jnp.concatenate(
            [_rope(q_ref[n * blk:(n + 1) * blk, r * HEAD_DIM:(r + 1) * HEAD_DIM].astype(F32), cos, sin, HEAD_DIM // 4)
             for r in range(WIN_GROUP)], axis=0).astype(BF16)
        dist = (n * blk + qoff) - (lo + lax.broadcasted_iota(jnp.int32, (1, hi - lo), 1))
        visible = jnp.logical_and(dist <= WIN_RADIUS, dist >= -WIN_RADIUS)
        s_loc = jnp.where(visible, _dot_t(q, k[lo:hi]) * scale, NEG)
        s_ctx = _dot_t(q, kc) * scale
        o = _softmax_pv([s_loc, s_ctx], [v[lo:hi], vc], sk)
        for r in range(WIN_GROUP):
            o_ref[n * blk:(n + 1) * blk, r * HEAD_DIM:(r + 1) * HEAD_DIM] = o[r * blk:(r + 1) * blk].astype(o_ref.dtype)


def _lat_window_attention(uq3, seq0, k4, v4, k_ctx, v_ctx, sink, cos, sin):
    n_lat, _, length, _ = k4.shape
    qw = WIN_GROUP * HEAD_DIM
    q0 = 0
    lc = k_ctx.shape[2]
    kv_spec = pl.BlockSpec((None, None, length, HEAD_DIM), lambda b, g: (b, g, 0, 0))
    ctx_spec = pl.BlockSpec((None, None, lc, HEAD_DIM), lambda b, g: (b, g, 0, 0))
    tab_spec = pl.BlockSpec((length, LANE), lambda b, g: (0, 0))
    return pl.pallas_call(
        functools.partial(_lat_window_kernel, length=length),
        grid=(n_lat, WIN_KV_HEADS),
        in_specs=[pl.BlockSpec(memory_space=pltpu.SMEM),
                  pl.BlockSpec((None, length, qw), lambda b, g: (seq0 + b, 0, q0 + g)),
                  kv_spec, kv_spec, ctx_spec, ctx_spec, tab_spec, tab_spec],
        out_specs=pl.BlockSpec((None, length, qw), lambda b, g: (b, 0, g)),
        out_shape=jax.ShapeDtypeStruct((n_lat, length, WIN_Q_WIDTH), BF16),
        compiler_params=_params(("arbitrary", "arbitrary"), 48),
        name="latent_window_attention",
    )(sink, uq3, k4, v4, k_ctx, v_ctx, cos, sin)


def _ctx_odd_kernel(q_ref, kv_ref, kpe_ref, qd_ref, kd_ref, vd_ref, o_ref):
    kw = MLA_NOPE + LANE
    kpe = kpe_ref[...].astype(BF16)
    scale_c = (MLA_NOPE + MLA_ROPE) ** -0.5
    for h in range(MLA_HEADS):
        q = q_ref[:, h * kw:(h + 1) * kw]
        k = jnp.concatenate([kv_ref[:, h * kw:h * kw + MLA_NOPE], kpe], axis=1)
        v = kv_ref[:, h * kw + MLA_NOPE:(h + 1) * kw]
        o = _softmax_pv([_dot_t(q, k) * scale_c], [v])
        o_ref[:, h * MLA_V:(h + 1) * MLA_V] = o.astype(o_ref.dtype)
    scale_d = HEAD_DIM ** -0.5
    base = MLA_HEADS * MLA_V
    for h in range(NA_HEADS):
        q = qd_ref[:, h * HEAD_DIM:(h + 1) * HEAD_DIM]
        o = _softmax_pv([_dot_t(q, kd_ref[h].astype(BF16)) * scale_d], [vd_ref[h].astype(BF16)])
        o_ref[:, base + h * HEAD_DIM:base + (h + 1) * HEAD_DIM] = o.astype(o_ref.dtype)


def _ctx_odd_attention(q3, kv3, kpe3, qd3, kd4, vd4, n_prompt, length):
    qn = q3.shape[-1]
    head_spec = pl.BlockSpec((None, NA_HEADS, length, HEAD_DIM), lambda b: (b, 0, 0, 0))
    return pl.pallas_call(
        _ctx_odd_kernel,
        grid=(n_prompt,),
        in_specs=[pl.BlockSpec((None, length, qn), lambda b: (b, 0, 0)),
                  pl.BlockSpec((None, length, qn), lambda b: (b, 0, 0)),
                  pl.BlockSpec((None, length, LANE), lambda b: (b, 0, 0)),
                  pl.BlockSpec((None, length, NA_WIDTH), lambda b: (b, 0, 0)),
                  head_spec, head_spec],
        out_specs=pl.BlockSpec((None, length, MLA_HEADS * MLA_V + NA_WIDTH), lambda b: (b, 0, 0)),
        out_shape=jax.ShapeDtypeStruct((n_prompt, length, MLA_HEADS * MLA_V + NA_WIDTH), BF16),
        compiler_params=_params(("arbitrary",), 48),
        name="ctx_odd_attention",
    )(q3, kv3, kpe3, qd3, kd4, vd4)


def _lat_mla_kernel(q_ref, kv_ref, kpe_ref, kvc_ref, kpec_ref, cos_ref, sin_ref, o_ref, *, length):
    scale = (MLA_NOPE + MLA_ROPE) ** -0.5
    rb = MLA_ROPE // 4
    kpe = _rope(kpe_ref[...], cos_ref[...], sin_ref[...], rb).astype(BF16)
    k_lat = jnp.concatenate([kv_ref[:, 0:MLA_NOPE], kpe], axis=1)
    v_lat = kv_ref[:, MLA_NOPE:]
    k_ctx = jnp.concatenate([kvc_ref[:, 0:MLA_NOPE], kpec_ref[...].astype(BF16)], axis=1)
    v_ctx = kvc_ref[:, MLA_NOPE:]
    qb = 256
    for n in range(length // qb):
        rows = slice(n * qb, (n + 1) * qb)
        q_pe = _rope(q_ref[rows, MLA_NOPE:].astype(F32), cos_ref[rows, :], sin_ref[rows, :], rb).astype(BF16)
        q = jnp.concatenate([q_ref[rows, 0:MLA_NOPE], q_pe], axis=1)
        o = _softmax_pv([_dot_t(q, k_lat) * scale, _dot_t(q, k_ctx) * scale], [v_lat, v_ctx])
        o_ref[rows, :] = o.astype(o_ref.dtype)


def _lat_mla_attention(q3, kv3, kvc3, kpe3, kpe_ctx, seq0, n_lat, length, cos, sin):
    kw = MLA_NOPE + LANE
    lc = kpe_ctx.shape[1]
    return pl.pallas_call(
        functools.partial(_lat_mla_kernel, length=length),
        grid=(n_lat, MLA_HEADS),
        in_specs=[pl.BlockSpec((None, length, kw), lambda b, h: (seq0 + b, 0, h)),
                  pl.BlockSpec((None, length, kw), lambda b, h: (seq0 + b, 0, h)),
                  pl.BlockSpec((None, length, LANE), lambda b, h: (seq0 + b, 0, 0)),
                  pl.BlockSpec((None, lc, kw), lambda b, h: (b, 0, h)),
                  pl.BlockSpec((None, lc, LANE), lambda b, h: (b, 0, 0)),
                  pl.BlockSpec((length, LANE), lambda b, h: (0, 0)),
                  pl.BlockSpec((length, LANE), lambda b, h: (0, 0))],
        out_specs=pl.BlockSpec((None, length, MLA_V), lambda b, h: (b, 0, h)),
        out_shape=jax.ShapeDtypeStruct((n_lat, length, MLA_HEADS * MLA_V), BF16),
        compiler_params=_params(("arbitrary", "arbitrary"), 48),
        name="latent_mla_attention",
    )(q3, kv3, kpe3, kvc3, kpe_ctx, cos, sin)


def _na_row_start(r, rows):
    kr = min(NA_ROWS, rows)
    return min(max(r - kr // 2, 0), rows - kr)


def _lat_na_kernel(q_ref, k_ref, v_ref, kc_ref, vc_ref, bias_ref, o_ref, *, length):
    scale = HEAD_DIM ** -0.5
    rows = length // GRID_W
    kr = min(NA_ROWS, rows)
    k = k_ref[...].astype(BF16)
    v = v_ref[...].astype(BF16)
    kc = kc_ref[...].astype(BF16)
    vc = vc_ref[...].astype(BF16)
    r = 0
    while r < rows:
        r_end = r + 1
        while r_end < rows and _na_row_start(r_end, rows) == _na_row_start(r, rows):
            r_end += 1
        r0 = _na_row_start(r, rows) * GRID_W
        nq = (r_end - r) * GRID_W
        q = q_ref[r * GRID_W:r_end * GRID_W, :]
        bias = bias_ref[r:r_end].reshape(nq, kr * GRID_W)
        s_nb = _dot_t(q, k[r0:r0 + kr * GRID_W]) * scale + bias
        s_ctx = _dot_t(q, kc) * scale
        o = _softmax_pv([s_nb, s_ctx], [v[r0:r0 + kr * GRID_W], vc])
        o_ref[r * GRID_W:r_end * GRID_W, :] = o.astype(o_ref.dtype)
        r = r_end


def _na_bias(rpb, length):
    rows = length // GRID_W
    kr = min(NA_ROWS, rows)
    col = jnp.arange(GRID_W)
    c_start = jnp.clip(col - NA_COLS // 2, 0, GRID_W - NA_COLS)
    col_valid = (col[None, :] >= c_start[:, None]) & (col[None, :] < c_start[:, None] + NA_COLS)
    off_c = jnp.clip(col[None, :] - col[:, None], -(NA_COLS - 1), NA_COLS - 1) + NA_COLS - 1
    onehot = (off_c[:, :, None] == jnp.arange(2 * NA_COLS - 1)[None, None, :]).astype(F32)
    table = jnp.einsum('hdj,qkj->hdqk', rpb.astype(F32), onehot, precision=lax.Precision.HIGHEST)
    table = jnp.where(col_valid[None, None], table, NEG)
    per_row = []
    for r in range(rows):
        r0 = _na_row_start(r, rows)
        per_row.append(jnp.concatenate([table[:, r0 + j - r + NA_ROWS - 1] for j in range(kr)], axis=-1))
    return jnp.stack(per_row, axis=1)


def _lat_na_attention(qd3, seq0, k4, v4, k_ctx, v_ctx, bias):
    n_lat, _, length, _ = k4.shape
    rows = length // GRID_W
    kr = min(NA_ROWS, rows)
    lc = k_ctx.shape[2]
    kv_spec = pl.BlockSpec((None, None, length, HEAD_DIM), lambda h, b: (b, h, 0, 0))
    ctx_spec = pl.BlockSpec((None, None, lc, HEAD_DIM), lambda h, b: (b, h, 0, 0))
    return pl.pallas_call(
        functools.partial(_lat_na_kernel, length=length),
        grid=(NA_HEADS, n_lat),
        in_specs=[pl.BlockSpec((None, length, HEAD_DIM), lambda h, b: (seq0 + b, 0, h)),
                  kv_spec, kv_spec, ctx_spec, ctx_spec,
                  pl.BlockSpec((None, rows, GRID_W, kr * GRID_W), lambda h, b: (h, 0, 0, 0))],
        out_specs=pl.BlockSpec((None, length, HEAD_DIM), lambda h, b: (b, 0, h)),
        out_shape=jax.ShapeDtypeStruct((n_lat, length, NA_WIDTH), BF16),
        compiler_params=_params(("arbitrary", "arbitrary"), 48),
        name="latent_neighborhood_attention",
    )(qd3, k4, v4, k_ctx, v_ctx, bias)


def kernel(x_prompt, x_sample, cache_l0_k, cache_l0_v, state_l0_re, state_l0_im, cache_l1_ckv, cache_l1_kpe, cache_l1_k, cache_l1_v, c, c_ctx, l0_ada_w, l0_ada_b, l0_norm1, l0_norm2, l0_w_in, l0_lambda_re, l0_lambda_im, l0_log_dt, l0_b_re, l0_b_im, l0_c_re, l0_c_im, l0_d_skip, l0_w_glu, l0_b_glu, l0_sink, l0_w_out, l0_ffn_w_up, l0_ffn_conv_w, l0_ffn_conv_b, l0_ffn_w_down, l1_ada_w, l1_ada_b, l1_norm1, l1_norm2, l1_w_in, l1_q_norm, l1_kv_norm, l1_w_uq, l1_w_ukv, l1_rpb, l1_w_out, l1_ffn_w_up, l1_ffn_conv_w, l1_ffn_conv_b, l1_ffn_w_down, final_norm):
    n_p, len_p, d = x_prompt.shape
    n_s, len_s, _ = x_sample.shape
    rows_p, rows_s = n_p * len_p, n_s * len_s
    t = rows_p + rows_s
    assert d == D_MODEL and len_s == ROW_TILE and ROW_TILE % len_p == 0 and n_s + 1 <= MAX_CONDS
    assert rows_p % ROW_TILE == 0
    assert len_p % S5_CHUNK == 0 and len_s % (2 * WIN_RADIUS) == 0 and len_s % GRID_W == 0
    seqs_p_units = t // len_p
    seqs_s_units = t // len_s
    seq0_s = rows_p // len_s
    groups = dict(n_prompt_rows=rows_p, lat_len=len_s)
    half = D_MODEL // 2

    cond = jnp.zeros((MAX_CONDS, d), F32).at[0].set(c_ctx).at[1:1 + n_s].set(c)
    mod0 = _modulation(cond, l0_ada_w, l0_ada_b)
    mod1 = _modulation(cond, l1_ada_w, l1_ada_b)

    x_parts = [x_prompt.reshape(rows_p, d), x_sample.reshape(rows_s, d)]

    def ffn(x, mod, norm2, w_up, conv_w, conv_b, w_down):
        h = _norm_mod([x], norm2, mod, 3, 4, **groups)
        act = _ffn_up(h, w_up, conv_w, conv_b, rows_p, len_p, len_s)
        return _matmul_residual([[(act, 0)]], w_down.astype(BF16), [x], mod, 5, 512, name="ffn_down",
                                bm=512, **groups)

    h = _norm_mod(x_parts, l0_norm1, mod0, 0, 1, **groups)
    u = _matmul(h, l0_w_in, F32, 512, "l0_in_proj_u", 0, S5_WIDTH)
    q0 = _matmul(h, l0_w_in, BF16, 512, "l0_in_proj_q", S5_WIDTH, WIN_Q_WIDTH)
    kcol = S5_WIDTH + WIN_Q_WIDTH
    vcol = kcol + WIN_KV_WIDTH
    out_k0 = _matmul_heads(h, l0_w_in, kcol, WIN_KV_HEADS, 0, rows_p, len_p, "l0_in_proj_k_ctx")
    out_v0 = _matmul_heads(h, l0_w_in, vcol, WIN_KV_HEADS, 0, rows_p, len_p, "l0_in_proj_v_ctx")
    k0_s = _matmul_heads(h, l0_w_in, kcol, WIN_KV_HEADS, rows_p, rows_s, len_s, "l0_in_proj_k_lat")
    v0_s = _matmul_heads(h, l0_w_in, vcol, WIN_KV_HEADS, rows_p, rows_s, len_s, "l0_in_proj_v_lat")

    assert rows_p % rows_s == 0
    sel = _s5_lane_permutation()
    sb, t_op, e_op, a16 = _s5_operators(l0_lambda_re, l0_lambda_im, l0_log_dt, l0_b_re, l0_b_im, l0_c_re, l0_c_im, l0_d_skip)
    nblk, gps, pw = S5_WIDTH // LANE, GROUPS_PER_STEP, 2 * S5_STATE

    def planes(cols):
        b = cols.shape[1]
        c5 = cols.reshape(nblk, gps, b, 2, pw).transpose(0, 2, 3, 1, 4)
        return c5.reshape(nblk, b, 2 * STATE_W)

    def state_cols(s):
        return s.astype(F32).transpose(2, 0, 1, 3).reshape(S5_GROUPS, s.shape[0], pw)

    np8, ns8 = -(-n_p // SUBLANE) * SUBLANE, -(-n_s // SUBLANE) * SUBLANE
    a_planes = planes(a16)
    h0_lat = planes(jnp.concatenate([state_cols(state_l0_re), state_cols(state_l0_im)], axis=-1))
    h0_s = jnp.pad(h0_lat, ((0, 0), (0, ns8 - n_s), (0, 0)))
    h0_p = jnp.zeros((nblk, np8, 2 * STATE_W), F32)
    y_p, hfin = _s5_stream(u, 0, n_p, len_p, sel, sb, t_op, e_op, a_planes, h0_p)
    y_s, _ = _s5_stream(u, rows_p // rows_s, n_s, len_s, sel, sb, t_op, e_op, a_planes, h0_s)
    a_out = _glu([y_p, y_s], l0_w_glu, l0_b_glu, rows_p)

    def state_out(plane):
        return (plane[:, :n_p].reshape(nblk, n_p, gps, 2, S5_STATE).transpose(1, 3, 0, 2, 4)
                .reshape(n_p, 2, S5_GROUPS, S5_STATE))

    out_sre = state_out(hfin[:, :, :STATE_W])
    out_sim = state_out(hfin[:, :, STATE_W:])

    cos_e, sin_e = _rope_tables(len_s, HEAD_DIM)
    o_p = _ctx_gqa_attention(q0.reshape(seqs_p_units, len_p, -1), out_k0, out_v0, l0_sink, n_p, len_p)
    o_s = _lat_window_attention(q0.reshape(seqs_s_units, len_s, -1), seq0_s, k0_s, v0_s,
                                cache_l0_k, cache_l0_v, l0_sink, cos_e, sin_e)
    x = _matmul_residual([[(a_out, 0)], [(o_p.reshape(rows_p, half), 0), (o_s.reshape(rows_s, half), 0)]],
                         l0_w_out.astype(BF16), x_parts, mod0, 2, 512, name="l0_out_proj", **groups)
    x = ffn(x, mod0, l0_norm2, l0_ffn_w_up, l0_ffn_conv_w, l0_ffn_conv_b, l0_ffn_w_down)

    s1, s2 = MLA_Q_LORA + MLA_KV_LORA, MLA_Q_LORA + MLA_KV_LORA + MLA_ROPE
    w_lora = l1_w_in[:, :s1].astype(BF16)
    w_dkv = l1_w_in[:, s2:].astype(BF16)
    w_kpe = jnp.pad(l1_w_in[:, s1:s2], ((0, 0), (0, LANE - MLA_ROPE))).astype(BF16)
    w_uq = l1_w_uq.reshape(MLA_Q_LORA, MLA_HEADS, MLA_NOPE + MLA_ROPE)
    w_uq = jnp.pad(w_uq, ((0, 0), (0, 0), (0, LANE - MLA_ROPE))).reshape(MLA_Q_LORA, -1).astype(BF16)

    h = _norm_mod([x], l1_norm1, mod1, 0, 1, **groups)
    cqkv = _matmul(h, w_lora, F32, 512, "l1_in_proj_lora")
    kpe = _matmul(h, w_kpe, F32, LANE, "l1_in_proj_kpe")
    qd = _matmul(h, w_dkv, BF16, 512, "l1_in_proj_qd", 0, NA_WIDTH)
    out_k1 = _matmul_heads(h, w_dkv, NA_WIDTH, NA_HEADS, 0, rows_p, len_p, "l1_in_proj_kd_ctx")
    out_v1 = _matmul_heads(h, w_dkv, 2 * NA_WIDTH, NA_HEADS, 0, rows_p, len_p, "l1_in_proj_vd_ctx")
    k1_s = _matmul_heads(h, w_dkv, NA_WIDTH, NA_HEADS, rows_p, rows_s, len_s, "l1_in_proj_kd_lat")
    v1_s = _matmul_heads(h, w_dkv, 2 * NA_WIDTH, NA_HEADS, rows_p, rows_s, len_s, "l1_in_proj_vd_lat")
    cqn = _rmsnorm_cols(cqkv, l1_q_norm, 0, MLA_Q_LORA, BF16)
    ckvn = _rmsnorm_cols(cqkv, l1_kv_norm, MLA_Q_LORA // MLA_KV_LORA, MLA_KV_LORA, F32)
    q_all = _matmul(cqn, w_uq, BF16, 2048, "l1_q_up")
    w_ukv = l1_w_ukv.astype(BF16)
    kv_all = _matmul(ckvn, w_ukv, BF16, 2048, "l1_kv_up")
    lc = cache_l1_ckv.shape[1]
    kv_ctx = _matmul(cache_l1_ckv.reshape(-1, MLA_KV_LORA), w_ukv, BF16, 2048, "l1_kv_up_ctx")

    out_ckv = ckvn[:rows_p].reshape(n_p, len_p, MLA_KV_LORA)
    out_kpe = kpe[:rows_p, :MLA_ROPE].reshape(n_p, len_p, MLA_ROPE)

    o_p = _ctx_odd_attention(q_all.reshape(seqs_p_units, len_p, -1),
                             kv_all.reshape(seqs_p_units, len_p, -1),
                             kpe.reshape(seqs_p_units, len_p, LANE),
                             qd.reshape(seqs_p_units, len_p, -1), out_k1, out_v1, n_p, len_p)
    cos_o, sin_o = _rope_tables(len_s, MLA_ROPE)
    kpe_ctx = jnp.pad(cache_l1_kpe, ((0, 0), (0, 0), (0, LANE - MLA_ROPE)))
    oc_s = _lat_mla_attention(q_all.reshape(seqs_s_units, len_s, -1),
                              kv_all.reshape(seqs_s_units, len_s, -1),
                              kv_ctx.reshape(n_s, lc, -1),
                              kpe.reshape(seqs_s_units, len_s, LANE), kpe_ctx,
                              seq0_s, n_s, len_s, cos_o, sin_o)
    od_s = _lat_na_attention(qd.reshape(seqs_s_units, len_s, -1), seq0_s, k1_s, v1_s,
                             cache_l1_k, cache_l1_v, _na_bias(l1_rpb, len_s))
    o_p2 = o_p.reshape(rows_p, 2 * half)
    x = _matmul_residual([[(o_p2, 0), (oc_s.reshape(rows_s, half), 0)],
                          [(o_p2, 1), (od_s.reshape(rows_s, half), 0)]],
                         l1_w_out.astype(BF16), [x], mod1, 2, 512, name="l1_out_proj", **groups)
    x = ffn(x, mod1, l1_norm2, l1_ffn_w_up, l1_ffn_conv_w, l1_ffn_conv_b, l1_ffn_w_down)

    y_prompt = _rmsnorm_cols(x, final_norm, 0, d, F32, 0, rows_p).reshape(n_p, len_p, d)
    y_sample = _rmsnorm_cols(x, final_norm, 0, d, F32, rows_p, rows_s).reshape(n_s, len_s, d)
    return (y_prompt, y_sample, out_k0, out_v0, out_sre, out_sim, out_ckv, out_kpe, out_k1, out_v1)
```

```python
import functools
import math

import jax
import jax.numpy as jnp
from jax import lax
from jax.experimental import pallas as pl
from jax.experimental.pallas import tpu as pltpu

F32 = jnp.float32
BF16 = jnp.bfloat16

D_MODEL = 4096
GRID_W = 64
HEAD_DIM = 128
S5_WIDTH = 2048
S5_GROUP = 16
S5_GROUPS = S5_WIDTH // S5_GROUP
S5_STATE = 64
S5_CHUNK = 16
WIN_HEADS = 16
WIN_KV_HEADS = 4
WIN_GROUP = WIN_HEADS // WIN_KV_HEADS
WIN_RADIUS = 128
WIN_Q_WIDTH = WIN_HEADS * HEAD_DIM
WIN_KV_WIDTH = WIN_KV_HEADS * HEAD_DIM
MLA_HEADS = 16
MLA_Q_LORA = 1024
MLA_KV_LORA = 512
MLA_NOPE = 128
MLA_ROPE = 64
MLA_V = 128
NA_HEADS = 16
NA_ROWS = 8
NA_COLS = 16
NA_WIDTH = NA_HEADS * HEAD_DIM
D_FF = 11008
ROPE_BASE = 10000.0
EPS = 1e-6
NEG = -1e30

LANE = 128
SUBLANE = 8
MAX_CONDS = 8
ROW_TILE = 1024
ATTN_HEADS_PER_STEP = 2


def _params(sem, vmem_mb):
    return pltpu.CompilerParams(dimension_semantics=sem, vmem_limit_bytes=vmem_mb * 1024 * 1024)


def _cond_index(i, bm, n_prompt_rows, lat_len):
    first = n_prompt_rows // bm
    per = lat_len // bm
    return jnp.where(i < first, 0, 1 + (i - first) // per)


def _dot(a, b):
    return jnp.dot(a, b, preferred_element_type=F32)


def _dot_t(a, b):
    return lax.dot_general(a, b, (((1,), (1,)), ((), ())), preferred_element_type=F32)


def _sigmoid(x):
    return 1.0 / (1.0 + jnp.exp(-x))


def _row_specs(parts, bm, width, col_fn, n_prompt_blocks, single_buffer=False):
    mode = dict(pipeline_mode=pl.Buffered(1)) if single_buffer else {}
    if len(parts) == 1:
        cb = parts[0][1]
        return [pl.BlockSpec((bm, width), lambda i, j: (i, col_fn(j, cb)), **mode)]
    cb_p, cb_s = parts[0][1], parts[1][1]
    last_p = n_prompt_blocks - 1
    return [pl.BlockSpec((bm, width), lambda i, j: (jnp.minimum(i, last_p), col_fn(j, cb_p)), **mode),
            pl.BlockSpec((bm, width), lambda i, j: (jnp.maximum(i - n_prompt_blocks, 0), col_fn(j, cb_s)), **mode)]


def _by_stream(i, n_prompt_blocks, operands, body):
    if all(len(o) == 1 for o in operands):
        body([o[0] for o in operands])
        return

    @pl.when(i < n_prompt_blocks)
    def _():
        body([o[0] for o in operands])

    @pl.when(i >= n_prompt_blocks)
    def _():
        body([o[-1] for o in operands])


def _mod_kernel(c_ref, w_ref, b_ref, o_ref):
    c = c_ref[...]
    s = c * _sigmoid(c)
    o_ref[...] = _dot(s.astype(BF16), w_ref[...].astype(BF16)) + b_ref[...]


def _modulation(cond, w, b):
    d, n = w.shape
    bn = 512
    out = pl.pallas_call(
        _mod_kernel,
        grid=(n // bn,),
        in_specs=[pl.BlockSpec((MAX_CONDS, d), lambda j: (0, 0)),
                  pl.BlockSpec((d, bn), lambda j: (0, j)),
                  pl.BlockSpec((1, bn), lambda j: (0, j))],
        out_specs=pl.BlockSpec((MAX_CONDS, bn), lambda j: (0, j)),
        out_shape=jax.ShapeDtypeStruct((MAX_CONDS, n), F32),
        compiler_params=_params(("arbitrary",), 40),
        name="modulation",
    )(cond, w, b.reshape(1, n))
    return out.reshape(MAX_CONDS, 1, n)


def _norm_mod_kernel(*refs, n_x, n_prompt_blocks):
    x_refs, (g_ref, sh_ref, sc_ref, o_ref) = refs[:n_x], refs[n_x:]

    def body(r):
        x = r[0][...]
        ms = jnp.mean(x * x, axis=-1, keepdims=True)
        y = x * lax.rsqrt(ms + EPS) * g_ref[...]
        o_ref[...] = (y * (1.0 + sc_ref[...]) + sh_ref[...]).astype(o_ref.dtype)

    _by_stream(pl.program_id(0), n_prompt_blocks, [x_refs], body)


def _norm_mod(x_parts, gain, mod, shift_slot, scale_slot, n_prompt_rows, lat_len):
    d = D_MODEL
    bm = 256
    t = sum(x.shape[0] for x in x_parts)
    npb = n_prompt_rows // bm
    cidx = functools.partial(_cond_index, bm=bm, n_prompt_rows=n_prompt_rows, lat_len=lat_len)
    specs = _row_specs([(x, 0) for x in x_parts], bm, d, lambda j, cb: 0, npb)
    to1d = lambda spec: pl.BlockSpec(spec.block_shape, lambda i, f=spec.index_map: f(i, 0))
    return pl.pallas_call(
        functools.partial(_norm_mod_kernel, n_x=len(x_parts), n_prompt_blocks=npb),
        grid=(t // bm,),
        in_specs=[to1d(s) for s in specs] + [
            pl.BlockSpec((1, d), lambda i: (0, 0)),
            pl.BlockSpec((None, 1, d), lambda i: (cidx(i), 0, shift_slot)),
            pl.BlockSpec((None, 1, d), lambda i: (cidx(i), 0, scale_slot))],
        out_specs=pl.BlockSpec((bm, d), lambda i: (i, 0)),
        out_shape=jax.ShapeDtypeStruct((t, d), BF16),
        compiler_params=_params(("arbitrary",), 40),
        name="norm_mod",
    )(*x_parts, gain.reshape(1, d), mod, mod)


def _rmsnorm_kernel(x_ref, g_ref, o_ref):
    x = x_ref[...].astype(F32)
    ms = jnp.mean(x * x, axis=-1, keepdims=True)
    o_ref[...] = (x * lax.rsqrt(ms + EPS) * g_ref[...]).astype(o_ref.dtype)


def _rmsnorm_cols(x, gain, col_block, width, out_dtype, row0=0, rows=None):
    bm = 256
    rows = x.shape[0] if rows is None else rows
    rb0 = row0 // bm
    return pl.pallas_call(
        _rmsnorm_kernel,
        grid=(rows // bm,),
        in_specs=[pl.BlockSpec((bm, width), lambda i: (i + rb0, col_block)),
                  pl.BlockSpec((1, width), lambda i: (0, 0))],
        out_specs=pl.BlockSpec((bm, width), lambda i: (i, 0)),
        out_shape=jax.ShapeDtypeStruct((rows, width), out_dtype),
        compiler_params=_params(("arbitrary",), 40),
        name="rmsnorm",
    )(x, gain.reshape(1, width))


def _mm_kernel(x_ref, w_ref, o_ref):
    o_ref[...] = _dot(x_ref[...].astype(BF16), w_ref[...].astype(BF16)).astype(o_ref.dtype)


def _matmul(x, w, out_dtype, bn, name, col0=0, ncols=None):
    m, k = x.shape
    ncols = w.shape[1] - col0 if ncols is None else ncols
    bm = min(ROW_TILE, m)
    assert m % bm == 0 and ncols % bn == 0 and col0 % bn == 0
    cb0 = col0 // bn
    return pl.pallas_call(
        _mm_kernel,
        grid=(m // bm, ncols // bn),
        in_specs=[pl.BlockSpec((bm, k), lambda i, j: (i, 0)),
                  pl.BlockSpec((k, bn), lambda i, j: (0, cb0 + j))],
        out_specs=pl.BlockSpec((bm, bn), lambda i, j: (i, j)),
        out_shape=jax.ShapeDtypeStruct((m, ncols), out_dtype),
        compiler_params=_params(("arbitrary", "arbitrary"), 56),
        name=name,
    )(x, w)


def _mm_heads_kernel(x_ref, w_ref, o_ref, *, seqs, seq_len, heads):
    acc = _dot(x_ref[...], w_ref[...].astype(BF16))
    for b in range(seqs):
        for hh in range(heads):
            o_ref[b, hh] = acc[b * seq_len:(b + 1) * seq_len, hh * HEAD_DIM:(hh + 1) * HEAD_DIM]


def _matmul_heads(x, w, col0, heads, row0, rows, seq_len, name):
    k = x.shape[1]
    bm = ROW_TILE
    hb = 4
    bn = hb * HEAD_DIM
    assert rows % bm == 0 and row0 % bm == 0 and bm % seq_len == 0 and heads % hb == 0 and col0 % bn == 0
    seqs = bm // seq_len
    rb0, cb0 = row0 // bm, col0 // bn
    return pl.pallas_call(
        functools.partial(_mm_heads_kernel, seqs=seqs, seq_len=seq_len, heads=hb),
        grid=(rows // bm, heads // hb),
        in_specs=[pl.BlockSpec((bm, k), lambda i, j: (rb0 + i, 0)),
                  pl.BlockSpec((k, bn), lambda i, j: (0, cb0 + j))],
        out_specs=pl.BlockSpec((seqs, hb, seq_len, HEAD_DIM), lambda i, j: (i, j, 0, 0)),
        out_shape=jax.ShapeDtypeStruct((rows // seq_len, heads, seq_len, HEAD_DIM), F32),
        compiler_params=_params(("arbitrary", "arbitrary"), 56),
        name=name,
    )(x, w)


def _mm_res_kernel(*refs, n_lhs, n_parts, n_res, n_prompt_blocks):
    pos = 0
    lhs = []
    for n in n_lhs:
        lhs.append(refs[pos:pos + n])
        pos += n
    w_refs = refs[pos:pos + n_parts]
    pos += n_parts
    res = refs[pos:pos + n_res]
    pos += n_res
    gate_ref, o_ref = refs[pos], refs[pos + 1]

    def body(r):
        acc = None
        for x_ref, w_ref in zip(r[:n_parts], w_refs):
            part = _dot(x_ref[...], w_ref[...].astype(BF16))
            acc = part if acc is None else acc + part
        o_ref[...] = r[n_parts][...] + gate_ref[...] * acc

    _by_stream(pl.program_id(0), n_prompt_blocks, lhs + [res], body)


def _matmul_residual(lhs_parts, w, res_parts, mod, gate_slot, bn, n_prompt_rows, lat_len, name,
                     single_buffer_x=False, vmem_mb=56, bm=ROW_TILE):
    n = w.shape[1]
    kq = w.shape[0] // len(lhs_parts)
    t = sum(r.shape[0] for r in res_parts)
    npb = n_prompt_rows // bm
    cidx = functools.partial(_cond_index, bm=bm, n_prompt_rows=n_prompt_rows, lat_len=lat_len)
    gate_col0 = gate_slot * (D_MODEL // bn)
    in_specs, args = [], []
    for parts in lhs_parts:
        in_specs += _row_specs(parts, bm, kq, lambda j, cb: cb, npb, single_buffer=single_buffer_x)
        args += [a for a, _ in parts]
    for q in range(len(lhs_parts)):
        in_specs.append(pl.BlockSpec((kq, bn), lambda i, j, q=q: (q, j)))
        args.append(w)
    in_specs += _row_specs([(r, 0) for r in res_parts], bm, bn, lambda j, cb: j, npb)
    args += list(res_parts)
    in_specs.append(pl.BlockSpec((None, 1, bn), lambda i, j: (cidx(i), 0, gate_col0 + j)))
    args.append(mod)
    body = functools.partial(_mm_res_kernel, n_lhs=tuple(len(p) for p in lhs_parts), n_parts=len(lhs_parts),
                             n_res=len(res_parts), n_prompt_blocks=npb)
    return pl.pallas_call(
        body,
        grid=(t // bm, n // bn),
        in_specs=in_specs,
        out_specs=pl.BlockSpec((bm, bn), lambda i, j: (i, j)),
        out_shape=jax.ShapeDtypeStruct((t, n), F32),
        compiler_params=_params(("arbitrary", "arbitrary"), vmem_mb),
        name=name,
    )(*args)


def _glu_kernel(*refs, n_y, n_prompt_blocks):
    y_refs = refs[:n_y]
    w_ref, b_ref, o_ref = refs[n_y:]

    def body(r):
        y = r[0][...]
        z = _dot(y.astype(BF16), w_ref[...]) + b_ref[...]
        o_ref[...] = (y * _sigmoid(z)).astype(o_ref.dtype)

    _by_stream(pl.program_id(0), n_prompt_blocks, [y_refs], body)


def _glu(y_parts, w, b, n_prompt_rows):
    k, n = w.shape
    bm = ROW_TILE
    m = sum(y.shape[0] for y in y_parts)
    npb = n_prompt_rows // bm
    parts = [(y, 0) for y in y_parts]
    return pl.pallas_call(
        functools.partial(_glu_kernel, n_y=len(y_parts), n_prompt_blocks=npb),
        grid=(m // bm, 1),
        in_specs=(_row_specs(parts, bm, k, lambda j, cb: 0, npb)
                  + [pl.BlockSpec((k, n), lambda i, j: (0, 0), pipeline_mode=pl.Buffered(1)),
                     pl.BlockSpec((1, n), lambda i, j: (0, 0))]),
        out_specs=pl.BlockSpec((bm, n), lambda i, j: (i, 0)),
        out_shape=jax.ShapeDtypeStruct((m, n), BF16),
        compiler_params=_params(("arbitrary", "arbitrary"), 56),
        name="s5_glu",
    )(*y_parts, w.astype(BF16), b.reshape(1, n))


def _ffn_up_kernel(x_ref, wg_ref, wv_ref, cwg_ref, cwv_ref, cbg_ref, cbv_ref, o_ref, *,
                   sub, n_sub, prompt_blocks, prompt_len, lat_len):
    i = pl.program_id(0)
    row = lax.broadcasted_iota(jnp.int32, (sub, 1), 0)
    wg = wg_ref[...].astype(BF16)
    wv = wv_ref[...].astype(BF16)

    for s in range(n_sub):
        period = jnp.where(i * n_sub + s < prompt_blocks, prompt_len, lat_len)
        pos = row & (period - 1)
        first = pos == 0
        last = pos == period - 1

        def conv(u, cw_ref, cb_ref):
            prev = jnp.where(first, 0.0, pltpu.roll(u, 1, 0))
            nxt = jnp.where(last, 0.0, pltpu.roll(u, sub - 1, 0))
            return prev * cw_ref[0:1, :] + u * cw_ref[1:2, :] + nxt * cw_ref[2:3, :] + cb_ref[...]

        x = x_ref[s * sub:(s + 1) * sub, :]
        g = conv(_dot(x, wg), cwg_ref, cbg_ref)
        v = conv(_dot(x, wv), cwv_ref, cbv_ref)
        o_ref[s * sub:(s + 1) * sub, :] = (g * _sigmoid(g) * v).astype(o_ref.dtype)


FFN_SUB_BLOCKS = 4


def _ffn_up(h, w_up, conv_w, conv_b, n_prompt_rows, prompt_len, lat_len):
    t, d = h.shape
    sub, bn = lat_len, 256
    n_sub = FFN_SUB_BLOCKS if (t // sub) % FFN_SUB_BLOCKS == 0 else 2
    bm = sub * n_sub
    assert t % bm == 0 and n_prompt_rows % sub == 0 and sub % prompt_len == 0
    assert prompt_len & (prompt_len - 1) == 0 and lat_len & (lat_len - 1) == 0
    nj = D_FF // bn
    conv_b = conv_b.reshape(1, 2 * D_FF)
    body = functools.partial(_ffn_up_kernel, sub=sub, n_sub=n_sub, prompt_blocks=n_prompt_rows // sub,
                             prompt_len=prompt_len, lat_len=lat_len)
    return pl.pallas_call(
        body,
        grid=(t // bm, nj),
        in_specs=[pl.BlockSpec((bm, d), lambda i, j: (i, 0), pipeline_mode=pl.Buffered(1)),
                  pl.BlockSpec((d, bn), lambda i, j: (0, j)),
                  pl.BlockSpec((d, bn), lambda i, j: (0, nj + j)),
                  pl.BlockSpec((3, bn), lambda i, j: (0, j)),
                  pl.BlockSpec((3, bn), lambda i, j: (0, nj + j)),
                  pl.BlockSpec((1, bn), lambda i, j: (0, j)),
                  pl.BlockSpec((1, bn), lambda i, j: (0, nj + j))],
        out_specs=pl.BlockSpec((bm, bn), lambda i, j: (i, j)),
        out_shape=jax.ShapeDtypeStruct((t, D_FF), BF16),
        compiler_params=_params(("arbitrary", "arbitrary"), 60),
        name="ffn_up_conv",
    )(h, w_up, w_up, conv_w, conv_w, conv_b, conv_b)


def _s5_operators(lam_re, lam_im, log_dt, b_re, b_im, c_re, c_im, d_skip):
    q, g, p, c = S5_CHUNK, S5_GROUPS, S5_STATE, S5_GROUP
    dt = jnp.exp(log_dt)[None, :, :, None]
    tau = jnp.arange(q + 1, dtype=F32)[:, None, None, None]
    mag = jnp.exp(lam_re[None] * dt * tau)
    ang = lam_im[None] * dt * tau
    pw_re, pw_im = mag * jnp.cos(ang), mag * jnp.sin(ang)
    dt1 = jnp.exp(log_dt)[:, :, None]
    m1 = jnp.exp(lam_re * dt1)
    ab_re, ab_im = m1 * jnp.cos(lam_im * dt1), m1 * jnp.sin(lam_im * dt1)
    den = lam_re * lam_re + lam_im * lam_im
    nr, ni = ab_re - 1.0, ab_im
    f_re = ((nr * lam_re + ni * lam_im) / den)[..., None]
    f_im = ((ni * lam_re - nr * lam_im) / den)[..., None]
    bb_re = f_re * b_re - f_im * b_im
    bb_im = f_re * b_im + f_im * b_re
    ct_re, ct_im = jnp.swapaxes(c_re, -1, -2), jnp.swapaxes(c_im, -1, -2)

    def power_times_c(fwd_taus, bwd_taus):
        pr = jnp.stack([pw_re[fwd_taus, 0], pw_re[bwd_taus, 1]], axis=0).transpose(0, 2, 3, 1)
        pi = jnp.stack([pw_im[fwd_taus, 0], pw_im[bwd_taus, 1]], axis=0).transpose(0, 2, 3, 1)
        r_re = pr[..., None] * ct_re[:, :, :, None, :] - pi[..., None] * ct_im[:, :, :, None, :]
        r_im = pr[..., None] * ct_im[:, :, :, None, :] + pi[..., None] * ct_re[:, :, :, None, :]
        return r_re.reshape(2, g, p, q * c), r_im.reshape(2, g, p, q * c)

    lags = jnp.arange(q)
    rk_re, rk_im = power_times_c(lags, lags[::-1])
    diag = jnp.tile(d_skip[:, :, None] * jnp.eye(c, dtype=F32)[None], (1, 1, q))
    t_op = _s5_toeplitz(rk_re, rk_im, bb_re, bb_im, diag)
    re_re, re_im = power_times_c(lags + 1, q - lags)
    zero = jnp.zeros_like(re_re[0])
    e_op = jnp.concatenate([re_re[0], zero, -re_im[0], zero, zero, re_re[1], zero, -re_im[1]], axis=1)
    def s_op(taus, d):
        pr, pi = pw_re[taus, d][:, :, None, :], pw_im[taus, d][:, :, None, :]
        br, bi = jnp.swapaxes(bb_re[d], -1, -2)[None], jnp.swapaxes(bb_im[d], -1, -2)[None]
        s_re = (pr * br - pi * bi).transpose(1, 0, 2, 3).reshape(g, q * c, p)
        s_im = (pr * bi + pi * br).transpose(1, 0, 2, 3).reshape(g, q * c, p)
        return s_re, s_im

    sf_re, sf_im = s_op(lags[::-1], 0)
    sb_re, sb_im = s_op(lags, 1)
    sb = jnp.concatenate([sf_re, sb_re, sf_im, sb_im], axis=2)
    a16 = jnp.concatenate([pw_re[q, 0], pw_re[q, 1], pw_im[q, 0], pw_im[q, 1]], axis=-1)[:, None, :]
    return sb.astype(BF16), t_op, e_op.astype(BF16), a16


def _s5_toeplitz_kernel(rr_ref, ri_ref, br_ref, bi_ref, dg_ref, o_ref, k_scr):
    c, q = S5_GROUP, S5_CHUNK
    for d in range(2):
        for ci in range(c):
            prod = br_ref[d, :, ci:ci + 1] * rr_ref[d] - bi_ref[d, :, ci:ci + 1] * ri_ref[d]
            k_scr[d * c + ci:d * c + ci + 1, :] = jnp.sum(prod, axis=0, keepdims=True)
    kf = k_scr[0:c, :]
    kb = k_scr[c:2 * c, :]
    lane = lax.broadcasted_iota(jnp.int32, (c, q * c), 1)
    dg = dg_ref[...]
    for s in range(q):
        f = kf if s == 0 else pltpu.roll(kf, c * s, 1)
        b = kb if s == q - 1 else pltpu.roll(kb, q * c - c * (q - 1 - s), 1)
        piece = jnp.where(lane >= c * s, f, 0.0) + jnp.where(lane < c * (s + 1), b, 0.0)
        piece = piece + jnp.where(jnp.logical_and(lane >= c * s, lane < c * (s + 1)), dg, 0.0)
        o_ref[s * c:(s + 1) * c, :] = piece.astype(o_ref.dtype)


def _s5_toeplitz(rk_re, rk_im, bb_re, bb_im, diag):
    g, p, c, qc = S5_GROUPS, S5_STATE, S5_GROUP, S5_CHUNK * S5_GROUP
    r_spec = pl.BlockSpec((2, None, p, qc), lambda i: (0, i, 0, 0))
    b_spec = pl.BlockSpec((2, None, p, c), lambda i: (0, i, 0, 0))
    return pl.pallas_call(
        _s5_toeplitz_kernel,
        grid=(g,),
        in_specs=[r_spec, r_spec, b_spec, b_spec, pl.BlockSpec((None, c, qc), lambda i: (i, 0, 0))],
        out_specs=pl.BlockSpec((None, qc, qc), lambda i: (i, 0, 0)),
        out_shape=jax.ShapeDtypeStruct((g, qc, qc), BF16),
        scratch_shapes=[pltpu.VMEM((2 * c, qc), F32)],
        compiler_params=_params(("arbitrary",), 32),
        name="s5_toeplitz",
    )(rk_re, rk_im, bb_re, bb_im, diag)


def _gelu_tanh(x):
    return 0.5 * x * (1.0 + jnp.tanh(math.sqrt(2.0 / math.pi) * (x + 0.044715 * (x * x * x))))


GROUPS_PER_STEP = LANE // S5_GROUP
STATE_W = GROUPS_PER_STEP * 2 * S5_STATE


def _s5_lane_permutation():
    j = jnp.arange(S5_CHUNK * LANE)
    s, gl, c = j // LANE, (j % LANE) // S5_GROUP, j % S5_GROUP
    k = gl * (S5_CHUNK * S5_GROUP) + s * S5_GROUP + c
    return (k[:, None] == jnp.arange(S5_CHUNK * LANE)[None, :]).astype(BF16)


def _s5_kernel(u_ref, sel_ref, sb_ref, t_ref, e_ref, a_ref, h0_ref, y_ref, hfin_ref, xg, w_scr, h_scr, *,
               batch, nb, nc, seq_len):
    qc = S5_CHUNK * S5_GROUP
    if batch < nb:
        xg[...] = jnp.zeros(xg.shape, BF16)
    for ch in range(nc):
        for s in range(S5_CHUNK):
            xg[ch * nb:ch * nb + batch, s * LANE:(s + 1) * LANE] = (
                u_ref[pl.ds(ch * S5_CHUNK + s, batch, stride=seq_len), :].astype(BF16))
    sel = sel_ref[...]
    ucat = _dot(xg[...], sel).astype(BF16)
    for gl in range(GROUPS_PER_STEP):
        w = _dot(ucat[:, gl * qc:(gl + 1) * qc], sb_ref[gl])
        w_scr[:, gl * LANE:(gl + 1) * LANE] = w[:, 0:LANE]
        w_scr[:, STATE_W + gl * LANE:STATE_W + (gl + 1) * LANE] = w[:, LANE:2 * LANE]
    ar = a_ref[0:1, 0:STATE_W]
    ai = a_ref[0:1, STATE_W:2 * STATE_W]
    fwd_lane = (lax.broadcasted_iota(jnp.int32, (1, STATE_W), 1) & S5_STATE) == 0
    hr = h0_ref[:, 0:STATE_W]
    hi = h0_ref[:, STATE_W:2 * STATE_W]
    for i in range(nc):
        ri = i * nb
        rj = (nc - 1 - i) * nb
        h_scr[ri:ri + nb, 0:STATE_W] = hr
        h_scr[ri:ri + nb, STATE_W:2 * STATE_W] = hi
        h_scr[rj:rj + nb, 2 * STATE_W:3 * STATE_W] = hr
        h_scr[rj:rj + nb, 3 * STATE_W:4 * STATE_W] = hi
        wr = jnp.where(fwd_lane, w_scr[ri:ri + nb, 0:STATE_W], w_scr[rj:rj + nb, 0:STATE_W])
        wi = jnp.where(fwd_lane, w_scr[ri:ri + nb, STATE_W:2 * STATE_W], w_scr[rj:rj + nb, STATE_W:2 * STATE_W])
        hr, hi = ar * hr - ai * hi + wr, ar * hi + ai * hr + wi
    hfin_ref[:, 0:STATE_W] = hr
    hfin_ref[:, STATE_W:2 * STATE_W] = hi
    for gl in range(GROUPS_PER_STEP):
        hcat = jnp.concatenate([h_scr[:, k * STATE_W + gl * LANE:k * STATE_W + (gl + 1) * LANE] for k in range(4)],
                               axis=1).astype(BF16)
        y = _dot(ucat[:, gl * qc:(gl + 1) * qc], t_ref[gl]) + _dot(hcat, e_ref[gl])
        xg[:, gl * qc:(gl + 1) * qc] = _gelu_tanh(y).astype(BF16)
    yp = _dot_t(xg[...], sel)
    for ch in range(nc):
        for s in range(S5_CHUNK):
            y_ref[pl.ds(ch * S5_CHUNK + s, batch, stride=seq_len), :] = (
                yp[ch * nb:ch * nb + batch, s * LANE:(s + 1) * LANE])


def _s5_chunk_rows(batch):
    return batch if batch % (SUBLANE // 2) == 0 else -(-batch // SUBLANE) * SUBLANE


def _s5_stream(u, row_block, batch, seq_len, sel, sb, t_op, e_op, a_planes, h0):
    n_rows = batch * seq_len
    nb = _s5_chunk_rows(batch)
    nc = seq_len // S5_CHUNK
    rows = nb * nc
    nblk = S5_WIDTH // LANE
    qc = S5_CHUNK * S5_GROUP
    body = functools.partial(_s5_kernel, batch=batch, nb=nb, nc=nc, seq_len=seq_len)
    return pl.pallas_call(
        body,
        grid=(nblk,),
        in_specs=[pl.BlockSpec((n_rows, LANE), lambda g: (row_block, g)),
                  pl.BlockSpec((S5_CHUNK * LANE, S5_CHUNK * LANE), lambda g: (0, 0), pipeline_mode=pl.Buffered(1)),
                  pl.BlockSpec((GROUPS_PER_STEP, qc, 4 * S5_STATE), lambda g: (g, 0, 0)),
                  pl.BlockSpec((GROUPS_PER_STEP, qc, qc), lambda g: (g, 0, 0)),
                  pl.BlockSpec((GROUPS_PER_STEP, 8 * S5_STATE, qc), lambda g: (g, 0, 0)),
                  pl.BlockSpec((None, 1, 2 * STATE_W), lambda g: (g, 0, 0)),
                  pl.BlockSpec((None, nb, 2 * STATE_W), lambda g: (g, 0, 0))],
        out_specs=[pl.BlockSpec((n_rows, LANE), lambda g: (0, g)),
                   pl.BlockSpec((None, nb, 2 * STATE_W), lambda g: (g, 0, 0))],
        out_shape=[jax.ShapeDtypeStruct((n_rows, S5_WIDTH), F32),
                   jax.ShapeDtypeStruct((nblk, nb, 2 * STATE_W), F32)],
        scratch_shapes=[pltpu.VMEM((rows, S5_CHUNK * LANE), BF16),
                        pltpu.VMEM((rows, 2 * STATE_W), F32),
                        pltpu.VMEM((rows, 4 * STATE_W), F32)],
        compiler_params=_params(("arbitrary",), 56),
        name="s5_chunked",
    )(u, sel, sb, t_op, e_op, a_planes, h0)


def _rope_tables(length, rot):
    n_freq = rot // 4
    t = jnp.arange(length)
    row = (t // GRID_W).astype(F32)
    col = (t % GRID_W).astype(F32)
    inv = ROPE_BASE ** (-jnp.arange(n_freq, dtype=F32) / n_freq)
    ar, ac = row[:, None] * inv, col[:, None] * inv
    cos = jnp.concatenate([jnp.cos(ar), jnp.cos(ar), jnp.cos(ac), jnp.cos(ac)], axis=-1)
    sin = jnp.concatenate([-jnp.sin(ar), jnp.sin(ar), -jnp.sin(ac), jnp.sin(ac)], axis=-1)
    pad = LANE - rot
    if pad:
        cos = jnp.concatenate([cos, jnp.ones((length, pad), F32)], axis=-1)
        sin = jnp.concatenate([sin, jnp.zeros((length, pad), F32)], axis=-1)
    return cos, sin


def _rope(x, cos, sin, blk):
    lane = lax.broadcasted_iota(jnp.int32, (1, LANE), 1)
    lower = (lane & blk) == 0
    partner = jnp.where(lower, pltpu.roll(x, LANE - blk, 1), pltpu.roll(x, blk, 1))
    return x * cos + partner * sin


def _softmax_pv(scores, values, extra_logit=None):
    m = scores[0].max(axis=-1, keepdims=True)
    for s in scores[1:]:
        m = jnp.maximum(m, s.max(axis=-1, keepdims=True))
    if extra_logit is not None:
        m = jnp.maximum(m, extra_logit)
    den = None
    out = None
    for s, v in zip(scores, values):
        p = jnp.exp(s - m)
        d = p.sum(axis=-1, keepdims=True)
        o = _dot(p.astype(BF16), v)
        den = d if den is None else den + d
        out = o if out is None else out + o
    if extra_logit is not None:
        den = den + jnp.exp(extra_logit - m)
    return out / den


def _ctx_gqa_kernel(sink_ref, q_ref, k_ref, v_ref, o_ref):
    scale = HEAD_DIM ** -0.5
    for g in range(WIN_KV_HEADS):
        k = k_ref[g].astype(BF16)
        v = v_ref[g].astype(BF16)
        for r in range(WIN_GROUP):
            cols = slice((g * WIN_GROUP + r) * HEAD_DIM, (g * WIN_GROUP + r + 1) * HEAD_DIM)
            s = _dot_t(q_ref[:, cols], k) * scale
            o = _softmax_pv([s], [v], sink_ref[g * WIN_GROUP + r])
            o_ref[:, cols] = o.astype(o_ref.dtype)


def _ctx_gqa_attention(uq3, k4, v4, sink, n_prompt, length):
    kv_spec = pl.BlockSpec((None, WIN_KV_HEADS, length, HEAD_DIM), lambda b: (b, 0, 0, 0))
    return pl.pallas_call(
        _ctx_gqa_kernel,
        grid=(n_prompt,),
        in_specs=[pl.BlockSpec(memory_space=pltpu.SMEM),
                  pl.BlockSpec((None, length, WIN_Q_WIDTH), lambda b: (b, 0, 0)),
                  kv_spec, kv_spec],
        out_specs=pl.BlockSpec((None, length, WIN_Q_WIDTH), lambda b: (b, 0, 0)),
        out_shape=jax.ShapeDtypeStruct((n_prompt, length, WIN_Q_WIDTH), BF16),
        compiler_params=_params(("arbitrary",), 40),
        name="ctx_gqa_attention",
    )(sink, uq3, k4, v4)


def _lat_window_kernel(sink_ref, q_ref, k_ref, v_ref, kc_ref, vc_ref, cos_ref, sin_ref, o_ref, *, length):
    g = pl.program_id(1)
    scale = HEAD_DIM ** -0.5
    blk = WIN_RADIUS
    nb = length // blk
    k = _rope(k_ref[...], cos_ref[...], sin_ref[...], HEAD_DIM // 4).astype(BF16)
    v = v_ref[...].astype(BF16)
    kc = kc_ref[...].astype(BF16)
    vc = vc_ref[...].astype(BF16)
    rows = WIN_GROUP * blk
    row = lax.broadcasted_iota(jnp.int32, (rows, 1), 0)
    sk = jnp.zeros((rows, 1), F32)
    for r in range(WIN_GROUP):
        sk = jnp.where(jnp.logical_and(row >= r * blk, row < (r + 1) * blk), sink_ref[g * WIN_GROUP + r], sk)
    qoff = row & (blk - 1)
    for n in range(nb):
        lo = max(0, n - 1) * blk
        hi = min(nb, n + 2) * blk
        cos = cos_ref[n * blk:(n + 1) * blk, :]
        sin = sin_ref[n * blk:(n + 1) * blk, :]
        q = jnp.concatenate(
            [_rope(q_ref[n * blk:(n + 1) * blk, r * HEAD_DIM:(r + 1) * HEAD_DIM].astype(F32), cos, sin, HEAD_DIM // 4)
             for r in range(WIN_GROUP)], axis=0).astype(BF16)
        dist = (n * blk + qoff) - (lo + lax.broadcasted_iota(jnp.int32, (1, hi - lo), 1))
        visible = jnp.logical_and(dist <= WIN_RADIUS, dist >= -WIN_RADIUS)
        s_loc = jnp.where(visible, _dot_t(q, k[lo:hi]) * scale, NEG)
        s_ctx = _dot_t(q, kc) * scale
        o = _softmax_pv([s_loc, s_ctx], [v[lo:hi], vc], sk)
        for r in range(WIN_GROUP):
            o_ref[n * blk:(n + 1) * blk, r * HEAD_DIM:(r + 1) * HEAD_DIM] = o[r * blk:(r + 1) * blk].astype(o_ref.dtype)


def _lat_window_attention(uq3, seq0, k4, v4, k_ctx, v_ctx, sink, cos, sin):
    n_lat, _, length, _ = k4.shape
    qw = WIN_GROUP * HEAD_DIM
    q0 = 0
    lc = k_ctx.shape[2]
    kv_spec = pl.BlockSpec((None, None, length, HEAD_DIM), lambda b, g: (b, g, 0, 0))
    ctx_spec = pl.BlockSpec((None, None, lc, HEAD_DIM), lambda b, g: (b, g, 0, 0))
    tab_spec = pl.BlockSpec((length, LANE), lambda b, g: (0, 0))
    return pl.pallas_call(
        functools.partial(_lat_window_kernel, length=length),
        grid=(n_lat, WIN_KV_HEADS),
        in_specs=[pl.BlockSpec(memory_space=pltpu.SMEM),
                  pl.BlockSpec((None, length, qw), lambda b, g: (seq0 + b, 0, q0 + g)),
                  kv_spec, kv_spec, ctx_spec, ctx_spec, tab_spec, tab_spec],
        out_specs=pl.BlockSpec((None, length, qw), lambda b, g: (b, 0, g)),
        out_shape=jax.ShapeDtypeStruct((n_lat, length, WIN_Q_WIDTH), BF16),
        compiler_params=_params(("arbitrary", "arbitrary"), 48),
        name="latent_window_attention",
    )(sink, uq3, k4, v4, k_ctx, v_ctx, cos, sin)


def _ctx_odd_kernel(q_ref, kv_ref, kpe_ref, qd_ref, kd_ref, vd_ref, o_ref):
    kw = MLA_NOPE + LANE
    kpe = kpe_ref[...].astype(BF16)
    scale_c = (MLA_NOPE + MLA_ROPE) ** -0.5
    for h in range(MLA_HEADS):
        q = q_ref[:, h * kw:(h + 1) * kw]
        k = jnp.concatenate([kv_ref[:, h * kw:h * kw + MLA_NOPE], kpe], axis=1)
        v = kv_ref[:, h * kw + MLA_NOPE:(h + 1) * kw]
        o = _softmax_pv([_dot_t(q, k) * scale_c], [v])
        o_ref[:, h * MLA_V:(h + 1) * MLA_V] = o.astype(o_ref.dtype)
    scale_d = HEAD_DIM ** -0.5
    base = MLA_HEADS * MLA_V
    for h in range(NA_HEADS):
        q = qd_ref[:, h * HEAD_DIM:(h + 1) * HEAD_DIM]
        o = _softmax_pv([_dot_t(q, kd_ref[h].astype(BF16)) * scale_d], [vd_ref[h].astype(BF16)])
        o_ref[:, base + h * HEAD_DIM:base + (h + 1) * HEAD_DIM] = o.astype(o_ref.dtype)


def _ctx_odd_attention(q3, kv3, kpe3, qd3, kd4, vd4, n_prompt, length):
    qn = q3.shape[-1]
    head_spec = pl.BlockSpec((None, NA_HEADS, length, HEAD_DIM), lambda b: (b, 0, 0, 0))
    return pl.pallas_call(
        _ctx_odd_kernel,
        grid=(n_prompt,),
        in_specs=[pl.BlockSpec((None, length, qn), lambda b: (b, 0, 0)),
                  pl.BlockSpec((None, length, qn), lambda b: (b, 0, 0)),
                  pl.BlockSpec((None, length, LANE), lambda b: (b, 0, 0)),
                  pl.BlockSpec((None, length, NA_WIDTH), lambda b: (b, 0, 0)),
                  head_spec, head_spec],
        out_specs=pl.BlockSpec((None, length, MLA_HEADS * MLA_V + NA_WIDTH), lambda b: (b, 0, 0)),
        out_shape=jax.ShapeDtypeStruct((n_prompt, length, MLA_HEADS * MLA_V + NA_WIDTH), BF16),
        compiler_params=_params(("arbitrary",), 48),
        name="ctx_odd_attention",
    )(q3, kv3, kpe3, qd3, kd4, vd4)


def _lat_mla_kernel(q_ref, kv_ref, kpe_ref, kvc_ref, kpec_ref, cos_ref, sin_ref, o_ref, *, length):
    scale = (MLA_NOPE + MLA_ROPE) ** -0.5
    rb = MLA_ROPE // 4
    hw = MLA_NOPE + LANE
    kpe = _rope(kpe_ref[...], cos_ref[...], sin_ref[...], rb).astype(BF16)
    kpe_ctx = kpec_ref[...].astype(BF16)
    qb = 256
    for hh in range(ATTN_HEADS_PER_STEP):
        c0 = hh * hw
        k_lat = jnp.concatenate([kv_ref[:, c0:c0 + MLA_NOPE], kpe], axis=1)
        v_lat = kv_ref[:, c0 + MLA_NOPE:c0 + hw]
        k_ctx = jnp.concatenate([kvc_ref[:, c0:c0 + MLA_NOPE], kpe_ctx], axis=1)
        v_ctx = kvc_ref[:, c0 + MLA_NOPE:c0 + hw]
        for n in range(length // qb):
            rows = slice(n * qb, (n + 1) * qb)
            q_pe = _rope(q_ref[rows, c0 + MLA_NOPE:c0 + hw].astype(F32), cos_ref[rows, :], sin_ref[rows, :],
                         rb).astype(BF16)
            q = jnp.concatenate([q_ref[rows, c0:c0 + MLA_NOPE], q_pe], axis=1)
            o = _softmax_pv([_dot_t(q, k_lat) * scale, _dot_t(q, k_ctx) * scale], [v_lat, v_ctx])
            o_ref[rows, hh * MLA_V:(hh + 1) * MLA_V] = o.astype(o_ref.dtype)


def _lat_mla_attention(q3, kv3, kvc3, kpe3, kpe_ctx, seq0, n_lat, length, cos, sin):
    kw = ATTN_HEADS_PER_STEP * (MLA_NOPE + LANE)
    lc = kpe_ctx.shape[1]
    return pl.pallas_call(
        functools.partial(_lat_mla_kernel, length=length),
        grid=(n_lat, MLA_HEADS // ATTN_HEADS_PER_STEP),
        in_specs=[pl.BlockSpec((None, length, kw), lambda b, h: (seq0 + b, 0, h)),
                  pl.BlockSpec((None, length, kw), lambda b, h: (seq0 + b, 0, h)),
                  pl.BlockSpec((None, length, LANE), lambda b, h: (seq0 + b, 0, 0)),
                  pl.BlockSpec((None, lc, kw), lambda b, h: (b, 0, h)),
                  pl.BlockSpec((None, lc, LANE), lambda b, h: (b, 0, 0)),
                  pl.BlockSpec((length, LANE), lambda b, h: (0, 0)),
                  pl.BlockSpec((length, LANE), lambda b, h: (0, 0))],
        out_specs=pl.BlockSpec((None, length, ATTN_HEADS_PER_STEP * MLA_V), lambda b, h: (b, 0, h)),
        out_shape=jax.ShapeDtypeStruct((n_lat, length, MLA_HEADS * MLA_V), BF16),
        compiler_params=_params(("arbitrary", "arbitrary"), 48),
        name="latent_mla_attention",
    )(q3, kv3, kpe3, kvc3, kpe_ctx, cos, sin)


def _na_row_start(r, rows):
    kr = min(NA_ROWS, rows)
    return min(max(r - kr // 2, 0), rows - kr)


def _lat_na_kernel(q_ref, k_ref, v_ref, kc_ref, vc_ref, bias_ref, o_ref, *, length):
    scale = HEAD_DIM ** -0.5
    rows = length // GRID_W
    kr = min(NA_ROWS, rows)
    for hh in range(ATTN_HEADS_PER_STEP):
        cols = slice(hh * HEAD_DIM, (hh + 1) * HEAD_DIM)
        k = k_ref[hh].astype(BF16)
        v = v_ref[hh].astype(BF16)
        kc = kc_ref[hh].astype(BF16)
        vc = vc_ref[hh].astype(BF16)
        r = 0
        while r < rows:
            r_end = r + 1
            while r_end < rows and _na_row_start(r_end, rows) == _na_row_start(r, rows):
                r_end += 1
            r0 = _na_row_start(r, rows) * GRID_W
            nq = (r_end - r) * GRID_W
            q = q_ref[r * GRID_W:r_end * GRID_W, cols]
            bias = bias_ref[hh, r:r_end].reshape(nq, kr * GRID_W)
            s_nb = _dot_t(q, k[r0:r0 + kr * GRID_W]) * scale + bias
            s_ctx = _dot_t(q, kc) * scale
            o = _softmax_pv([s_nb, s_ctx], [v[r0:r0 + kr * GRID_W], vc])
            o_ref[r * GRID_W:r_end * GRID_W, cols] = o.astype(o_ref.dtype)
            r = r_end


def _na_bias(rpb, length):
    rows = length // GRID_W
    kr = min(NA_ROWS, rows)
    col = jnp.arange(GRID_W)
    c_start = jnp.clip(col - NA_COLS // 2, 0, GRID_W - NA_COLS)
    col_valid = (col[None, :] >= c_start[:, None]) & (col[None, :] < c_start[:, None] + NA_COLS)
    off_c = jnp.clip(col[None, :] - col[:, None], -(NA_COLS - 1), NA_COLS - 1) + NA_COLS - 1
    onehot = (off_c[:, :, None] == jnp.arange(2 * NA_COLS - 1)[None, None, :]).astype(F32)
    table = jnp.einsum('hdj,qkj->hdqk', rpb.astype(F32), onehot, precision=lax.Precision.HIGHEST)
    table = jnp.where(col_valid[None, None], table, NEG)
    per_row = []
    for r in range(rows):
        r0 = _na_row_start(r, rows)
        per_row.append(jnp.concatenate([table[:, r0 + j - r + NA_ROWS - 1] for j in range(kr)], axis=-1))
    return jnp.stack(per_row, axis=1)


def _lat_na_attention(qd3, seq0, k4, v4, k_ctx, v_ctx, bias):
    n_lat, _, length, _ = k4.shape
    rows = length // GRID_W
    kr = min(NA_ROWS, rows)
    lc = k_ctx.shape[2]
    hp = ATTN_HEADS_PER_STEP
    kv_spec = pl.BlockSpec((None, hp, length, HEAD_DIM), lambda h, b: (b, h, 0, 0))
    ctx_spec = pl.BlockSpec((None, hp, lc, HEAD_DIM), lambda h, b: (b, h, 0, 0))
    return pl.pallas_call(
        functools.partial(_lat_na_kernel, length=length),
        grid=(NA_HEADS // hp, n_lat),
        in_specs=[pl.BlockSpec((None, length, hp * HEAD_DIM), lambda h, b: (seq0 + b, 0, h)),
                  kv_spec, kv_spec, ctx_spec, ctx_spec,
                  pl.BlockSpec((hp, rows, GRID_W, kr * GRID_W), lambda h, b: (h, 0, 0, 0))],
        out_specs=pl.BlockSpec((None, length, hp * HEAD_DIM), lambda h, b: (b, 0, h)),
        out_shape=jax.ShapeDtypeStruct((n_lat, length, NA_WIDTH), BF16),
        compiler_params=_params(("arbitrary", "arbitrary"), 48),
        name="latent_neighborhood_attention",
    )(qd3, k4, v4, k_ctx, v_ctx, bias)


def kernel(x_prompt, x_sample, cache_l0_k, cache_l0_v, state_l0_re, state_l0_im, cache_l1_ckv, cache_l1_kpe, cache_l1_k, cache_l1_v, c, c_ctx, l0_ada_w, l0_ada_b, l0_norm1, l0_norm2, l0_w_in, l0_lambda_re, l0_lambda_im, l0_log_dt, l0_b_re, l0_b_im, l0_c_re, l0_c_im, l0_d_skip, l0_w_glu, l0_b_glu, l0_sink, l0_w_out, l0_ffn_w_up, l0_ffn_conv_w, l0_ffn_conv_b, l0_ffn_w_down, l1_ada_w, l1_ada_b, l1_norm1, l1_norm2, l1_w_in, l1_q_norm, l1_kv_norm, l1_w_uq, l1_w_ukv, l1_rpb, l1_w_out, l1_ffn_w_up, l1_ffn_conv_w, l1_ffn_conv_b, l1_ffn_w_down, final_norm):
    n_p, len_p, d = x_prompt.shape
    n_s, len_s, _ = x_sample.shape
    rows_p, rows_s = n_p * len_p, n_s * len_s
    t = rows_p + rows_s
    assert d == D_MODEL and len_s == ROW_TILE and ROW_TILE % len_p == 0 and n_s + 1 <= MAX_CONDS
    assert rows_p % ROW_TILE == 0
    assert len_p % S5_CHUNK == 0 and len_s % (2 * WIN_RADIUS) == 0 and len_s % GRID_W == 0
    seqs_p_units = t // len_p
    seqs_s_units = t // len_s
    seq0_s = rows_p // len_s
    groups = dict(n_prompt_rows=rows_p, lat_len=len_s)
    half = D_MODEL // 2

    cond = jnp.zeros((MAX_CONDS, d), F32).at[0].set(c_ctx).at[1:1 + n_s].set(c)
    mod0 = _modulation(cond, l0_ada_w, l0_ada_b)
    mod1 = _modulation(cond, l1_ada_w, l1_ada_b)

    x_parts = [x_prompt.reshape(rows_p, d), x_sample.reshape(rows_s, d)]

    def ffn(x, mod, norm2, w_up, conv_w, conv_b, w_down):
        h = _norm_mod([x], norm2, mod, 3, 4, **groups)
        act = _ffn_up(h, w_up, conv_w, conv_b, rows_p, len_p, len_s)
        return _matmul_residual([[(act, 0)]], w_down.astype(BF16), [x], mod, 5, 512, name="ffn_down",
                                bm=512, **groups)

    h = _norm_mod(x_parts, l0_norm1, mod0, 0, 1, **groups)
    u = _matmul(h, l0_w_in, F32, 512, "l0_in_proj_u", 0, S5_WIDTH)
    q0 = _matmul(h, l0_w_in, BF16, 512, "l0_in_proj_q", S5_WIDTH, WIN_Q_WIDTH)
    kcol = S5_WIDTH + WIN_Q_WIDTH
    vcol = kcol + WIN_KV_WIDTH
    out_k0 = _matmul_heads(h, l0_w_in, kcol, WIN_KV_HEADS, 0, rows_p, len_p, "l0_in_proj_k_ctx")
    out_v0 = _matmul_heads(h, l0_w_in, vcol, WIN_KV_HEADS, 0, rows_p, len_p, "l0_in_proj_v_ctx")
    k0_s = _matmul_heads(h, l0_w_in, kcol, WIN_KV_HEADS, rows_p, rows_s, len_s, "l0_in_proj_k_lat")
    v0_s = _matmul_heads(h, l0_w_in, vcol, WIN_KV_HEADS, rows_p, rows_s, len_s, "l0_in_proj_v_lat")

    assert rows_p % rows_s == 0
    sel = _s5_lane_permutation()
    sb, t_op, e_op, a16 = _s5_operators(l0_lambda_re, l0_lambda_im, l0_log_dt, l0_b_re, l0_b_im, l0_c_re, l0_c_im, l0_d_skip)
    nblk, gps, pw = S5_WIDTH // LANE, GROUPS_PER_STEP, 2 * S5_STATE

    def planes(cols):
        b = cols.shape[1]
        c5 = cols.reshape(nblk, gps, b, 2, pw).transpose(0, 2, 3, 1, 4)
        return c5.reshape(nblk, b, 2 * STATE_W)

    def state_cols(s):
        return s.astype(F32).transpose(2, 0, 1, 3).reshape(S5_GROUPS, s.shape[0], pw)

    np8, ns8 = _s5_chunk_rows(n_p), _s5_chunk_rows(n_s)
    a_planes = planes(a16)
    h0_lat = planes(jnp.concatenate([state_cols(state_l0_re), state_cols(state_l0_im)], axis=-1))
    h0_s = jnp.pad(h0_lat, ((0, 0), (0, ns8 - n_s), (0, 0)))
    h0_p = jnp.zeros((nblk, np8, 2 * STATE_W), F32)
    y_p, hfin = _s5_stream(u, 0, n_p, len_p, sel, sb, t_op, e_op, a_planes, h0_p)
    y_s, _ = _s5_stream(u, rows_p // rows_s, n_s, len_s, sel, sb, t_op, e_op, a_planes, h0_s)
    a_out = _glu([y_p, y_s], l0_w_glu, l0_b_glu, rows_p)

    def state_out(plane):
        return (plane[:, :n_p].reshape(nblk, n_p, gps, 2, S5_STATE).transpose(1, 3, 0, 2, 4)
                .reshape(n_p, 2, S5_GROUPS, S5_STATE))

    out_sre = state_out(hfin[:, :, :STATE_W])
    out_sim = state_out(hfin[:, :, STATE_W:])

    cos_e, sin_e = _rope_tables(len_s, HEAD_DIM)
    o_p = _ctx_gqa_attention(q0.reshape(seqs_p_units, len_p, -1), out_k0, out_v0, l0_sink, n_p, len_p)
    o_s = _lat_window_attention(q0.reshape(seqs_s_units, len_s, -1), seq0_s, k0_s, v0_s,
                                cache_l0_k, cache_l0_v, l0_sink, cos_e, sin_e)
    x = _matmul_residual([[(a_out, 0)], [(o_p.reshape(rows_p, half), 0), (o_s.reshape(rows_s, half), 0)]],
                         l0_w_out.astype(BF16), x_parts, mod0, 2, 512, name="l0_out_proj", **groups)
    x = ffn(x, mod0, l0_norm2, l0_ffn_w_up, l0_ffn_conv_w, l0_ffn_conv_b, l0_ffn_w_down)

    s1, s2 = MLA_Q_LORA + MLA_KV_LORA, MLA_Q_LORA + MLA_KV_LORA + MLA_ROPE
    w_lora = l1_w_in[:, :s1].astype(BF16)
    w_dkv = l1_w_in[:, s2:].astype(BF16)
    w_kpe = jnp.pad(l1_w_in[:, s1:s2], ((0, 0), (0, LANE - MLA_ROPE))).astype(BF16)
    w_uq = l1_w_uq.reshape(MLA_Q_LORA, MLA_HEADS, MLA_NOPE + MLA_ROPE)
    w_uq = jnp.pad(w_uq, ((0, 0), (0, 0), (0, LANE - MLA_ROPE))).reshape(MLA_Q_LORA, -1).astype(BF16)

    h = _norm_mod([x], l1_norm1, mod1, 0, 1, **groups)
    cqkv = _matmul(h, w_lora, F32, 512, "l1_in_proj_lora")
    kpe = _matmul(h, w_kpe, F32, LANE, "l1_in_proj_kpe")
    qd = _matmul(h, w_dkv, BF16, 512, "l1_in_proj_qd", 0, NA_WIDTH)
    out_k1 = _matmul_heads(h, w_dkv, NA_WIDTH, NA_HEADS, 0, rows_p, len_p, "l1_in_proj_kd_ctx")
    out_v1 = _matmul_heads(h, w_dkv, 2 * NA_WIDTH, NA_HEADS, 0, rows_p, len_p, "l1_in_proj_vd_ctx")
    k1_s = _matmul_heads(h, w_dkv, NA_WIDTH, NA_HEADS, rows_p, rows_s, len_s, "l1_in_proj_kd_lat")
    v1_s = _matmul_heads(h, w_dkv, 2 * NA_WIDTH, NA_HEADS, rows_p, rows_s, len_s, "l1_in_proj_vd_lat")
    cqn = _rmsnorm_cols(cqkv, l1_q_norm, 0, MLA_Q_LORA, BF16)
    ckvn = _rmsnorm_cols(cqkv, l1_kv_norm, MLA_Q_LORA // MLA_KV_LORA, MLA_KV_LORA, F32)
    q_all = _matmul(cqn, w_uq, BF16, 2048, "l1_q_up")
    w_ukv = l1_w_ukv.astype(BF16)
    kv_all = _matmul(ckvn, w_ukv, BF16, 2048, "l1_kv_up")
    lc = cache_l1_ckv.shape[1]
    kv_ctx = _matmul(cache_l1_ckv.reshape(-1, MLA_KV_LORA), w_ukv, BF16, 2048, "l1_kv_up_ctx")

    out_ckv = ckvn[:rows_p].reshape(n_p, len_p, MLA_KV_LORA)
    out_kpe = kpe[:rows_p, :MLA_ROPE].reshape(n_p, len_p, MLA_ROPE)

    o_p = _ctx_odd_attention(q_all.reshape(seqs_p_units, len_p, -1),
                             kv_all.reshape(seqs_p_units, len_p, -1),
                             kpe.reshape(seqs_p_units, len_p, LANE),
                             qd.reshape(seqs_p_units, len_p, -1), out_k1, out_v1, n_p, len_p)
    cos_o, sin_o = _rope_tables(len_s, MLA_ROPE)
    kpe_ctx = jnp.pad(cache_l1_kpe, ((0, 0), (0, 0), (0, LANE - MLA_ROPE)))
    oc_s = _lat_mla_attention(q_all.reshape(seqs_s_units, len_s, -1),
                              kv_all.reshape(seqs_s_units, len_s, -1),
                              kv_ctx.reshape(n_s, lc, -1),
                              kpe.reshape(seqs_s_units, len_s, LANE), kpe_ctx,
                              seq0_s, n_s, len_s, cos_o, sin_o)
    od_s = _lat_na_attention(qd.reshape(seqs_s_units, len_s, -1), seq0_s, k1_s, v1_s,
                             cache_l1_k, cache_l1_v, _na_bias(l1_rpb, len_s))
    o_p2 = o_p.reshape(rows_p, 2 * half)
    x = _matmul_residual([[(o_p2, 0), (oc_s.reshape(rows_s, half), 0)],
                          [(o_p2, 1), (od_s.reshape(rows_s, half), 0)]],
                         l1_w_out.astype(BF16), [x], mod1, 2, 512, name="l1_out_proj", **groups)
    x = ffn(x, mod1, l1_norm2, l1_ffn_w_up, l1_ffn_conv_w, l1_ffn_conv_b, l1_ffn_w_down)

    y_prompt = _rmsnorm_cols(x, final_norm, 0, d, F32, 0, rows_p).reshape(n_p, len_p, d)
    y_sample = _rmsnorm_cols(x, final_norm, 0, d, F32, rows_p, rows_s).reshape(n_s, len_s, d)
    return (y_prompt, y_sample, out_k0, out_v0, out_sre, out_sim, out_ckv, out_kpe, out_k1, out_v1)
```

```python
import functools
import math

import jax
import jax.numpy as jnp
from jax import lax
from jax.experimental import pallas as pl
from jax.experimental.pallas import tpu as pltpu

F32 = jnp.float32
BF16 = jnp.bfloat16

D_MODEL = 4096
GRID_W = 64
HEAD_DIM = 128
S5_WIDTH = 2048
S5_GROUP = 16
S5_GROUPS = S5_WIDTH // S5_GROUP
S5_STATE = 64
S5_CHUNK = 16
WIN_HEADS = 16
WIN_KV_HEADS = 4
WIN_GROUP = WIN_HEADS // WIN_KV_HEADS
WIN_RADIUS = 128
WIN_Q_WIDTH = WIN_HEADS * HEAD_DIM
WIN_KV_WIDTH = WIN_KV_HEADS * HEAD_DIM
MLA_HEADS = 16
MLA_Q_LORA = 1024
MLA_KV_LORA = 512
MLA_NOPE = 128
MLA_ROPE = 64
MLA_V = 128
NA_HEADS = 16
NA_ROWS = 8
NA_COLS = 16
NA_WIDTH = NA_HEADS * HEAD_DIM
D_FF = 11008
ROPE_BASE = 10000.0
EPS = 1e-6
NEG = -1e30

LANE = 128
SUBLANE = 8
MAX_CONDS = 8
ROW_TILE = 1024
ATTN_HEADS_PER_STEP = 2


def _params(sem, vmem_mb):
    return pltpu.CompilerParams(dimension_semantics=sem, vmem_limit_bytes=vmem_mb * 1024 * 1024)


def _cond_index(i, bm, n_prompt_rows, lat_len):
    first = n_prompt_rows // bm
    per = lat_len // bm
    return jnp.where(i < first, 0, 1 + (i - first) // per)


def _dot(a, b):
    return jnp.dot(a, b, preferred_element_type=F32)


def _dot_t(a, b):
    return lax.dot_general(a, b, (((1,), (1,)), ((), ())), preferred_element_type=F32)


def _sigmoid(x):
    return 1.0 / (1.0 + jnp.exp(-x))


def _row_specs(parts, bm, width, col_fn, n_prompt_blocks, single_buffer=False):
    mode = dict(pipeline_mode=pl.Buffered(1)) if single_buffer else {}
    if len(parts) == 1:
        cb = parts[0][1]
        return [pl.BlockSpec((bm, width), lambda i, j: (i, col_fn(j, cb)), **mode)]
    cb_p, cb_s = parts[0][1], parts[1][1]
    last_p = n_prompt_blocks - 1
    return [pl.BlockSpec((bm, width), lambda i, j: (jnp.minimum(i, last_p), col_fn(j, cb_p)), **mode),
            pl.BlockSpec((bm, width), lambda i, j: (jnp.maximum(i - n_prompt_blocks, 0), col_fn(j, cb_s)), **mode)]


def _by_stream(i, n_prompt_blocks, operands, body):
    if all(len(o) == 1 for o in operands):
        body([o[0] for o in operands])
        return

    @pl.when(i < n_prompt_blocks)
    def _():
        body([o[0] for o in operands])

    @pl.when(i >= n_prompt_blocks)
    def _():
        body([o[-1] for o in operands])


def _mod_kernel(c_ref, w_ref, b_ref, o_ref):
    c = c_ref[...]
    s = c * _sigmoid(c)
    o_ref[...] = _dot(s.astype(BF16), w_ref[...].astype(BF16)) + b_ref[...]


def _modulation(cond, w, b):
    d, n = w.shape
    bn = 512
    out = pl.pallas_call(
        _mod_kernel,
        grid=(n // bn,),
        in_specs=[pl.BlockSpec((MAX_CONDS, d), lambda j: (0, 0)),
                  pl.BlockSpec((d, bn), lambda j: (0, j)),
                  pl.BlockSpec((1, bn), lambda j: (0, j))],
        out_specs=pl.BlockSpec((MAX_CONDS, bn), lambda j: (0, j)),
        out_shape=jax.ShapeDtypeStruct((MAX_CONDS, n), F32),
        compiler_params=_params(("arbitrary",), 40),
        name="modulation",
    )(cond, w, b.reshape(1, n))
    return out.reshape(MAX_CONDS, 1, n)


def _norm_mod_kernel(*refs, n_x, n_prompt_blocks):
    x_refs, (g_ref, sh_ref, sc_ref, o_ref) = refs[:n_x], refs[n_x:]

    def body(r):
        x = r[0][...]
        ms = jnp.mean(x * x, axis=-1, keepdims=True)
        y = x * lax.rsqrt(ms + EPS) * g_ref[...]
        o_ref[...] = (y * (1.0 + sc_ref[...]) + sh_ref[...]).astype(o_ref.dtype)

    _by_stream(pl.program_id(0), n_prompt_blocks, [x_refs], body)


def _norm_mod(x_parts, gain, mod, shift_slot, scale_slot, n_prompt_rows, lat_len):
    d = D_MODEL
    bm = 256
    t = sum(x.shape[0] for x in x_parts)
    npb = n_prompt_rows // bm
    cidx = functools.partial(_cond_index, bm=bm, n_prompt_rows=n_prompt_rows, lat_len=lat_len)
    specs = _row_specs([(x, 0) for x in x_parts], bm, d, lambda j, cb: 0, npb)
    to1d = lambda spec: pl.BlockSpec(spec.block_shape, lambda i, f=spec.index_map: f(i, 0))
    return pl.pallas_call(
        functools.partial(_norm_mod_kernel, n_x=len(x_parts), n_prompt_blocks=npb),
        grid=(t // bm,),
        in_specs=[to1d(s) for s in specs] + [
            pl.BlockSpec((1, d), lambda i: (0, 0)),
            pl.BlockSpec((None, 1, d), lambda i: (cidx(i), 0, shift_slot)),
            pl.BlockSpec((None, 1, d), lambda i: (cidx(i), 0, scale_slot))],
        out_specs=pl.BlockSpec((bm, d), lambda i: (i, 0)),
        out_shape=jax.ShapeDtypeStruct((t, d), BF16),
        compiler_params=_params(("arbitrary",), 40),
        name="norm_mod",
    )(*x_parts, gain.reshape(1, d), mod, mod)


def _rmsnorm_kernel(x_ref, g_ref, o_ref):
    x = x_ref[...].astype(F32)
    ms = jnp.mean(x * x, axis=-1, keepdims=True)
    o_ref[...] = (x * lax.rsqrt(ms + EPS) * g_ref[...]).astype(o_ref.dtype)


def _rmsnorm_cols(x, gain, col_block, width, out_dtype, row0=0, rows=None):
    bm = 256
    rows = x.shape[0] if rows is None else rows
    rb0 = row0 // bm
    return pl.pallas_call(
        _rmsnorm_kernel,
        grid=(rows // bm,),
        in_specs=[pl.BlockSpec((bm, width), lambda i: (i + rb0, col_block)),
                  pl.BlockSpec((1, width), lambda i: (0, 0))],
        out_specs=pl.BlockSpec((bm, width), lambda i: (i, 0)),
        out_shape=jax.ShapeDtypeStruct((rows, width), out_dtype),
        compiler_params=_params(("arbitrary",), 40),
        name="rmsnorm",
    )(x, gain.reshape(1, width))


def _mm_kernel(x_ref, w_ref, o_ref):
    o_ref[...] = _dot(x_ref[...].astype(BF16), w_ref[...].astype(BF16)).astype(o_ref.dtype)


def _matmul(x, w, out_dtype, bn, name, col0=0, ncols=None):
    m, k = x.shape
    ncols = w.shape[1] - col0 if ncols is None else ncols
    bm = min(ROW_TILE, m)
    assert m % bm == 0 and ncols % bn == 0 and col0 % bn == 0
    cb0 = col0 // bn
    return pl.pallas_call(
        _mm_kernel,
        grid=(m // bm, ncols // bn),
        in_specs=[pl.BlockSpec((bm, k), lambda i, j: (i, 0)),
                  pl.BlockSpec((k, bn), lambda i, j: (0, cb0 + j))],
        out_specs=pl.BlockSpec((bm, bn), lambda i, j: (i, j)),
        out_shape=jax.ShapeDtypeStruct((m, ncols), out_dtype),
        compiler_params=_params(("arbitrary", "arbitrary"), 56),
        name=name,
    )(x, w)


def _mm_heads_kernel(x_ref, w_ref, o_ref, *, seqs, seq_len, heads):
    acc = _dot(x_ref[...], w_ref[...].astype(BF16))
    for b in range(seqs):
        for hh in range(heads):
            o_ref[b, hh] = acc[b * seq_len:(b + 1) * seq_len, hh * HEAD_DIM:(hh + 1) * HEAD_DIM]


def _matmul_heads(x, w, col0, heads, row0, rows, seq_len, name):
    k = x.shape[1]
    bm = ROW_TILE
    hb = 4
    bn = hb * HEAD_DIM
    assert rows % bm == 0 and row0 % bm == 0 and bm % seq_len == 0 and heads % hb == 0 and col0 % bn == 0
    seqs = bm // seq_len
    rb0, cb0 = row0 // bm, col0 // bn
    return pl.pallas_call(
        functools.partial(_mm_heads_kernel, seqs=seqs, seq_len=seq_len, heads=hb),
        grid=(rows // bm, heads // hb),
        in_specs=[pl.BlockSpec((bm, k), lambda i, j: (rb0 + i, 0)),
                  pl.BlockSpec((k, bn), lambda i, j: (0, cb0 + j))],
        out_specs=pl.BlockSpec((seqs, hb, seq_len, HEAD_DIM), lambda i, j: (i, j, 0, 0)),
        out_shape=jax.ShapeDtypeStruct((rows // seq_len, heads, seq_len, HEAD_DIM), F32),
        compiler_params=_params(("arbitrary", "arbitrary"), 56),
        name=name,
    )(x, w)


def _mm_res_kernel(*refs, n_lhs, n_parts, n_res, n_prompt_blocks):
    pos = 0
    lhs = []
    for n in n_lhs:
        lhs.append(refs[pos:pos + n])
        pos += n
    w_refs = refs[pos:pos + n_parts]
    pos += n_parts
    res = refs[pos:pos + n_res]
    pos += n_res
    gate_ref, o_ref = refs[pos], refs[pos + 1]

    def body(r):
        acc = None
        for x_ref, w_ref in zip(r[:n_parts], w_refs):
            part = _dot(x_ref[...], w_ref[...].astype(BF16))
            acc = part if acc is None else acc + part
        o_ref[...] = r[n_parts][...] + gate_ref[...] * acc

    _by_stream(pl.program_id(0), n_prompt_blocks, lhs + [res], body)


def _matmul_residual(lhs_parts, w, res_parts, mod, gate_slot, bn, n_prompt_rows, lat_len, name,
                     single_buffer_x=False, vmem_mb=56, bm=ROW_TILE):
    n = w.shape[1]
    kq = w.shape[0] // len(lhs_parts)
    t = sum(r.shape[0] for r in res_parts)
    npb = n_prompt_rows // bm
    cidx = functools.partial(_cond_index, bm=bm, n_prompt_rows=n_prompt_rows, lat_len=lat_len)
    gate_col0 = gate_slot * (D_MODEL // bn)
    in_specs, args = [], []
    for parts in lhs_parts:
        in_specs += _row_specs(parts, bm, kq, lambda j, cb: cb, npb, single_buffer=single_buffer_x)
        args += [a for a, _ in parts]
    for q in range(len(lhs_parts)):
        in_specs.append(pl.BlockSpec((kq, bn), lambda i, j, q=q: (q, j)))
        args.append(w)
    in_specs += _row_specs([(r, 0) for r in res_parts], bm, bn, lambda j, cb: j, npb)
    args += list(res_parts)
    in_specs.append(pl.BlockSpec((None, 1, bn), lambda i, j: (cidx(i), 0, gate_col0 + j)))
    args.append(mod)
    body = functools.partial(_mm_res_kernel, n_lhs=tuple(len(p) for p in lhs_parts), n_parts=len(lhs_parts),
                             n_res=len(res_parts), n_prompt_blocks=npb)
    return pl.pallas_call(
        body,
        grid=(t // bm, n // bn),
        in_specs=in_specs,
        out_specs=pl.BlockSpec((bm, bn), lambda i, j: (i, j)),
        out_shape=jax.ShapeDtypeStruct((t, n), F32),
        compiler_params=_params(("arbitrary", "arbitrary"), vmem_mb),
        name=name,
    )(*args)


def _glu_kernel(*refs, n_y, n_prompt_blocks):
    y_refs = refs[:n_y]
    w_ref, b_ref, o_ref = refs[n_y:]

    def body(r):
        y = r[0][...]
        z = _dot(y.astype(BF16), w_ref[...]) + b_ref[...]
        o_ref[...] = (y * _sigmoid(z)).astype(o_ref.dtype)

    _by_stream(pl.program_id(0), n_prompt_blocks, [y_refs], body)


def _glu(y_parts, w, b, n_prompt_rows):
    k, n = w.shape
    bm = ROW_TILE
    m = sum(y.shape[0] for y in y_parts)
    npb = n_prompt_rows // bm
    parts = [(y, 0) for y in y_parts]
    return pl.pallas_call(
        functools.partial(_glu_kernel, n_y=len(y_parts), n_prompt_blocks=npb),
        grid=(m // bm, 1),
        in_specs=(_row_specs(parts, bm, k, lambda j, cb: 0, npb)
                  + [pl.BlockSpec((k, n), lambda i, j: (0, 0), pipeline_mode=pl.Buffered(1)),
                     pl.BlockSpec((1, n), lambda i, j: (0, 0))]),
        out_specs=pl.BlockSpec((bm, n), lambda i, j: (i, 0)),
        out_shape=jax.ShapeDtypeStruct((m, n), BF16),
        compiler_params=_params(("arbitrary", "arbitrary"), 56),
        name="s5_glu",
    )(*y_parts, w.astype(BF16), b.reshape(1, n))


def _ffn_up_kernel(x_ref, wg_ref, wv_ref, cwg_ref, cwv_ref, cbg_ref, cbv_ref, o_ref, *,
                   sub, n_sub, prompt_blocks, prompt_len, lat_len):
    i = pl.program_id(0)
    row = lax.broadcasted_iota(jnp.int32, (sub, 1), 0)
    wg = wg_ref[...].astype(BF16)
    wv = wv_ref[...].astype(BF16)

    for s in range(n_sub):
        period = jnp.where(i * n_sub + s < prompt_blocks, prompt_len, lat_len)
        pos = row & (period - 1)
        first = pos == 0
        last = pos == period - 1

        def conv(u, cw_ref, cb_ref):
            prev = jnp.where(first, 0.0, pltpu.roll(u, 1, 0))
            nxt = jnp.where(last, 0.0, pltpu.roll(u, sub - 1, 0))
            return prev * cw_ref[0:1, :] + u * cw_ref[1:2, :] + nxt * cw_ref[2:3, :] + cb_ref[...]

        x = x_ref[s * sub:(s + 1) * sub, :]
        g = conv(_dot(x, wg), cwg_ref, cbg_ref)
        v = conv(_dot(x, wv), cwv_ref, cbv_ref)
        o_ref[s * sub:(s + 1) * sub, :] = (g * _sigmoid(g) * v).astype(o_ref.dtype)


FFN_SUB_BLOCKS = 4


def _ffn_up(h, w_up, conv_w, conv_b, n_prompt_rows, prompt_len, lat_len):
    t, d = h.shape
    sub, bn = lat_len, 256
    n_sub = FFN_SUB_BLOCKS if (t // sub) % FFN_SUB_BLOCKS == 0 else 2
    bm = sub * n_sub
    assert t % bm == 0 and n_prompt_rows % sub == 0 and sub % prompt_len == 0
    assert prompt_len & (prompt_len - 1) == 0 and lat_len & (lat_len - 1) == 0
    nj = D_FF // bn
    conv_b = conv_b.reshape(1, 2 * D_FF)
    body = functools.partial(_ffn_up_kernel, sub=sub, n_sub=n_sub, prompt_blocks=n_prompt_rows // sub,
                             prompt_len=prompt_len, lat_len=lat_len)
    return pl.pallas_call(
        body,
        grid=(t // bm, nj),
        in_specs=[pl.BlockSpec((bm, d), lambda i, j: (i, 0), pipeline_mode=pl.Buffered(1)),
                  pl.BlockSpec((d, bn), lambda i, j: (0, j)),
                  pl.BlockSpec((d, bn), lambda i, j: (0, nj + j)),
                  pl.BlockSpec((3, bn), lambda i, j: (0, j)),
                  pl.BlockSpec((3, bn), lambda i, j: (0, nj + j)),
                  pl.BlockSpec((1, bn), lambda i, j: (0, j)),
                  pl.BlockSpec((1, bn), lambda i, j: (0, nj + j))],
        out_specs=pl.BlockSpec((bm, bn), lambda i, j: (i, j)),
        out_shape=jax.ShapeDtypeStruct((t, D_FF), BF16),
        compiler_params=_params(("arbitrary", "arbitrary"), 60),
        name="ffn_up_conv",
    )(h, w_up, w_up, conv_w, conv_w, conv_b, conv_b)


def _s5_operators(lam_re, lam_im, log_dt, b_re, b_im, c_re, c_im, d_skip):
    q, g, p, c = S5_CHUNK, S5_GROUPS, S5_STATE, S5_GROUP
    dt = jnp.exp(log_dt)[None, :, :, None]
    tau = jnp.arange(q + 1, dtype=F32)[:, None, None, None]
    mag = jnp.exp(lam_re[None] * dt * tau)
    ang = lam_im[None] * dt * tau
    pw_re, pw_im = mag * jnp.cos(ang), mag * jnp.sin(ang)
    dt1 = jnp.exp(log_dt)[:, :, None]
    m1 = jnp.exp(lam_re * dt1)
    ab_re, ab_im = m1 * jnp.cos(lam_im * dt1), m1 * jnp.sin(lam_im * dt1)
    den = lam_re * lam_re + lam_im * lam_im
    nr, ni = ab_re - 1.0, ab_im
    f_re = ((nr * lam_re + ni * lam_im) / den)[..., None]
    f_im = ((ni * lam_re - nr * lam_im) / den)[..., None]
    bb_re = f_re * b_re - f_im * b_im
    bb_im = f_re * b_im + f_im * b_re
    ct_re, ct_im = jnp.swapaxes(c_re, -1, -2), jnp.swapaxes(c_im, -1, -2)

    def power_times_c(fwd_taus, bwd_taus):
        pr = jnp.stack([pw_re[fwd_taus, 0], pw_re[bwd_taus, 1]], axis=0).transpose(0, 2, 3, 1)
        pi = jnp.stack([pw_im[fwd_taus, 0], pw_im[bwd_taus, 1]], axis=0).transpose(0, 2, 3, 1)
        r_re = pr[..., None] * ct_re[:, :, :, None, :] - pi[..., None] * ct_im[:, :, :, None, :]
        r_im = pr[..., None] * ct_im[:, :, :, None, :] + pi[..., None] * ct_re[:, :, :, None, :]
        return r_re.reshape(2, g, p, q * c), r_im.reshape(2, g, p, q * c)

    lags = jnp.arange(q)
    rk_re, rk_im = power_times_c(lags, lags[::-1])
    diag = jnp.tile(d_skip[:, :, None] * jnp.eye(c, dtype=F32)[None], (1, 1, q))
    bbt_re, bbt_im = jnp.swapaxes(bb_re, -1, -2), jnp.swapaxes(bb_im, -1, -2)
    t_op = _s5_toeplitz(rk_re, rk_im, bbt_re, bbt_im, diag)
    re_re, re_im = power_times_c(lags + 1, q - lags)
    zero = jnp.zeros_like(re_re[0])
    e_op = jnp.concatenate([re_re[0], zero, -re_im[0], zero, zero, re_re[1], zero, -re_im[1]], axis=1)
    def s_op(taus, d):
        pr, pi = pw_re[taus, d][:, :, None, :], pw_im[taus, d][:, :, None, :]
        br, bi = bbt_re[d][None], bbt_im[d][None]
        s_re = (pr * br - pi * bi).transpose(1, 0, 2, 3).reshape(g, q * c, p)
        s_im = (pr * bi + pi * br).transpose(1, 0, 2, 3).reshape(g, q * c, p)
        return s_re, s_im

    sf_re, sf_im = s_op(lags[::-1], 0)
    sb_re, sb_im = s_op(lags, 1)
    sb = jnp.concatenate([sf_re, sb_re, sf_im, sb_im], axis=2)
    a16 = jnp.concatenate([pw_re[q, 0], pw_re[q, 1], pw_im[q, 0], pw_im[q, 1]], axis=-1)[:, None, :]
    return sb.astype(BF16), t_op, e_op.astype(BF16), a16


def _s5_toeplitz_kernel(rr_ref, ri_ref, br_ref, bi_ref, dg_ref, o_ref):
    c, q = S5_GROUP, S5_CHUNK

    def lag_kernels(d):
        hp = lax.Precision.HIGHEST
        return (jnp.dot(br_ref[d], rr_ref[d], precision=hp, preferred_element_type=F32)
                - jnp.dot(bi_ref[d], ri_ref[d], precision=hp, preferred_element_type=F32))

    kf = lag_kernels(0)
    kb = lag_kernels(1)
    lane = lax.broadcasted_iota(jnp.int32, (c, q * c), 1)
    dg = dg_ref[...]
    for s in range(q):
        f = kf if s == 0 else pltpu.roll(kf, c * s, 1)
        b = kb if s == q - 1 else pltpu.roll(kb, q * c - c * (q - 1 - s), 1)
        piece = jnp.where(lane >= c * s, f, 0.0) + jnp.where(lane < c * (s + 1), b, 0.0)
        piece = piece + jnp.where(jnp.logical_and(lane >= c * s, lane < c * (s + 1)), dg, 0.0)
        o_ref[s * c:(s + 1) * c, :] = piece.astype(o_ref.dtype)


def _s5_toeplitz(rk_re, rk_im, bbt_re, bbt_im, diag):
    g, p, c, qc = S5_GROUPS, S5_STATE, S5_GROUP, S5_CHUNK * S5_GROUP
    r_spec = pl.BlockSpec((2, None, p, qc), lambda i: (0, i, 0, 0))
    b_spec = pl.BlockSpec((2, None, c, p), lambda i: (0, i, 0, 0))
    return pl.pallas_call(
        _s5_toeplitz_kernel,
        grid=(g,),
        in_specs=[r_spec, r_spec, b_spec, b_spec, pl.BlockSpec((None, c, qc), lambda i: (i, 0, 0))],
        out_specs=pl.BlockSpec((None, qc, qc), lambda i: (i, 0, 0)),
        out_shape=jax.ShapeDtypeStruct((g, qc, qc), BF16),
        compiler_params=_params(("arbitrary",), 32),
        name="s5_toeplitz",
    )(rk_re, rk_im, bbt_re, bbt_im, diag)


def _gelu_tanh(x):
    return 0.5 * x * (1.0 + jnp.tanh(math.sqrt(2.0 / math.pi) * (x + 0.044715 * (x * x * x))))


GROUPS_PER_STEP = LANE // S5_GROUP
STATE_W = GROUPS_PER_STEP * 2 * S5_STATE


def _s5_lane_permutation():
    j = jnp.arange(S5_CHUNK * LANE)
    s, gl, c = j // LANE, (j % LANE) // S5_GROUP, j % S5_GROUP
    k = gl * (S5_CHUNK * S5_GROUP) + s * S5_GROUP + c
    return (k[:, None] == jnp.arange(S5_CHUNK * LANE)[None, :]).astype(BF16)


def _s5_kernel(u_ref, sel_ref, sb_ref, t_ref, e_ref, a_ref, h0_ref, y_ref, hfin_ref, xg, w_scr, h_scr, *,
               batch, nb, nc, seq_len):
    qc = S5_CHUNK * S5_GROUP
    if batch < nb:
        xg[...] = jnp.zeros(xg.shape, BF16)
    for ch in range(nc):
        for s in range(S5_CHUNK):
            xg[ch * nb:ch * nb + batch, s * LANE:(s + 1) * LANE] = (
                u_ref[pl.ds(ch * S5_CHUNK + s, batch, stride=seq_len), :].astype(BF16))
    sel = sel_ref[...]
    ucat = _dot(xg[...], sel).astype(BF16)
    for gl in range(GROUPS_PER_STEP):
        w = _dot(ucat[:, gl * qc:(gl + 1) * qc], sb_ref[gl])
        w_scr[:, gl * LANE:(gl + 1) * LANE] = w[:, 0:LANE]
        w_scr[:, STATE_W + gl * LANE:STATE_W + (gl + 1) * LANE] = w[:, LANE:2 * LANE]
    ar = a_ref[0:1, 0:STATE_W]
    ai = a_ref[0:1, STATE_W:2 * STATE_W]
    fwd_lane = (lax.broadcasted_iota(jnp.int32, (1, STATE_W), 1) & S5_STATE) == 0
    hr = h0_ref[:, 0:STATE_W]
    hi = h0_ref[:, STATE_W:2 * STATE_W]
    for i in range(nc):
        ri = i * nb
        rj = (nc - 1 - i) * nb
        h_scr[ri:ri + nb, 0:STATE_W] = hr
        h_scr[ri:ri + nb, STATE_W:2 * STATE_W] = hi
        h_scr[rj:rj + nb, 2 * STATE_W:3 * STATE_W] = hr
        h_scr[rj:rj + nb, 3 * STATE_W:4 * STATE_W] = hi
        wr = jnp.where(fwd_lane, w_scr[ri:ri + nb, 0:STATE_W], w_scr[rj:rj + nb, 0:STATE_W])
        wi = jnp.where(fwd_lane, w_scr[ri:ri + nb, STATE_W:2 * STATE_W], w_scr[rj:rj + nb, STATE_W:2 * STATE_W])
        hr, hi = ar * hr - ai * hi + wr, ar * hi + ai * hr + wi
    hfin_ref[:, 0:STATE_W] = hr
    hfin_ref[:, STATE_W:2 * STATE_W] = hi
    for gl in range(GROUPS_PER_STEP):
        hcat = jnp.concatenate([h_scr[:, k * STATE_W + gl * LANE:k * STATE_W + (gl + 1) * LANE] for k in range(4)],
                               axis=1).astype(BF16)
        y = _dot(ucat[:, gl * qc:(gl + 1) * qc], t_ref[gl]) + _dot(hcat, e_ref[gl])
        xg[:, gl * qc:(gl + 1) * qc] = _gelu_tanh(y).astype(BF16)
    yp = _dot_t(xg[...], sel)
    for ch in range(nc):
        for s in range(S5_CHUNK):
            y_ref[pl.ds(ch * S5_CHUNK + s, batch, stride=seq_len), :] = (
                yp[ch * nb:ch * nb + batch, s * LANE:(s + 1) * LANE])


def _s5_chunk_rows(batch):
    return batch if batch % (SUBLANE // 2) == 0 else -(-batch // SUBLANE) * SUBLANE


def _s5_stream(u, row_block, batch, seq_len, sel, sb, t_op, e_op, a_planes, h0):
    n_rows = batch * seq_len
    nb = _s5_chunk_rows(batch)
    nc = seq_len // S5_CHUNK
    rows = nb * nc
    nblk = S5_WIDTH // LANE
    qc = S5_CHUNK * S5_GROUP
    body = functools.partial(_s5_kernel, batch=batch, nb=nb, nc=nc, seq_len=seq_len)
    return pl.pallas_call(
        body,
        grid=(nblk,),
        in_specs=[pl.BlockSpec((n_rows, LANE), lambda g: (row_block, g)),
                  pl.BlockSpec((S5_CHUNK * LANE, S5_CHUNK * LANE), lambda g: (0, 0), pipeline_mode=pl.Buffered(1)),
                  pl.BlockSpec((GROUPS_PER_STEP, qc, 4 * S5_STATE), lambda g: (g, 0, 0)),
                  pl.BlockSpec((GROUPS_PER_STEP, qc, qc), lambda g: (g, 0, 0)),
                  pl.BlockSpec((GROUPS_PER_STEP, 8 * S5_STATE, qc), lambda g: (g, 0, 0)),
                  pl.BlockSpec((None, 1, 2 * STATE_W), lambda g: (g, 0, 0)),
                  pl.BlockSpec((None, nb, 2 * STATE_W), lambda g: (g, 0, 0))],
        out_specs=[pl.BlockSpec((n_rows, LANE), lambda g: (0, g)),
                   pl.BlockSpec((None, nb, 2 * STATE_W), lambda g: (g, 0, 0))],
        out_shape=[jax.ShapeDtypeStruct((n_rows, S5_WIDTH), F32),
                   jax.ShapeDtypeStruct((nblk, nb, 2 * STATE_W), F32)],
        scratch_shapes=[pltpu.VMEM((rows, S5_CHUNK * LANE), BF16),
                        pltpu.VMEM((rows, 2 * STATE_W), F32),
                        pltpu.VMEM((rows, 4 * STATE_W), F32)],
        compiler_params=_params(("arbitrary",), 56),
        name="s5_chunked",
    )(u, sel, sb, t_op, e_op, a_planes, h0)


def _rope_tables(length, rot):
    n_freq = rot // 4
    t = jnp.arange(length)
    row = (t // GRID_W).astype(F32)
    col = (t % GRID_W).astype(F32)
    inv = ROPE_BASE ** (-jnp.arange(n_freq, dtype=F32) / n_freq)
    ar, ac = row[:, None] * inv, col[:, None] * inv
    cos = jnp.concatenate([jnp.cos(ar), jnp.cos(ar), jnp.cos(ac), jnp.cos(ac)], axis=-1)
    sin = jnp.concatenate([-jnp.sin(ar), jnp.sin(ar), -jnp.sin(ac), jnp.sin(ac)], axis=-1)
    pad = LANE - rot
    if pad:
        cos = jnp.concatenate([cos, jnp.ones((length, pad), F32)], axis=-1)
        sin = jnp.concatenate([sin, jnp.zeros((length, pad), F32)], axis=-1)
    return cos, sin


def _rope(x, cos, sin, blk):
    lane = lax.broadcasted_iota(jnp.int32, (1, LANE), 1)
    lower = (lane & blk) == 0
    partner = jnp.where(lower, pltpu.roll(x, LANE - blk, 1), pltpu.roll(x, blk, 1))
    return x * cos + partner * sin


def _softmax_pv(scores, values, extra_logit=None):
    m = scores[0].max(axis=-1, keepdims=True)
    for s in scores[1:]:
        m = jnp.maximum(m, s.max(axis=-1, keepdims=True))
    if extra_logit is not None:
        m = jnp.maximum(m, extra_logit)
    den = None
    out = None
    for s, v in zip(scores, values):
        p = jnp.exp(s - m)
        d = p.sum(axis=-1, keepdims=True)
        o = _dot(p.astype(BF16), v)
        den = d if den is None else den + d
        out = o if out is None else out + o
    if extra_logit is not None:
        den = den + jnp.exp(extra_logit - m)
    return out / den


def _ctx_gqa_kernel(sink_ref, q_ref, k_ref, v_ref, o_ref):
    scale = HEAD_DIM ** -0.5
    for g in range(WIN_KV_HEADS):
        k = k_ref[g].astype(BF16)
        v = v_ref[g].astype(BF16)
        for r in range(WIN_GROUP):
            cols = slice((g * WIN_GROUP + r) * HEAD_DIM, (g * WIN_GROUP + r + 1) * HEAD_DIM)
            s = _dot_t(q_ref[:, cols], k) * scale
            o = _softmax_pv([s], [v], sink_ref[g * WIN_GROUP + r])
            o_ref[:, cols] = o.astype(o_ref.dtype)


def _ctx_gqa_attention(uq3, k4, v4, sink, n_prompt, length):
    kv_spec = pl.BlockSpec((None, WIN_KV_HEADS, length, HEAD_DIM), lambda b: (b, 0, 0, 0))
    return pl.pallas_call(
        _ctx_gqa_kernel,
        grid=(n_prompt,),
        in_specs=[pl.BlockSpec(memory_space=pltpu.SMEM),
                  pl.BlockSpec((None, length, WIN_Q_WIDTH), lambda b: (b, 0, 0)),
                  kv_spec, kv_spec],
        out_specs=pl.BlockSpec((None, length, WIN_Q_WIDTH), lambda b: (b, 0, 0)),
        out_shape=jax.ShapeDtypeStruct((n_prompt, length, WIN_Q_WIDTH), BF16),
        compiler_params=_params(("arbitrary",), 40),
        name="ctx_gqa_attention",
    )(sink, uq3, k4, v4)


def _lat_window_kernel(sink_ref, q_ref, k_ref, v_ref, kc_ref, vc_ref, cos_ref, sin_ref, o_ref, *, length):
    g = pl.program_id(1)
    scale = HEAD_DIM ** -0.5
    blk = WIN_RADIUS
    nb = length // blk
    k = _rope(k_ref[...], cos_ref[...], sin_ref[...], HEAD_DIM // 4).astype(BF16)
    v = v_ref[...].astype(BF16)
    kc = kc_ref[...].astype(BF16)
    vc = vc_ref[...].astype(BF16)
    rows = WIN_GROUP * blk
    row = lax.broadcasted_iota(jnp.int32, (rows, 1), 0)
    sk = jnp.zeros((rows, 1), F32)
    for r in range(WIN_GROUP):
        sk = jnp.where(jnp.logical_and(row >= r * blk, row < (r + 1) * blk), sink_ref[g * WIN_GROUP + r], sk)
    qoff = row & (blk - 1)
    for n in range(nb):
        lo = max(0, n - 1) * blk
        hi = min(nb, n + 2) * blk
        cos = cos_ref[n * blk:(n + 1) * blk, :]
        sin = sin_ref[n * blk:(n + 1) * blk, :]
        q = jnp.concatenate(
            [_rope(q_ref[n * blk:(n + 1) * blk, r * HEAD_DIM:(r + 1) * HEAD_DIM].astype(F32), cos, sin, HEAD_DIM // 4)
             for r in range(WIN_GROUP)], axis=0).astype(BF16)
        dist = (n * blk + qoff) - (lo + lax.broadcasted_iota(jnp.int32, (1, hi - lo), 1))
        visible = jnp.logical_and(dist <= WIN_RADIUS, dist >= -WIN_RADIUS)
        s_loc = jnp.where(visible, _dot_t(q, k[lo:hi]) * scale, NEG)
        s_ctx = _dot_t(q, kc) * scale
        o = _softmax_pv([s_loc, s_ctx], [v[lo:hi], vc], sk)
        for r in range(WIN_GROUP):
            o_ref[n * blk:(n + 1) * blk, r * HEAD_DIM:(r + 1) * HEAD_DIM] = o[r * blk:(r + 1) * blk].astype(o_ref.dtype)


def _lat_window_attention(uq3, seq0, k4, v4, k_ctx, v_ctx, sink, cos, sin):
    n_lat, _, length, _ = k4.shape
    qw = WIN_GROUP * HEAD_DIM
    lc = k_ctx.shape[2]
    kv_spec = pl.BlockSpec((None, None, length, HEAD_DIM), lambda b, g: (b, g, 0, 0))
    ctx_spec = pl.BlockSpec((None, None, lc, HEAD_DIM), lambda b, g: (b, g, 0, 0))
    tab_spec = pl.BlockSpec((length, LANE), lambda b, g: (0, 0))
    return pl.pallas_call(
        functools.partial(_lat_window_kernel, length=length),
        grid=(n_lat, WIN_KV_HEADS),
        in_specs=[pl.BlockSpec(memory_space=pltpu.SMEM),
                  pl.BlockSpec((None, length, qw), lambda b, g: (seq0 + b, 0, g)),
                  kv_spec, kv_spec, ctx_spec, ctx_spec, tab_spec, tab_spec],
        out_specs=pl.BlockSpec((None, length, qw), lambda b, g: (b, 0, g)),
        out_shape=jax.ShapeDtypeStruct((n_lat, length, WIN_Q_WIDTH), BF16),
        compiler_params=_params(("arbitrary", "arbitrary"), 48),
        name="latent_window_attention",
    )(sink, uq3, k4, v4, k_ctx, v_ctx, cos, sin)


def _ctx_odd_kernel(q_ref, kv_ref, kpe_ref, qd_ref, kd_ref, vd_ref, o_ref):
    kw = MLA_NOPE + LANE
    kpe = kpe_ref[...].astype(BF16)
    scale_c = (MLA_NOPE + MLA_ROPE) ** -0.5
    for h in range(MLA_HEADS):
        q = q_ref[:, h * kw:(h + 1) * kw]
        k = jnp.concatenate([kv_ref[:, h * kw:h * kw + MLA_NOPE], kpe], axis=1)
        v = kv_ref[:, h * kw + MLA_NOPE:(h + 1) * kw]
        o = _softmax_pv([_dot_t(q, k) * scale_c], [v])
        o_ref[:, h * MLA_V:(h + 1) * MLA_V] = o.astype(o_ref.dtype)
    scale_d = HEAD_DIM ** -0.5
    base = MLA_HEADS * MLA_V
    for h in range(NA_HEADS):
        q = qd_ref[:, h * HEAD_DIM:(h + 1) * HEAD_DIM]
        o = _softmax_pv([_dot_t(q, kd_ref[h].astype(BF16)) * scale_d], [vd_ref[h].astype(BF16)])
        o_ref[:, base + h * HEAD_DIM:base + (h + 1) * HEAD_DIM] = o.astype(o_ref.dtype)


def _ctx_odd_attention(q3, kv3, kpe3, qd3, kd4, vd4, n_prompt, length):
    qn = q3.shape[-1]
    head_spec = pl.BlockSpec((None, NA_HEADS, length, HEAD_DIM), lambda b: (b, 0, 0, 0))
    return pl.pallas_call(
        _ctx_odd_kernel,
        grid=(n_prompt,),
        in_specs=[pl.BlockSpec((None, length, qn), lambda b: (b, 0, 0)),
                  pl.BlockSpec((None, length, qn), lambda b: (b, 0, 0)),
                  pl.BlockSpec((None, length, LANE), lambda b: (b, 0, 0)),
                  pl.BlockSpec((None, length, NA_WIDTH), lambda b: (b, 0, 0)),
                  head_spec, head_spec],
        out_specs=pl.BlockSpec((None, length, MLA_HEADS * MLA_V + NA_WIDTH), lambda b: (b, 0, 0)),
        out_shape=jax.ShapeDtypeStruct((n_prompt, length, MLA_HEADS * MLA_V + NA_WIDTH), BF16),
        compiler_params=_params(("arbitrary",), 48),
        name="ctx_odd_attention",
    )(q3, kv3, kpe3, qd3, kd4, vd4)


def _lat_mla_kernel(q_ref, kv_ref, kpe_ref, kvc_ref, kpec_ref, cos_ref, sin_ref, o_ref, *, length):
    scale = (MLA_NOPE + MLA_ROPE) ** -0.5
    rb = MLA_ROPE // 4
    hw = MLA_NOPE + LANE
    kpe = _rope(kpe_ref[...], cos_ref[...], sin_ref[...], rb).astype(BF16)
    kpe_ctx = kpec_ref[...].astype(BF16)
    qb = 256
    for hh in range(ATTN_HEADS_PER_STEP):
        c0 = hh * hw
        k_lat = jnp.concatenate([kv_ref[:, c0:c0 + MLA_NOPE], kpe], axis=1)
        v_lat = kv_ref[:, c0 + MLA_NOPE:c0 + hw]
        k_ctx = jnp.concatenate([kvc_ref[:, c0:c0 + MLA_NOPE], kpe_ctx], axis=1)
        v_ctx = kvc_ref[:, c0 + MLA_NOPE:c0 + hw]
        for n in range(length // qb):
            rows = slice(n * qb, (n + 1) * qb)
            q_pe = _rope(q_ref[rows, c0 + MLA_NOPE:c0 + hw].astype(F32), cos_ref[rows, :], sin_ref[rows, :],
                         rb).astype(BF16)
            q = jnp.concatenate([q_ref[rows, c0:c0 + MLA_NOPE], q_pe], axis=1)
            o = _softmax_pv([_dot_t(q, k_lat) * scale, _dot_t(q, k_ctx) * scale], [v_lat, v_ctx])
            o_ref[rows, hh * MLA_V:(hh + 1) * MLA_V] = o.astype(o_ref.dtype)


def _lat_mla_attention(q3, kv3, kvc3, kpe3, kpe_ctx, seq0, n_lat, length, cos, sin):
    kw = ATTN_HEADS_PER_STEP * (MLA_NOPE + LANE)
    lc = kpe_ctx.shape[1]
    return pl.pallas_call(
        functools.partial(_lat_mla_kernel, length=length),
        grid=(n_lat, MLA_HEADS // ATTN_HEADS_PER_STEP),
        in_specs=[pl.BlockSpec((None, length, kw), lambda b, h: (seq0 + b, 0, h)),
                  pl.BlockSpec((None, length, kw), lambda b, h: (seq0 + b, 0, h)),
                  pl.BlockSpec((None, length, LANE), lambda b, h: (seq0 + b, 0, 0)),
                  pl.BlockSpec((None, lc, kw), lambda b, h: (b, 0, h)),
                  pl.BlockSpec((None, lc, LANE), lambda b, h: (b, 0, 0)),
                  pl.BlockSpec((length, LANE), lambda b, h: (0, 0)),
                  pl.BlockSpec((length, LANE), lambda b, h: (0, 0))],
        out_specs=pl.BlockSpec((None, length, ATTN_HEADS_PER_STEP * MLA_V), lambda b, h: (b, 0, h)),
        out_shape=jax.ShapeDtypeStruct((n_lat, length, MLA_HEADS * MLA_V), BF16),
        compiler_params=_params(("arbitrary", "arbitrary"), 48),
        name="latent_mla_attention",
    )(q3, kv3, kpe3, kvc3, kpe_ctx, cos, sin)


def _na_row_start(r, rows):
    kr = min(NA_ROWS, rows)
    return min(max(r - kr // 2, 0), rows - kr)


def _lat_na_kernel(q_ref, k_ref, v_ref, kc_ref, vc_ref, bias_ref, o_ref, *, length):
    scale = HEAD_DIM ** -0.5
    rows = length // GRID_W
    kr = min(NA_ROWS, rows)
    for hh in range(ATTN_HEADS_PER_STEP):
        cols = slice(hh * HEAD_DIM, (hh + 1) * HEAD_DIM)
        k = k_ref[hh].astype(BF16)
        v = v_ref[hh].astype(BF16)
        kc = kc_ref[hh].astype(BF16)
        vc = vc_ref[hh].astype(BF16)
        r = 0
        while r < rows:
            r_end = r + 1
            while r_end < rows and _na_row_start(r_end, rows) == _na_row_start(r, rows):
                r_end += 1
            r0 = _na_row_start(r, rows) * GRID_W
            nq = (r_end - r) * GRID_W
            q = q_ref[r * GRID_W:r_end * GRID_W, cols]
            bias = bias_ref[hh, r:r_end].reshape(nq, kr * GRID_W)
            s_nb = _dot_t(q, k[r0:r0 + kr * GRID_W]) * scale + bias
            s_ctx = _dot_t(q, kc) * scale
            o = _softmax_pv([s_nb, s_ctx], [v[r0:r0 + kr * GRID_W], vc])
            o_ref[r * GRID_W:r_end * GRID_W, cols] = o.astype(o_ref.dtype)
            r = r_end


def _na_bias(rpb, length):
    rows = length // GRID_W
    kr = min(NA_ROWS, rows)
    col = jnp.arange(GRID_W)
    c_start = jnp.clip(col - NA_COLS // 2, 0, GRID_W - NA_COLS)
    col_valid = (col[None, :] >= c_start[:, None]) & (col[None, :] < c_start[:, None] + NA_COLS)
    off_c = jnp.clip(col[None, :] - col[:, None], -(NA_COLS - 1), NA_COLS - 1) + NA_COLS - 1
    onehot = (off_c[:, :, None] == jnp.arange(2 * NA_COLS - 1)[None, None, :]).astype(F32)
    table = jnp.einsum('hdj,qkj->hdqk', rpb.astype(F32), onehot, precision=lax.Precision.HIGHEST)
    table = jnp.where(col_valid[None, None], table, NEG)
    per_row = []
    for r in range(rows):
        r0 = _na_row_start(r, rows)
        per_row.append(jnp.concatenate([table[:, r0 + j - r + NA_ROWS - 1] for j in range(kr)], axis=-1))
    return jnp.stack(per_row, axis=1)


def _lat_na_attention(qd3, seq0, k4, v4, k_ctx, v_ctx, bias):
    n_lat, _, length, _ = k4.shape
    rows = length // GRID_W
    kr = min(NA_ROWS, rows)
    lc = k_ctx.shape[2]
    hp = ATTN_HEADS_PER_STEP
    kv_spec = pl.BlockSpec((None, hp, length, HEAD_DIM), lambda h, b: (b, h, 0, 0))
    ctx_spec = pl.BlockSpec((None, hp, lc, HEAD_DIM), lambda h, b: (b, h, 0, 0))
    return pl.pallas_call(
        functools.partial(_lat_na_kernel, length=length),
        grid=(NA_HEADS // hp, n_lat),
        in_specs=[pl.BlockSpec((None, length, hp * HEAD_DIM), lambda h, b: (seq0 + b, 0, h)),
                  kv_spec, kv_spec, ctx_spec, ctx_spec,
                  pl.BlockSpec((hp, rows, GRID_W, kr * GRID_W), lambda h, b: (h, 0, 0, 0))],
        out_specs=pl.BlockSpec((None, length, hp * HEAD_DIM), lambda h, b: (b, 0, h)),
        out_shape=jax.ShapeDtypeStruct((n_lat, length, NA_WIDTH), BF16),
        compiler_params=_params(("arbitrary", "arbitrary"), 48),
        name="latent_neighborhood_attention",
    )(qd3, k4, v4, k_ctx, v_ctx, bias)


def kernel(x_prompt, x_sample, cache_l0_k, cache_l0_v, state_l0_re, state_l0_im, cache_l1_ckv, cache_l1_kpe, cache_l1_k, cache_l1_v, c, c_ctx, l0_ada_w, l0_ada_b, l0_norm1, l0_norm2, l0_w_in, l0_lambda_re, l0_lambda_im, l0_log_dt, l0_b_re, l0_b_im, l0_c_re, l0_c_im, l0_d_skip, l0_w_glu, l0_b_glu, l0_sink, l0_w_out, l0_ffn_w_up, l0_ffn_conv_w, l0_ffn_conv_b, l0_ffn_w_down, l1_ada_w, l1_ada_b, l1_norm1, l1_norm2, l1_w_in, l1_q_norm, l1_kv_norm, l1_w_uq, l1_w_ukv, l1_rpb, l1_w_out, l1_ffn_w_up, l1_ffn_conv_w, l1_ffn_conv_b, l1_ffn_w_down, final_norm):
    n_p, len_p, d = x_prompt.shape
    n_s, len_s, _ = x_sample.shape
    rows_p, rows_s = n_p * len_p, n_s * len_s
    t = rows_p + rows_s
    assert d == D_MODEL and len_s == ROW_TILE and ROW_TILE % len_p == 0 and n_s + 1 <= MAX_CONDS
    assert rows_p % ROW_TILE == 0
    assert len_p % S5_CHUNK == 0 and len_s % (2 * WIN_RADIUS) == 0 and len_s % GRID_W == 0
    seqs_p_units = t // len_p
    seqs_s_units = t // len_s
    seq0_s = rows_p // len_s
    groups = dict(n_prompt_rows=rows_p, lat_len=len_s)
    half = D_MODEL // 2

    cond = jnp.zeros((MAX_CONDS, d), F32).at[0].set(c_ctx).at[1:1 + n_s].set(c)
    mod0 = _modulation(cond, l0_ada_w, l0_ada_b)
    mod1 = _modulation(cond, l1_ada_w, l1_ada_b)

    x_parts = [x_prompt.reshape(rows_p, d), x_sample.reshape(rows_s, d)]

    def ffn(x, mod, norm2, w_up, conv_w, conv_b, w_down):
        h = _norm_mod([x], norm2, mod, 3, 4, **groups)
        act = _ffn_up(h, w_up, conv_w, conv_b, rows_p, len_p, len_s)
        return _matmul_residual([[(act, 0)]], w_down.astype(BF16), [x], mod, 5, 512, name="ffn_down",
                                bm=512, **groups)

    h = _norm_mod(x_parts, l0_norm1, mod0, 0, 1, **groups)
    u = _matmul(h, l0_w_in, F32, 512, "l0_in_proj_u", 0, S5_WIDTH)
    q0 = _matmul(h, l0_w_in, BF16, 512, "l0_in_proj_q", S5_WIDTH, WIN_Q_WIDTH)
    kcol = S5_WIDTH + WIN_Q_WIDTH
    vcol = kcol + WIN_KV_WIDTH
    out_k0 = _matmul_heads(h, l0_w_in, kcol, WIN_KV_HEADS, 0, rows_p, len_p, "l0_in_proj_k_ctx")
    out_v0 = _matmul_heads(h, l0_w_in, vcol, WIN_KV_HEADS, 0, rows_p, len_p, "l0_in_proj_v_ctx")
    k0_s = _matmul_heads(h, l0_w_in, kcol, WIN_KV_HEADS, rows_p, rows_s, len_s, "l0_in_proj_k_lat")
    v0_s = _matmul_heads(h, l0_w_in, vcol, WIN_KV_HEADS, rows_p, rows_s, len_s, "l0_in_proj_v_lat")

    assert rows_p % rows_s == 0
    sel = _s5_lane_permutation()
    sb, t_op, e_op, a16 = _s5_operators(l0_lambda_re, l0_lambda_im, l0_log_dt, l0_b_re, l0_b_im, l0_c_re, l0_c_im, l0_d_skip)
    nblk, gps, pw = S5_WIDTH // LANE, GROUPS_PER_STEP, 2 * S5_STATE

    def planes(cols):
        b = cols.shape[1]
        c5 = cols.reshape(nblk, gps, b, 2, pw).transpose(0, 2, 3, 1, 4)
        return c5.reshape(nblk, b, 2 * STATE_W)

    def state_cols(s):
        return s.astype(F32).transpose(2, 0, 1, 3).reshape(S5_GROUPS, s.shape[0], pw)

    np8, ns8 = _s5_chunk_rows(n_p), _s5_chunk_rows(n_s)
    a_planes = planes(a16)
    h0_lat = planes(jnp.concatenate([state_cols(state_l0_re), state_cols(state_l0_im)], axis=-1))
    h0_s = jnp.pad(h0_lat, ((0, 0), (0, ns8 - n_s), (0, 0)))
    h0_p = jnp.zeros((nblk, np8, 2 * STATE_W), F32)
    y_p, hfin = _s5_stream(u, 0, n_p, len_p, sel, sb, t_op, e_op, a_planes, h0_p)
    y_s, _ = _s5_stream(u, rows_p // rows_s, n_s, len_s, sel, sb, t_op, e_op, a_planes, h0_s)
    a_out = _glu([y_p, y_s], l0_w_glu, l0_b_glu, rows_p)

    def state_out(plane):
        return (plane[:, :n_p].reshape(nblk, n_p, gps, 2, S5_STATE).transpose(1, 3, 0, 2, 4)
                .reshape(n_p, 2, S5_GROUPS, S5_STATE))

    out_sre = state_out(hfin[:, :, :STATE_W])
    out_sim = state_out(hfin[:, :, STATE_W:])

    cos_e, sin_e = _rope_tables(len_s, HEAD_DIM)
    o_p = _ctx_gqa_attention(q0.reshape(seqs_p_units, len_p, -1), out_k0, out_v0, l0_sink, n_p, len_p)
    o_s = _lat_window_attention(q0.reshape(seqs_s_units, len_s, -1), seq0_s, k0_s, v0_s,
                                cache_l0_k, cache_l0_v, l0_sink, cos_e, sin_e)
    x = _matmul_residual([[(a_out, 0)], [(o_p.reshape(rows_p, half), 0), (o_s.reshape(rows_s, half), 0)]],
                         l0_w_out.astype(BF16), x_parts, mod0, 2, 512, name="l0_out_proj", **groups)
    x = ffn(x, mod0, l0_norm2, l0_ffn_w_up, l0_ffn_conv_w, l0_ffn_conv_b, l0_ffn_w_down)

    s1, s2 = MLA_Q_LORA + MLA_KV_LORA, MLA_Q_LORA + MLA_KV_LORA + MLA_ROPE
    w_lora = l1_w_in[:, :s1].astype(BF16)
    w_dkv = l1_w_in[:, s2:].astype(BF16)
    w_kpe = jnp.pad(l1_w_in[:, s1:s2], ((0, 0), (0, LANE - MLA_ROPE))).astype(BF16)
    w_uq = l1_w_uq.reshape(MLA_Q_LORA, MLA_HEADS, MLA_NOPE + MLA_ROPE)
    w_uq = jnp.pad(w_uq, ((0, 0), (0, 0), (0, LANE - MLA_ROPE))).reshape(MLA_Q_LORA, -1).astype(BF16)

    h = _norm_mod([x], l1_norm1, mod1, 0, 1, **groups)
    cqkv = _matmul(h, w_lora, F32, 512, "l1_in_proj_lora")
    kpe = _matmul(h, w_kpe, F32, LANE, "l1_in_proj_kpe")
    qd = _matmul(h, w_dkv, BF16, 512, "l1_in_proj_qd", 0, NA_WIDTH)
    out_k1 = _matmul_heads(h, w_dkv, NA_WIDTH, NA_HEADS, 0, rows_p, len_p, "l1_in_proj_kd_ctx")
    out_v1 = _matmul_heads(h, w_dkv, 2 * NA_WIDTH, NA_HEADS, 0, rows_p, len_p, "l1_in_proj_vd_ctx")
    k1_s = _matmul_heads(h, w_dkv, NA_WIDTH, NA_HEADS, rows_p, rows_s, len_s, "l1_in_proj_kd_lat")
    v1_s = _matmul_heads(h, w_dkv, 2 * NA_WIDTH, NA_HEADS, rows_p, rows_s, len_s, "l1_in_proj_vd_lat")
    cqn = _rmsnorm_cols(cqkv, l1_q_norm, 0, MLA_Q_LORA, BF16)
    ckvn = _rmsnorm_cols(cqkv, l1_kv_norm, MLA_Q_LORA // MLA_KV_LORA, MLA_KV_LORA, F32)
    q_all = _matmul(cqn, w_uq, BF16, 2048, "l1_q_up")
    w_ukv = l1_w_ukv.astype(BF16)
    kv_all = _matmul(ckvn, w_ukv, BF16, 2048, "l1_kv_up")
    lc = cache_l1_ckv.shape[1]
    kv_ctx = _matmul(cache_l1_ckv.reshape(-1, MLA_KV_LORA), w_ukv, BF16, 2048, "l1_kv_up_ctx")

    out_ckv = ckvn[:rows_p].reshape(n_p, len_p, MLA_KV_LORA)
    out_kpe = kpe[:rows_p, :MLA_ROPE].reshape(n_p, len_p, MLA_ROPE)

    o_p = _ctx_odd_attention(q_all.reshape(seqs_p_units, len_p, -1),
                             kv_all.reshape(seqs_p_units, len_p, -1),
                             kpe.reshape(seqs_p_units, len_p, LANE),
                             qd.reshape(seqs_p_units, len_p, -1), out_k1, out_v1, n_p, len_p)
    cos_o, sin_o = _rope_tables(len_s, MLA_ROPE)
    kpe_ctx = jnp.pad(cache_l1_kpe, ((0, 0), (0, 0), (0, LANE - MLA_ROPE)))
    oc_s = _lat_mla_attention(q_all.reshape(seqs_s_units, len_s, -1),
                              kv_all.reshape(seqs_s_units, len_s, -1),
                              kv_ctx.reshape(n_s, lc, -1),
                              kpe.reshape(seqs_s_units, len_s, LANE), kpe_ctx,
                              seq0_s, n_s, len_s, cos_o, sin_o)
    od_s = _lat_na_attention(qd.reshape(seqs_s_units, len_s, -1), seq0_s, k1_s, v1_s,
                             cache_l1_k, cache_l1_v, _na_bias(l1_rpb, len_s))
    o_p2 = o_p.reshape(rows_p, 2 * half)
    x = _matmul_residual([[(o_p2, 0), (oc_s.reshape(rows_s, half), 0)],
                          [(o_p2, 1), (od_s.reshape(rows_s, half), 0)]],
                         l1_w_out.astype(BF16), [x], mod1, 2, 512, name="l1_out_proj", **groups)
    x = ffn(x, mod1, l1_norm2, l1_ffn_w_up, l1_ffn_conv_w, l1_ffn_conv_b, l1_ffn_w_down)

    y_prompt = _rmsnorm_cols(x, final_norm, 0, d, F32, 0, rows_p).reshape(n_p, len_p, d)
    y_sample = _rmsnorm_cols(x, final_norm, 0, d, F32, rows_p, rows_s).reshape(n_s, len_s, d)
    return (y_prompt, y_sample, out_k0, out_v0, out_sre, out_sim, out_ckv, out_kpe, out_k1, out_v1)
```

```python
import functools
import math

import jax
import jax.numpy as jnp
from jax import lax
from jax.experimental import pallas as pl
from jax.experimental.pallas import tpu as pltpu

F32 = jnp.float32
BF16 = jnp.bfloat16

D_MODEL = 4096
GRID_W = 64
HEAD_DIM = 128
S5_WIDTH = 2048
S5_GROUP = 16
S5_GROUPS = S5_WIDTH // S5_GROUP
S5_STATE = 64
S5_CHUNK = 16
WIN_HEADS = 16
WIN_KV_HEADS = 4
WIN_GROUP = WIN_HEADS // WIN_KV_HEADS
WIN_RADIUS = 128
WIN_Q_WIDTH = WIN_HEADS * HEAD_DIM
WIN_KV_WIDTH = WIN_KV_HEADS * HEAD_DIM
MLA_HEADS = 16
MLA_Q_LORA = 1024
MLA_KV_LORA = 512
MLA_NOPE = 128
MLA_ROPE = 64
MLA_V = 128
NA_HEADS = 16
NA_ROWS = 8
NA_COLS = 16
NA_WIDTH = NA_HEADS * HEAD_DIM
D_FF = 11008
ROPE_BASE = 10000.0
EPS = 1e-6
NEG = -1e30

LANE = 128
SUBLANE = 8
MAX_CONDS = 8
ROW_TILE = 1024
ATTN_HEADS_PER_STEP = 2


def _params(sem, vmem_mb):
    return pltpu.CompilerParams(dimension_semantics=sem, vmem_limit_bytes=vmem_mb * 1024 * 1024)


def _cond_index(i, bm, n_prompt_rows, lat_len):
    first = n_prompt_rows // bm
    per = lat_len // bm
    return jnp.where(i < first, 0, 1 + (i - first) // per)


def _dot(a, b):
    return jnp.dot(a, b, preferred_element_type=F32)


def _dot_t(a, b):
    return lax.dot_general(a, b, (((1,), (1,)), ((), ())), preferred_element_type=F32)


def _sigmoid(x):
    return 1.0 / (1.0 + jnp.exp(-x))


def _row_specs(parts, bm, width, col_fn, n_prompt_blocks, single_buffer=False):
    mode = dict(pipeline_mode=pl.Buffered(1)) if single_buffer else {}
    if len(parts) == 1:
        cb = parts[0][1]
        return [pl.BlockSpec((bm, width), lambda i, j: (i, col_fn(j, cb)), **mode)]
    cb_p, cb_s = parts[0][1], parts[1][1]
    last_p = n_prompt_blocks - 1
    return [pl.BlockSpec((bm, width), lambda i, j: (jnp.minimum(i, last_p), col_fn(j, cb_p)), **mode),
            pl.BlockSpec((bm, width), lambda i, j: (jnp.maximum(i - n_prompt_blocks, 0), col_fn(j, cb_s)), **mode)]


def _by_stream(i, n_prompt_blocks, operands, body):
    if all(len(o) == 1 for o in operands):
        body([o[0] for o in operands])
        return

    @pl.when(i < n_prompt_blocks)
    def _():
        body([o[0] for o in operands])

    @pl.when(i >= n_prompt_blocks)
    def _():
        body([o[-1] for o in operands])


def _mod_kernel(c_ref, w_ref, b_ref, o_ref):
    c = c_ref[...]
    s = c * _sigmoid(c)
    o_ref[...] = _dot(s.astype(BF16), w_ref[...].astype(BF16)) + b_ref[...]


def _modulation(cond, w, b):
    d, n = w.shape
    bn = 512
    out = pl.pallas_call(
        _mod_kernel,
        grid=(n // bn,),
        in_specs=[pl.BlockSpec((MAX_CONDS, d), lambda j: (0, 0)),
                  pl.BlockSpec((d, bn), lambda j: (0, j)),
                  pl.BlockSpec((1, bn), lambda j: (0, j))],
        out_specs=pl.BlockSpec((MAX_CONDS, bn), lambda j: (0, j)),
        out_shape=jax.ShapeDtypeStruct((MAX_CONDS, n), F32),
        compiler_params=_params(("arbitrary",), 40),
        name="modulation",
    )(cond, w, b.reshape(1, n))
    return out.reshape(MAX_CONDS, 1, n)


def _norm_mod_kernel(*refs, n_x, n_prompt_blocks):
    x_refs, (g_ref, sh_ref, sc_ref, o_ref) = refs[:n_x], refs[n_x:]

    def body(r):
        x = r[0][...]
        ms = jnp.mean(x * x, axis=-1, keepdims=True)
        y = x * lax.rsqrt(ms + EPS) * g_ref[...]
        o_ref[...] = (y * (1.0 + sc_ref[...]) + sh_ref[...]).astype(o_ref.dtype)

    _by_stream(pl.program_id(0), n_prompt_blocks, [x_refs], body)


def _norm_mod(x_parts, gain, mod, shift_slot, scale_slot, n_prompt_rows, lat_len):
    d = D_MODEL
    bm = 512
    t = sum(x.shape[0] for x in x_parts)
    assert n_prompt_rows % bm == 0 and lat_len % bm == 0
    npb = n_prompt_rows // bm
    cidx = functools.partial(_cond_index, bm=bm, n_prompt_rows=n_prompt_rows, lat_len=lat_len)
    specs = _row_specs([(x, 0) for x in x_parts], bm, d, lambda j, cb: 0, npb)
    to1d = lambda spec: pl.BlockSpec(spec.block_shape, lambda i, f=spec.index_map: f(i, 0))
    return pl.pallas_call(
        functools.partial(_norm_mod_kernel, n_x=len(x_parts), n_prompt_blocks=npb),
        grid=(t // bm,),
        in_specs=[to1d(s) for s in specs] + [
            pl.BlockSpec((1, d), lambda i: (0, 0)),
            pl.BlockSpec((None, 1, d), lambda i: (cidx(i), 0, shift_slot)),
            pl.BlockSpec((None, 1, d), lambda i: (cidx(i), 0, scale_slot))],
        out_specs=pl.BlockSpec((bm, d), lambda i: (i, 0)),
        out_shape=jax.ShapeDtypeStruct((t, d), BF16),
        compiler_params=_params(("arbitrary",), 56),
        name="norm_mod",
    )(*x_parts, gain.reshape(1, d), mod, mod)


def _rmsnorm_kernel(x_ref, g_ref, o_ref):
    x = x_ref[...].astype(F32)
    ms = jnp.mean(x * x, axis=-1, keepdims=True)
    o_ref[...] = (x * lax.rsqrt(ms + EPS) * g_ref[...]).astype(o_ref.dtype)


def _rmsnorm_cols(x, gain, col_block, width, out_dtype, row0=0, rows=None):
    bm = 256
    rows = x.shape[0] if rows is None else rows
    rb0 = row0 // bm
    return pl.pallas_call(
        _rmsnorm_kernel,
        grid=(rows // bm,),
        in_specs=[pl.BlockSpec((bm, width), lambda i: (i + rb0, col_block)),
                  pl.BlockSpec((1, width), lambda i: (0, 0))],
        out_specs=pl.BlockSpec((bm, width), lambda i: (i, 0)),
        out_shape=jax.ShapeDtypeStruct((rows, width), out_dtype),
        compiler_params=_params(("arbitrary",), 40),
        name="rmsnorm",
    )(x, gain.reshape(1, width))


def _mm_kernel(x_ref, w_ref, o_ref):
    o_ref[...] = _dot(x_ref[...].astype(BF16), w_ref[...].astype(BF16)).astype(o_ref.dtype)


def _matmul(x, w, out_dtype, bn, name, col0=0, ncols=None):
    m, k = x.shape
    ncols = w.shape[1] - col0 if ncols is None else ncols
    bm = min(ROW_TILE, m)
    assert m % bm == 0 and ncols % bn == 0 and col0 % bn == 0
    cb0 = col0 // bn
    return pl.pallas_call(
        _mm_kernel,
        grid=(m // bm, ncols // bn),
        in_specs=[pl.BlockSpec((bm, k), lambda i, j: (i, 0)),
                  pl.BlockSpec((k, bn), lambda i, j: (0, cb0 + j))],
        out_specs=pl.BlockSpec((bm, bn), lambda i, j: (i, j)),
        out_shape=jax.ShapeDtypeStruct((m, ncols), out_dtype),
        compiler_params=_params(("arbitrary", "arbitrary"), 56),
        name=name,
    )(x, w)


def _mm_heads_kernel(x_ref, w_ref, o_ref, *, seqs, seq_len, heads):
    acc = _dot(x_ref[...], w_ref[...].astype(BF16))
    for b in range(seqs):
        for hh in range(heads):
            o_ref[b, hh] = acc[b * seq_len:(b + 1) * seq_len, hh * HEAD_DIM:(hh + 1) * HEAD_DIM]


def _matmul_heads(x, w, col0, heads, row0, rows, seq_len, name):
    k = x.shape[1]
    bm = ROW_TILE
    hb = 4
    bn = hb * HEAD_DIM
    assert rows % bm == 0 and row0 % bm == 0 and bm % seq_len == 0 and heads % hb == 0 and col0 % bn == 0
    seqs = bm // seq_len
    rb0, cb0 = row0 // bm, col0 // bn
    return pl.pallas_call(
        functools.partial(_mm_heads_kernel, seqs=seqs, seq_len=seq_len, heads=hb),
        grid=(rows // bm, heads // hb),
        in_specs=[pl.BlockSpec((bm, k), lambda i, j: (rb0 + i, 0)),
                  pl.BlockSpec((k, bn), lambda i, j: (0, cb0 + j))],
        out_specs=pl.BlockSpec((seqs, hb, seq_len, HEAD_DIM), lambda i, j: (i, j, 0, 0)),
        out_shape=jax.ShapeDtypeStruct((rows // seq_len, heads, seq_len, HEAD_DIM), F32),
        compiler_params=_params(("arbitrary", "arbitrary"), 56),
        name=name,
    )(x, w)


def _mm_res_kernel(*refs, n_lhs, n_parts, n_res, n_prompt_blocks):
    pos = 0
    lhs = []
    for n in n_lhs:
        lhs.append(refs[pos:pos + n])
        pos += n
    w_refs = refs[pos:pos + n_parts]
    pos += n_parts
    res = refs[pos:pos + n_res]
    pos += n_res
    gate_ref, o_ref = refs[pos], refs[pos + 1]

    def body(r):
        acc = None
        for x_ref, w_ref in zip(r[:n_parts], w_refs):
            part = _dot(x_ref[...], w_ref[...].astype(BF16))
            acc = part if acc is None else acc + part
        o_ref[...] = r[n_parts][...] + gate_ref[...] * acc

    _by_stream(pl.program_id(0), n_prompt_blocks, lhs + [res], body)


def _matmul_residual(lhs_parts, w, res_parts, mod, gate_slot, bn, n_prompt_rows, lat_len, name,
                     single_buffer_x=False, vmem_mb=56, bm=ROW_TILE):
    n = w.shape[1]
    kq = w.shape[0] // len(lhs_parts)
    t = sum(r.shape[0] for r in res_parts)
    npb = n_prompt_rows // bm
    cidx = functools.partial(_cond_index, bm=bm, n_prompt_rows=n_prompt_rows, lat_len=lat_len)
    gate_col0 = gate_slot * (D_MODEL // bn)
    in_specs, args = [], []
    for parts in lhs_parts:
        in_specs += _row_specs(parts, bm, kq, lambda j, cb: cb, npb, single_buffer=single_buffer_x)
        args += [a for a, _ in parts]
    for q in range(len(lhs_parts)):
        in_specs.append(pl.BlockSpec((kq, bn), lambda i, j, q=q: (q, j)))
        args.append(w)
    in_specs += _row_specs([(r, 0) for r in res_parts], bm, bn, lambda j, cb: j, npb)
    args += list(res_parts)
    in_specs.append(pl.BlockSpec((None, 1, bn), lambda i, j: (cidx(i), 0, gate_col0 + j)))
    args.append(mod)
    body = functools.partial(_mm_res_kernel, n_lhs=tuple(len(p) for p in lhs_parts), n_parts=len(lhs_parts),
                             n_res=len(res_parts), n_prompt_blocks=npb)
    return pl.pallas_call(
        body,
        grid=(t // bm, n // bn),
        in_specs=in_specs,
        out_specs=pl.BlockSpec((bm, bn), lambda i, j: (i, j)),
        out_shape=jax.ShapeDtypeStruct((t, n), F32),
        compiler_params=_params(("arbitrary", "arbitrary"), vmem_mb),
        name=name,
    )(*args)


def _glu_kernel(*refs, n_y, n_prompt_blocks):
    y_refs = refs[:n_y]
    w_ref, b_ref, o_ref = refs[n_y:]

    def body(r):
        y = r[0][...]
        z = _dot(y.astype(BF16), w_ref[...]) + b_ref[...]
        o_ref[...] = (y * _sigmoid(z)).astype(o_ref.dtype)

    _by_stream(pl.program_id(0), n_prompt_blocks, [y_refs], body)


def _glu(y_parts, w, b, n_prompt_rows):
    k, n = w.shape
    bm = ROW_TILE
    m = sum(y.shape[0] for y in y_parts)
    npb = n_prompt_rows // bm
    parts = [(y, 0) for y in y_parts]
    return pl.pallas_call(
        functools.partial(_glu_kernel, n_y=len(y_parts), n_prompt_blocks=npb),
        grid=(m // bm, 1),
        in_specs=(_row_specs(parts, bm, k, lambda j, cb: 0, npb)
                  + [pl.BlockSpec((k, n), lambda i, j: (0, 0), pipeline_mode=pl.Buffered(1)),
                     pl.BlockSpec((1, n), lambda i, j: (0, 0))]),
        out_specs=pl.BlockSpec((bm, n), lambda i, j: (i, 0)),
        out_shape=jax.ShapeDtypeStruct((m, n), BF16),
        compiler_params=_params(("arbitrary", "arbitrary"), 56),
        name="s5_glu",
    )(*y_parts, w.astype(BF16), b.reshape(1, n))


def _ffn_up_kernel(x_ref, wg_ref, wv_ref, cwg_ref, cwv_ref, cbg_ref, cbv_ref, o_ref, *,
                   sub, n_sub, prompt_blocks, prompt_len, lat_len):
    i = pl.program_id(0)
    row = lax.broadcasted_iota(jnp.int32, (sub, 1), 0)
    wg = wg_ref[...].astype(BF16)
    wv = wv_ref[...].astype(BF16)

    for s in range(n_sub):
        period = jnp.where(i * n_sub + s < prompt_blocks, prompt_len, lat_len)
        pos = row & (period - 1)
        first = pos == 0
        last = pos == period - 1

        def conv(u, cw_ref, cb_ref):
            prev = jnp.where(first, 0.0, pltpu.roll(u, 1, 0))
            nxt = jnp.where(last, 0.0, pltpu.roll(u, sub - 1, 0))
            return prev * cw_ref[0:1, :] + u * cw_ref[1:2, :] + nxt * cw_ref[2:3, :] + cb_ref[...]

        x = x_ref[s * sub:(s + 1) * sub, :]
        g = conv(_dot(x, wg), cwg_ref, cbg_ref)
        v = conv(_dot(x, wv), cwv_ref, cbv_ref)
        o_ref[s * sub:(s + 1) * sub, :] = (g * _sigmoid(g) * v).astype(o_ref.dtype)


FFN_SUB_BLOCKS = 4


def _ffn_up(h, w_up, conv_w, conv_b, n_prompt_rows, prompt_len, lat_len):
    t, d = h.shape
    sub, bn = lat_len, 256
    n_sub = FFN_SUB_BLOCKS if (t // sub) % FFN_SUB_BLOCKS == 0 else 2
    bm = sub * n_sub
    assert t % bm == 0 and n_prompt_rows % sub == 0 and sub % prompt_len == 0
    assert prompt_len & (prompt_len - 1) == 0 and lat_len & (lat_len - 1) == 0
    nj = D_FF // bn
    conv_b = conv_b.reshape(1, 2 * D_FF)
    body = functools.partial(_ffn_up_kernel, sub=sub, n_sub=n_sub, prompt_blocks=n_prompt_rows // sub,
                             prompt_len=prompt_len, lat_len=lat_len)
    return pl.pallas_call(
        body,
        grid=(t // bm, nj),
        in_specs=[pl.BlockSpec((bm, d), lambda i, j: (i, 0), pipeline_mode=pl.Buffered(1)),
                  pl.BlockSpec((d, bn), lambda i, j: (0, j)),
                  pl.BlockSpec((d, bn), lambda i, j: (0, nj + j)),
                  pl.BlockSpec((3, bn), lambda i, j: (0, j)),
                  pl.BlockSpec((3, bn), lambda i, j: (0, nj + j)),
                  pl.BlockSpec((1, bn), lambda i, j: (0, j)),
                  pl.BlockSpec((1, bn), lambda i, j: (0, nj + j))],
        out_specs=pl.BlockSpec((bm, bn), lambda i, j: (i, j)),
        out_shape=jax.ShapeDtypeStruct((t, D_FF), BF16),
        compiler_params=_params(("arbitrary", "arbitrary"), 60),
        name="ffn_up_conv",
    )(h, w_up, w_up, conv_w, conv_w, conv_b, conv_b)


def _s5_operators(lam_re, lam_im, log_dt, b_re, b_im, c_re, c_im, d_skip):
    q, g, p, c = S5_CHUNK, S5_GROUPS, S5_STATE, S5_GROUP
    dt = jnp.exp(log_dt)[None, :, :, None]
    tau = jnp.arange(q + 1, dtype=F32)[:, None, None, None]
    mag = jnp.exp(lam_re[None] * dt * tau)
    ang = lam_im[None] * dt * tau
    pw_re, pw_im = mag * jnp.cos(ang), mag * jnp.sin(ang)
    dt1 = jnp.exp(log_dt)[:, :, None]
    m1 = jnp.exp(lam_re * dt1)
    ab_re, ab_im = m1 * jnp.cos(lam_im * dt1), m1 * jnp.sin(lam_im * dt1)
    den = lam_re * lam_re + lam_im * lam_im
    nr, ni = ab_re - 1.0, ab_im
    f_re = ((nr * lam_re + ni * lam_im) / den)[..., None]
    f_im = ((ni * lam_re - nr * lam_im) / den)[..., None]
    bb_re = f_re * b_re - f_im * b_im
    bb_im = f_re * b_im + f_im * b_re
    ct_re, ct_im = jnp.swapaxes(c_re, -1, -2), jnp.swapaxes(c_im, -1, -2)

    def power_times_c(fwd_taus, bwd_taus):
        pr = jnp.stack([pw_re[fwd_taus, 0], pw_re[bwd_taus, 1]], axis=0).transpose(0, 2, 3, 1)
        pi = jnp.stack([pw_im[fwd_taus, 0], pw_im[bwd_taus, 1]], axis=0).transpose(0, 2, 3, 1)
        r_re = pr[..., None] * ct_re[:, :, :, None, :] - pi[..., None] * ct_im[:, :, :, None, :]
        r_im = pr[..., None] * ct_im[:, :, :, None, :] + pi[..., None] * ct_re[:, :, :, None, :]
        return r_re.reshape(2, g, p, q * c), r_im.reshape(2, g, p, q * c)

    lags = jnp.arange(q)
    rk_re, rk_im = power_times_c(lags, lags[::-1])
    diag = jnp.tile(d_skip[:, :, None] * jnp.eye(c, dtype=F32)[None], (1, 1, q))
    bbt_re, bbt_im = jnp.swapaxes(bb_re, -1, -2), jnp.swapaxes(bb_im, -1, -2)
    t_op = _s5_toeplitz(rk_re, rk_im, bbt_re, bbt_im, diag)
    re_re, re_im = power_times_c(lags + 1, q - lags)
    zero = jnp.zeros_like(re_re[0])
    e_op = jnp.concatenate([re_re[0], zero, -re_im[0], zero, zero, re_re[1], zero, -re_im[1]], axis=1)
    def s_op(taus, d):
        pr, pi = pw_re[taus, d][:, :, None, :], pw_im[taus, d][:, :, None, :]
        br, bi = bbt_re[d][None], bbt_im[d][None]
        s_re = (pr * br - pi * bi).transpose(1, 0, 2, 3).reshape(g, q * c, p)
        s_im = (pr * bi + pi * br).transpose(1, 0, 2, 3).reshape(g, q * c, p)
        return s_re, s_im

    sf_re, sf_im = s_op(lags[::-1], 0)
    sb_re, sb_im = s_op(lags, 1)
    sb = jnp.concatenate([sf_re, sb_re, sf_im, sb_im], axis=2)
    a16 = jnp.concatenate([pw_re[q, 0], pw_re[q, 1], pw_im[q, 0], pw_im[q, 1]], axis=-1)[:, None, :]
    return sb.astype(BF16), t_op, e_op.astype(BF16), a16


def _s5_toeplitz_kernel(rr_ref, ri_ref, br_ref, bi_ref, dg_ref, o_ref):
    c, q = S5_GROUP, S5_CHUNK
    lane = lax.broadcasted_iota(jnp.int32, (c, q * c), 1)

    def lag_kernels(d, gg):
        hp = lax.Precision.HIGHEST
        return (jnp.dot(br_ref[d, gg], rr_ref[d, gg], precision=hp, preferred_element_type=F32)
                - jnp.dot(bi_ref[d, gg], ri_ref[d, gg], precision=hp, preferred_element_type=F32))

    for gg in range(TOEPLITZ_GROUPS_PER_STEP):
        kf = lag_kernels(0, gg)
        kb = lag_kernels(1, gg)
        dg = dg_ref[gg]
        for s in range(q):
            f = kf if s == 0 else pltpu.roll(kf, c * s, 1)
            b = kb if s == q - 1 else pltpu.roll(kb, q * c - c * (q - 1 - s), 1)
            piece = jnp.where(lane >= c * s, f, 0.0) + jnp.where(lane < c * (s + 1), b, 0.0)
            piece = piece + jnp.where(jnp.logical_and(lane >= c * s, lane < c * (s + 1)), dg, 0.0)
            o_ref[gg, s * c:(s + 1) * c, :] = piece.astype(o_ref.dtype)


TOEPLITZ_GROUPS_PER_STEP = 4


def _s5_toeplitz(rk_re, rk_im, bbt_re, bbt_im, diag):
    g, p, c, qc = S5_GROUPS, S5_STATE, S5_GROUP, S5_CHUNK * S5_GROUP
    gs = TOEPLITZ_GROUPS_PER_STEP
    r_spec = pl.BlockSpec((2, gs, p, qc), lambda i: (0, i, 0, 0))
    b_spec = pl.BlockSpec((2, gs, c, p), lambda i: (0, i, 0, 0))
    return pl.pallas_call(
        _s5_toeplitz_kernel,
        grid=(g // gs,),
        in_specs=[r_spec, r_spec, b_spec, b_spec, pl.BlockSpec((gs, c, qc), lambda i: (i, 0, 0))],
        out_specs=pl.BlockSpec((gs, qc, qc), lambda i: (i, 0, 0)),
        out_shape=jax.ShapeDtypeStruct((g, qc, qc), BF16),
        compiler_params=_params(("arbitrary",), 32),
        name="s5_toeplitz",
    )(rk_re, rk_im, bbt_re, bbt_im, diag)


def _gelu_tanh(x):
    return 0.5 * x * (1.0 + jnp.tanh(math.sqrt(2.0 / math.pi) * (x + 0.044715 * (x * x * x))))


GROUPS_PER_STEP = LANE // S5_GROUP
STATE_W = GROUPS_PER_STEP * 2 * S5_STATE


def _s5_lane_permutation():
    j = jnp.arange(S5_CHUNK * LANE)
    s, gl, c = j // LANE, (j % LANE) // S5_GROUP, j % S5_GROUP
    k = gl * (S5_CHUNK * S5_GROUP) + s * S5_GROUP + c
    return (k[:, None] == jnp.arange(S5_CHUNK * LANE)[None, :]).astype(BF16)


def _s5_kernel(u_ref, sel_ref, sb_ref, t_ref, e_ref, a_ref, h0_ref, y_ref, hfin_ref, xg, w_scr, h_scr, *,
               batch, nb, nc, seq_len):
    qc = S5_CHUNK * S5_GROUP
    if batch < nb:
        xg[...] = jnp.zeros(xg.shape, BF16)
    for ch in range(nc):
        for s in range(S5_CHUNK):
            xg[ch * nb:ch * nb + batch, s * LANE:(s + 1) * LANE] = (
                u_ref[pl.ds(ch * S5_CHUNK + s, batch, stride=seq_len), :].astype(BF16))
    sel = sel_ref[...]
    ucat = _dot(xg[...], sel).astype(BF16)
    for gl in range(GROUPS_PER_STEP):
        w = _dot(ucat[:, gl * qc:(gl + 1) * qc], sb_ref[gl])
        w_scr[:, gl * LANE:(gl + 1) * LANE] = w[:, 0:LANE]
        w_scr[:, STATE_W + gl * LANE:STATE_W + (gl + 1) * LANE] = w[:, LANE:2 * LANE]
    ar = a_ref[0:1, 0:STATE_W]
    ai = a_ref[0:1, STATE_W:2 * STATE_W]
    fwd_lane = (lax.broadcasted_iota(jnp.int32, (1, STATE_W), 1) & S5_STATE) == 0
    hr = h0_ref[:, 0:STATE_W]
    hi = h0_ref[:, STATE_W:2 * STATE_W]
    for i in range(nc):
        ri = i * nb
        rj = (nc - 1 - i) * nb
        h_scr[ri:ri + nb, 0:STATE_W] = hr
        h_scr[ri:ri + nb, STATE_W:2 * STATE_W] = hi
        h_scr[rj:rj + nb, 2 * STATE_W:3 * STATE_W] = hr
        h_scr[rj:rj + nb, 3 * STATE_W:4 * STATE_W] = hi
        wr = jnp.where(fwd_lane, w_scr[ri:ri + nb, 0:STATE_W], w_scr[rj:rj + nb, 0:STATE_W])
        wi = jnp.where(fwd_lane, w_scr[ri:ri + nb, STATE_W:2 * STATE_W], w_scr[rj:rj + nb, STATE_W:2 * STATE_W])
        hr, hi = ar * hr - ai * hi + wr, ar * hi + ai * hr + wi
    hfin_ref[:, 0:STATE_W] = hr
    hfin_ref[:, STATE_W:2 * STATE_W] = hi
    for gl in range(GROUPS_PER_STEP):
        hcat = jnp.concatenate([h_scr[:, k * STATE_W + gl * LANE:k * STATE_W + (gl + 1) * LANE] for k in range(4)],
                               axis=1).astype(BF16)
        y = _dot(ucat[:, gl * qc:(gl + 1) * qc], t_ref[gl]) + _dot(hcat, e_ref[gl])
        xg[:, gl * qc:(gl + 1) * qc] = _gelu_tanh(y).astype(BF16)
    yp = _dot_t(xg[...], sel)
    for ch in range(nc):
        for s in range(S5_CHUNK):
            y_ref[pl.ds(ch * S5_CHUNK + s, batch, stride=seq_len), :] = (
                yp[ch * nb:ch * nb + batch, s * LANE:(s + 1) * LANE])


def _s5_chunk_rows(batch):
    return batch if batch % (SUBLANE // 2) == 0 else -(-batch // SUBLANE) * SUBLANE


def _s5_stream(u, row_block, batch, seq_len, sel, sb, t_op, e_op, a_planes, h0):
    n_rows = batch * seq_len
    nb = _s5_chunk_rows(batch)
    nc = seq_len // S5_CHUNK
    rows = nb * nc
    nblk = S5_WIDTH // LANE
    qc = S5_CHUNK * S5_GROUP
    body = functools.partial(_s5_kernel, batch=batch, nb=nb, nc=nc, seq_len=seq_len)
    return pl.pallas_call(
        body,
        grid=(nblk,),
        in_specs=[pl.BlockSpec((n_rows, LANE), lambda g: (row_block, g)),
                  pl.BlockSpec((S5_CHUNK * LANE, S5_CHUNK * LANE), lambda g: (0, 0), pipeline_mode=pl.Buffered(1)),
                  pl.BlockSpec((GROUPS_PER_STEP, qc, 4 * S5_STATE), lambda g: (g, 0, 0)),
                  pl.BlockSpec((GROUPS_PER_STEP, qc, qc), lambda g: (g, 0, 0)),
                  pl.BlockSpec((GROUPS_PER_STEP, 8 * S5_STATE, qc), lambda g: (g, 0, 0)),
                  pl.BlockSpec((None, 1, 2 * STATE_W), lambda g: (g, 0, 0)),
                  pl.BlockSpec((None, nb, 2 * STATE_W), lambda g: (g, 0, 0))],
        out_specs=[pl.BlockSpec((n_rows, LANE), lambda g: (0, g)),
                   pl.BlockSpec((None, nb, 2 * STATE_W), lambda g: (g, 0, 0))],
        out_shape=[jax.ShapeDtypeStruct((n_rows, S5_WIDTH), F32),
                   jax.ShapeDtypeStruct((nblk, nb, 2 * STATE_W), F32)],
        scratch_shapes=[pltpu.VMEM((rows, S5_CHUNK * LANE), BF16),
                        pltpu.VMEM((rows, 2 * STATE_W), F32),
                        pltpu.VMEM((rows, 4 * STATE_W), F32)],
        compiler_params=_params(("arbitrary",), 56),
        name="s5_chunked",
    )(u, sel, sb, t_op, e_op, a_planes, h0)


def _rope_tables(length, rot):
    n_freq = rot // 4
    t = jnp.arange(length)
    row = (t // GRID_W).astype(F32)
    col = (t % GRID_W).astype(F32)
    inv = ROPE_BASE ** (-jnp.arange(n_freq, dtype=F32) / n_freq)
    ar, ac = row[:, None] * inv, col[:, None] * inv
    cos = jnp.concatenate([jnp.cos(ar), jnp.cos(ar), jnp.cos(ac), jnp.cos(ac)], axis=-1)
    sin = jnp.concatenate([-jnp.sin(ar), jnp.sin(ar), -jnp.sin(ac), jnp.sin(ac)], axis=-1)
    pad = LANE - rot
    if pad:
        cos = jnp.concatenate([cos, jnp.ones((length, pad), F32)], axis=-1)
        sin = jnp.concatenate([sin, jnp.zeros((length, pad), F32)], axis=-1)
    return cos, sin


def _rope(x, cos, sin, blk):
    lane = lax.broadcasted_iota(jnp.int32, (1, LANE), 1)
    lower = (lane & blk) == 0
    partner = jnp.where(lower, pltpu.roll(x, LANE - blk, 1), pltpu.roll(x, blk, 1))
    return x * cos + partner * sin


def _softmax_pv(scores, values, extra_logit=None):
    m = scores[0].max(axis=-1, keepdims=True)
    for s in scores[1:]:
        m = jnp.maximum(m, s.max(axis=-1, keepdims=True))
    if extra_logit is not None:
        m = jnp.maximum(m, extra_logit)
    den = None
    out = None
    for s, v in zip(scores, values):
        p = jnp.exp(s - m)
        d = p.sum(axis=-1, keepdims=True)
        o = _dot(p.astype(BF16), v)
        den = d if den is None else den + d
        out = o if out is None else out + o
    if extra_logit is not None:
        den = den + jnp.exp(extra_logit - m)
    return out / den


def _ctx_gqa_kernel(sink_ref, q_ref, k_ref, v_ref, o_ref):
    scale = HEAD_DIM ** -0.5
    for g in range(WIN_KV_HEADS):
        k = k_ref[g].astype(BF16)
        v = v_ref[g].astype(BF16)
        for r in range(WIN_GROUP):
            cols = slice((g * WIN_GROUP + r) * HEAD_DIM, (g * WIN_GROUP + r + 1) * HEAD_DIM)
            s = _dot_t(q_ref[:, cols], k) * scale
            o = _softmax_pv([s], [v], sink_ref[g * WIN_GROUP + r])
            o_ref[:, cols] = o.astype(o_ref.dtype)


def _ctx_gqa_attention(uq3, k4, v4, sink, n_prompt, length):
    kv_spec = pl.BlockSpec((None, WIN_KV_HEADS, length, HEAD_DIM), lambda b: (b, 0, 0, 0))
    return pl.pallas_call(
        _ctx_gqa_kernel,
        grid=(n_prompt,),
        in_specs=[pl.BlockSpec(memory_space=pltpu.SMEM),
                  pl.BlockSpec((None, length, WIN_Q_WIDTH), lambda b: (b, 0, 0)),
                  kv_spec, kv_spec],
        out_specs=pl.BlockSpec((None, length, WIN_Q_WIDTH), lambda b: (b, 0, 0)),
        out_shape=jax.ShapeDtypeStruct((n_prompt, length, WIN_Q_WIDTH), BF16),
        compiler_params=_params(("arbitrary",), 40),
        name="ctx_gqa_attention",
    )(sink, uq3, k4, v4)


def _lat_window_kernel(sink_ref, q_ref, k_ref, v_ref, kc_ref, vc_ref, cos_ref, sin_ref, o_ref, *, length):
    g = pl.program_id(1)
    scale = HEAD_DIM ** -0.5
    blk = WIN_RADIUS
    nb = length // blk
    k = _rope(k_ref[...], cos_ref[...], sin_ref[...], HEAD_DIM // 4).astype(BF16)
    v = v_ref[...].astype(BF16)
    kc = kc_ref[...].astype(BF16)
    vc = vc_ref[...].astype(BF16)
    rows = WIN_GROUP * blk
    row = lax.broadcasted_iota(jnp.int32, (rows, 1), 0)
    sk = jnp.zeros((rows, 1), F32)
    for r in range(WIN_GROUP):
        sk = jnp.where(jnp.logical_and(row >= r * blk, row < (r + 1) * blk), sink_ref[g * WIN_GROUP + r], sk)
    qoff = row & (blk - 1)
    for n in range(nb):
        lo = max(0, n - 1) * blk
        hi = min(nb, n + 2) * blk
        cos = cos_ref[n * blk:(n + 1) * blk, :]
        sin = sin_ref[n * blk:(n + 1) * blk, :]
        q = jnp.concatenate(
            [_rope(q_ref[n * blk:(n + 1) * blk, r * HEAD_DIM:(r + 1) * HEAD_DIM].astype(F32), cos, sin, HEAD_DIM // 4)
             for r in range(WIN_GROUP)], axis=0).astype(BF16)
        dist = (n * blk + qoff) - (lo + lax.broadcasted_iota(jnp.int32, (1, hi - lo), 1))
        visible = jnp.logical_and(dist <= WIN_RADIUS, dist >= -WIN_RADIUS)
        s_loc = jnp.where(visible, _dot_t(q, k[lo:hi]) * scale, NEG)
        s_ctx = _dot_t(q, kc) * scale
        o = _softmax_pv([s_loc, s_ctx], [v[lo:hi], vc], sk)
        for r in range(WIN_GROUP):
            o_ref[n * blk:(n + 1) * blk, r * HEAD_DIM:(r + 1) * HEAD_DIM] = o[r * blk:(r + 1) * blk].astype(o_ref.dtype)


def _lat_window_attention(uq3, seq0, k4, v4, k_ctx, v_ctx, sink, cos, sin):
    n_lat, _, length, _ = k4.shape
    qw = WIN_GROUP * HEAD_DIM
    lc = k_ctx.shape[2]
    kv_spec = pl.BlockSpec((None, None, length, HEAD_DIM), lambda b, g: (b, g, 0, 0))
    ctx_spec = pl.BlockSpec((None, None, lc, HEAD_DIM), lambda b, g: (b, g, 0, 0))
    tab_spec = pl.BlockSpec((length, LANE), lambda b, g: (0, 0))
    return pl.pallas_call(
        functools.partial(_lat_window_kernel, length=length),
        grid=(n_lat, WIN_KV_HEADS),
        in_specs=[pl.BlockSpec(memory_space=pltpu.SMEM),
                  pl.BlockSpec((None, length, qw), lambda b, g: (seq0 + b, 0, g)),
                  kv_spec, kv_spec, ctx_spec, ctx_spec, tab_spec, tab_spec],
        out_specs=pl.BlockSpec((None, length, qw), lambda b, g: (b, 0, g)),
        out_shape=jax.ShapeDtypeStruct((n_lat, length, WIN_Q_WIDTH), BF16),
        compiler_params=_params(("arbitrary", "arbitrary"), 48),
        name="latent_window_attention",
    )(sink, uq3, k4, v4, k_ctx, v_ctx, cos, sin)


def _ctx_odd_kernel(q_ref, kv_ref, kpe_ref, qd_ref, kd_ref, vd_ref, o_ref):
    kw = MLA_NOPE + LANE
    kpe = kpe_ref[...].astype(BF16)
    scale_c = (MLA_NOPE + MLA_ROPE) ** -0.5
    for h in range(MLA_HEADS):
        q = q_ref[:, h * kw:(h + 1) * kw]
        k = jnp.concatenate([kv_ref[:, h * kw:h * kw + MLA_NOPE], kpe], axis=1)
        v = kv_ref[:, h * kw + MLA_NOPE:(h + 1) * kw]
        o = _softmax_pv([_dot_t(q, k) * scale_c], [v])
        o_ref[:, h * MLA_V:(h + 1) * MLA_V] = o.astype(o_ref.dtype)
    scale_d = HEAD_DIM ** -0.5
    base = MLA_HEADS * MLA_V
    for h in range(NA_HEADS):
        q = qd_ref[:, h * HEAD_DIM:(h + 1) * HEAD_DIM]
        o = _softmax_pv([_dot_t(q, kd_ref[h].astype(BF16)) * scale_d], [vd_ref[h].astype(BF16)])
        o_ref[:, base + h * HEAD_DIM:base + (h + 1) * HEAD_DIM] = o.astype(o_ref.dtype)


def _ctx_odd_attention(q3, kv3, kpe3, qd3, kd4, vd4, n_prompt, length):
    qn = q3.shape[-1]
    head_spec = pl.BlockSpec((None, NA_HEADS, length, HEAD_DIM), lambda b: (b, 0, 0, 0))
    return pl.pallas_call(
        _ctx_odd_kernel,
        grid=(n_prompt,),
        in_specs=[pl.BlockSpec((None, length, qn), lambda b: (b, 0, 0)),
                  pl.BlockSpec((None, length, qn), lambda b: (b, 0, 0)),
                  pl.BlockSpec((None, length, LANE), lambda b: (b, 0, 0)),
                  pl.BlockSpec((None, length, NA_WIDTH), lambda b: (b, 0, 0)),
                  head_spec, head_spec],
        out_specs=pl.BlockSpec((None, length, MLA_HEADS * MLA_V + NA_WIDTH), lambda b: (b, 0, 0)),
        out_shape=jax.ShapeDtypeStruct((n_prompt, length, MLA_HEADS * MLA_V + NA_WIDTH), BF16),
        compiler_params=_params(("arbitrary",), 48),
        name="ctx_odd_attention",
    )(q3, kv3, kpe3, qd3, kd4, vd4)


def _lat_mla_kernel(q_ref, kv_ref, kpe_ref, kvc_ref, kpec_ref, cos_ref, sin_ref, o_ref, *, length):
    scale = (MLA_NOPE + MLA_ROPE) ** -0.5
    rb = MLA_ROPE // 4
    hw = MLA_NOPE + LANE
    kpe = _rope(kpe_ref[...], cos_ref[...], sin_ref[...], rb).astype(BF16)
    kpe_ctx = kpec_ref[...].astype(BF16)
    qb = 256
    for hh in range(ATTN_HEADS_PER_STEP):
        c0 = hh * hw
        k_lat = jnp.concatenate([kv_ref[:, c0:c0 + MLA_NOPE], kpe], axis=1)
        v_lat = kv_ref[:, c0 + MLA_NOPE:c0 + hw]
        k_ctx = jnp.concatenate([kvc_ref[:, c0:c0 + MLA_NOPE], kpe_ctx], axis=1)
        v_ctx = kvc_ref[:, c0 + MLA_NOPE:c0 + hw]
        for n in range(length // qb):
            rows = slice(n * qb, (n + 1) * qb)
            q_pe = _rope(q_ref[rows, c0 + MLA_NOPE:c0 + hw].astype(F32), cos_ref[rows, :], sin_ref[rows, :],
                         rb).astype(BF16)
            q = jnp.concatenate([q_ref[rows, c0:c0 + MLA_NOPE], q_pe], axis=1)
            o = _softmax_pv([_dot_t(q, k_lat) * scale, _dot_t(q, k_ctx) * scale], [v_lat, v_ctx])
            o_ref[rows, hh * MLA_V:(hh + 1) * MLA_V] = o.astype(o_ref.dtype)


def _lat_mla_attention(q3, kv3, kvc3, kpe3, kpe_ctx, seq0, n_lat, length, cos, sin):
    kw = ATTN_HEADS_PER_STEP * (MLA_NOPE + LANE)
    lc = kpe_ctx.shape[1]
    return pl.pallas_call(
        functools.partial(_lat_mla_kernel, length=length),
        grid=(n_lat, MLA_HEADS // ATTN_HEADS_PER_STEP),
        in_specs=[pl.BlockSpec((None, length, kw), lambda b, h: (seq0 + b, 0, h)),
                  pl.BlockSpec((None, length, kw), lambda b, h: (seq0 + b, 0, h)),
                  pl.BlockSpec((None, length, LANE), lambda b, h: (seq0 + b, 0, 0)),
                  pl.BlockSpec((None, lc, kw), lambda b, h: (b, 0, h)),
                  pl.BlockSpec((None, lc, LANE), lambda b, h: (b, 0, 0)),
                  pl.BlockSpec((length, LANE), lambda b, h: (0, 0)),
                  pl.BlockSpec((length, LANE), lambda b, h: (0, 0))],
        out_specs=pl.BlockSpec((None, length, ATTN_HEADS_PER_STEP * MLA_V), lambda b, h: (b, 0, h)),
        out_shape=jax.ShapeDtypeStruct((n_lat, length, MLA_HEADS * MLA_V), BF16),
        compiler_params=_params(("arbitrary", "arbitrary"), 48),
        name="latent_mla_attention",
    )(q3, kv3, kpe3, kvc3, kpe_ctx, cos, sin)


def _na_row_start(r, rows):
    kr = min(NA_ROWS, rows)
    return min(max(r - kr // 2, 0), rows - kr)


def _lat_na_kernel(q_ref, k_ref, v_ref, kc_ref, vc_ref, bias_ref, o_ref, *, length):
    scale = HEAD_DIM ** -0.5
    rows = length // GRID_W
    kr = min(NA_ROWS, rows)
    for hh in range(ATTN_HEADS_PER_STEP):
        cols = slice(hh * HEAD_DIM, (hh + 1) * HEAD_DIM)
        k = k_ref[hh].astype(BF16)
        v = v_ref[hh].astype(BF16)
        kc = kc_ref[hh].astype(BF16)
        vc = vc_ref[hh].astype(BF16)
        r = 0
        while r < rows:
            r_end = r + 1
            while r_end < rows and _na_row_start(r_end, rows) == _na_row_start(r, rows):
                r_end += 1
            r0 = _na_row_start(r, rows) * GRID_W
            nq = (r_end - r) * GRID_W
            q = q_ref[r * GRID_W:r_end * GRID_W, cols]
            bias = bias_ref[hh, r:r_end].reshape(nq, kr * GRID_W)
            s_nb = _dot_t(q, k[r0:r0 + kr * GRID_W]) * scale + bias
            s_ctx = _dot_t(q, kc) * scale
            o = _softmax_pv([s_nb, s_ctx], [v[r0:r0 + kr * GRID_W], vc])
            o_ref[r * GRID_W:r_end * GRID_W, cols] = o.astype(o_ref.dtype)
            r = r_end


def _na_bias(rpb, length):
    rows = length // GRID_W
    kr = min(NA_ROWS, rows)
    col = jnp.arange(GRID_W)
    c_start = jnp.clip(col - NA_COLS // 2, 0, GRID_W - NA_COLS)
    col_valid = (col[None, :] >= c_start[:, None]) & (col[None, :] < c_start[:, None] + NA_COLS)
    off_c = jnp.clip(col[None, :] - col[:, None], -(NA_COLS - 1), NA_COLS - 1) + NA_COLS - 1
    onehot = (off_c[:, :, None] == jnp.arange(2 * NA_COLS - 1)[None, None, :]).astype(F32)
    table = jnp.einsum('hdj,qkj->hdqk', rpb.astype(F32), onehot, precision=lax.Precision.HIGHEST)
    table = jnp.where(col_valid[None, None], table, NEG)
    per_row = []
    for r in range(rows):
        r0 = _na_row_start(r, rows)
        per_row.append(jnp.concatenate([table[:, r0 + j - r + NA_ROWS - 1] for j in range(kr)], axis=-1))
    return jnp.stack(per_row, axis=1)


def _lat_na_attention(qd3, seq0, k4, v4, k_ctx, v_ctx, bias):
    n_lat, _, length, _ = k4.shape
    rows = length // GRID_W
    kr = min(NA_ROWS, rows)
    lc = k_ctx.shape[2]
    hp = ATTN_HEADS_PER_STEP
    kv_spec = pl.BlockSpec((None, hp, length, HEAD_DIM), lambda h, b: (b, h, 0, 0))
    ctx_spec = pl.BlockSpec((None, hp, lc, HEAD_DIM), lambda h, b: (b, h, 0, 0))
    return pl.pallas_call(
        functools.partial(_lat_na_kernel, length=length),
        grid=(NA_HEADS // hp, n_lat),
        in_specs=[pl.BlockSpec((None, length, hp * HEAD_DIM), lambda h, b: (seq0 + b, 0, h)),
                  kv_spec, kv_spec, ctx_spec, ctx_spec,
                  pl.BlockSpec((hp, rows, GRID_W, kr * GRID_W), lambda h, b: (h, 0, 0, 0))],
        out_specs=pl.BlockSpec((None, length, hp * HEAD_DIM), lambda h, b: (b, 0, h)),
        out_shape=jax.ShapeDtypeStruct((n_lat, length, NA_WIDTH), BF16),
        compiler_params=_params(("arbitrary", "arbitrary"), 48),
        name="latent_neighborhood_attention",
    )(qd3, k4, v4, k_ctx, v_ctx, bias)


def kernel(x_prompt, x_sample, cache_l0_k, cache_l0_v, state_l0_re, state_l0_im, cache_l1_ckv, cache_l1_kpe, cache_l1_k, cache_l1_v, c, c_ctx, l0_ada_w, l0_ada_b, l0_norm1, l0_norm2, l0_w_in, l0_lambda_re, l0_lambda_im, l0_log_dt, l0_b_re, l0_b_im, l0_c_re, l0_c_im, l0_d_skip, l0_w_glu, l0_b_glu, l0_sink, l0_w_out, l0_ffn_w_up, l0_ffn_conv_w, l0_ffn_conv_b, l0_ffn_w_down, l1_ada_w, l1_ada_b, l1_norm1, l1_norm2, l1_w_in, l1_q_norm, l1_kv_norm, l1_w_uq, l1_w_ukv, l1_rpb, l1_w_out, l1_ffn_w_up, l1_ffn_conv_w, l1_ffn_conv_b, l1_ffn_w_down, final_norm):
    n_p, len_p, d = x_prompt.shape
    n_s, len_s, _ = x_sample.shape
    rows_p, rows_s = n_p * len_p, n_s * len_s
    t = rows_p + rows_s
    assert d == D_MODEL and len_s == ROW_TILE and ROW_TILE % len_p == 0 and n_s + 1 <= MAX_CONDS
    assert rows_p % ROW_TILE == 0
    assert len_p % S5_CHUNK == 0 and len_s % (2 * WIN_RADIUS) == 0 and len_s % GRID_W == 0
    seqs_p_units = t // len_p
    seqs_s_units = t // len_s
    seq0_s = rows_p // len_s
    groups = dict(n_prompt_rows=rows_p, lat_len=len_s)
    half = D_MODEL // 2

    cond = jnp.zeros((MAX_CONDS, d), F32).at[0].set(c_ctx).at[1:1 + n_s].set(c)
    mod0 = _modulation(cond, l0_ada_w, l0_ada_b)
    mod1 = _modulation(cond, l1_ada_w, l1_ada_b)

    x_parts = [x_prompt.reshape(rows_p, d), x_sample.reshape(rows_s, d)]

    def ffn(x, mod, norm2, w_up, conv_w, conv_b, w_down):
        h = _norm_mod([x], norm2, mod, 3, 4, **groups)
        act = _ffn_up(h, w_up, conv_w, conv_b, rows_p, len_p, len_s)
        return _matmul_residual([[(act, 0)]], w_down.astype(BF16), [x], mod, 5, 512, name="ffn_down",
                                bm=512, **groups)

    h = _norm_mod(x_parts, l0_norm1, mod0, 0, 1, **groups)
    u = _matmul(h, l0_w_in, F32, 512, "l0_in_proj_u", 0, S5_WIDTH)
    q0 = _matmul(h, l0_w_in, BF16, 512, "l0_in_proj_q", S5_WIDTH, WIN_Q_WIDTH)
    kcol = S5_WIDTH + WIN_Q_WIDTH
    vcol = kcol + WIN_KV_WIDTH
    out_k0 = _matmul_heads(h, l0_w_in, kcol, WIN_KV_HEADS, 0, rows_p, len_p, "l0_in_proj_k_ctx")
    out_v0 = _matmul_heads(h, l0_w_in, vcol, WIN_KV_HEADS, 0, rows_p, len_p, "l0_in_proj_v_ctx")
    k0_s = _matmul_heads(h, l0_w_in, kcol, WIN_KV_HEADS, rows_p, rows_s, len_s, "l0_in_proj_k_lat")
    v0_s = _matmul_heads(h, l0_w_in, vcol, WIN_KV_HEADS, rows_p, rows_s, len_s, "l0_in_proj_v_lat")

    assert rows_p % rows_s == 0
    sel = _s5_lane_permutation()
    sb, t_op, e_op, a16 = _s5_operators(l0_lambda_re, l0_lambda_im, l0_log_dt, l0_b_re, l0_b_im, l0_c_re, l0_c_im, l0_d_skip)
    nblk, gps, pw = S5_WIDTH // LANE, GROUPS_PER_STEP, 2 * S5_STATE

    def planes(cols):
        b = cols.shape[1]
        c5 = cols.reshape(nblk, gps, b, 2, pw).transpose(0, 2, 3, 1, 4)
        return c5.reshape(nblk, b, 2 * STATE_W)

    def state_cols(s):
        return s.astype(F32).transpose(2, 0, 1, 3).reshape(S5_GROUPS, s.shape[0], pw)

    np8, ns8 = _s5_chunk_rows(n_p), _s5_chunk_rows(n_s)
    a_planes = planes(a16)
    h0_lat = planes(jnp.concatenate([state_cols(state_l0_re), state_cols(state_l0_im)], axis=-1))
    h0_s = jnp.pad(h0_lat, ((0, 0), (0, ns8 - n_s), (0, 0)))
    h0_p = jnp.zeros((nblk, np8, 2 * STATE_W), F32)
    y_p, hfin = _s5_stream(u, 0, n_p, len_p, sel, sb, t_op, e_op, a_planes, h0_p)
    y_s, _ = _s5_stream(u, rows_p // rows_s, n_s, len_s, sel, sb, t_op, e_op, a_planes, h0_s)
    a_out = _glu([y_p, y_s], l0_w_glu, l0_b_glu, rows_p)

    def state_out(plane):
        return (plane[:, :n_p].reshape(nblk, n_p, gps, 2, S5_STATE).transpose(1, 3, 0, 2, 4)
                .reshape(n_p, 2, S5_GROUPS, S5_STATE))

    out_sre = state_out(hfin[:, :, :STATE_W])
    out_sim = state_out(hfin[:, :, STATE_W:])

    cos_e, sin_e = _rope_tables(len_s, HEAD_DIM)
    o_p = _ctx_gqa_attention(q0.reshape(seqs_p_units, len_p, -1), out_k0, out_v0, l0_sink, n_p, len_p)
    o_s = _lat_window_attention(q0.reshape(seqs_s_units, len_s, -1), seq0_s, k0_s, v0_s,
                                cache_l0_k, cache_l0_v, l0_sink, cos_e, sin_e)
    x = _matmul_residual([[(a_out, 0)], [(o_p.reshape(rows_p, half), 0), (o_s.reshape(rows_s, half), 0)]],
                         l0_w_out.astype(BF16), x_parts, mod0, 2, 512, name="l0_out_proj", **groups)
    x = ffn(x, mod0, l0_norm2, l0_ffn_w_up, l0_ffn_conv_w, l0_ffn_conv_b, l0_ffn_w_down)

    s1, s2 = MLA_Q_LORA + MLA_KV_LORA, MLA_Q_LORA + MLA_KV_LORA + MLA_ROPE
    w_lora = l1_w_in[:, :s1].astype(BF16)
    w_dkv = l1_w_in[:, s2:].astype(BF16)
    w_kpe = jnp.pad(l1_w_in[:, s1:s2], ((0, 0), (0, LANE - MLA_ROPE))).astype(BF16)
    w_uq = l1_w_uq.reshape(MLA_Q_LORA, MLA_HEADS, MLA_NOPE + MLA_ROPE)
    w_uq = jnp.pad(w_uq, ((0, 0), (0, 0), (0, LANE - MLA_ROPE))).reshape(MLA_Q_LORA, -1).astype(BF16)

    h = _norm_mod([x], l1_norm1, mod1, 0, 1, **groups)
    cqkv = _matmul(h, w_lora, F32, 512, "l1_in_proj_lora")
    kpe = _matmul(h, w_kpe, F32, LANE, "l1_in_proj_kpe")
    qd = _matmul(h, w_dkv, BF16, 512, "l1_in_proj_qd", 0, NA_WIDTH)
    out_k1 = _matmul_heads(h, w_dkv, NA_WIDTH, NA_HEADS, 0, rows_p, len_p, "l1_in_proj_kd_ctx")
    out_v1 = _matmul_heads(h, w_dkv, 2 * NA_WIDTH, NA_HEADS, 0, rows_p, len_p, "l1_in_proj_vd_ctx")
    k1_s = _matmul_heads(h, w_dkv, NA_WIDTH, NA_HEADS, rows_p, rows_s, len_s, "l1_in_proj_kd_lat")
    v1_s = _matmul_heads(h, w_dkv, 2 * NA_WIDTH, NA_HEADS, rows_p, rows_s, len_s, "l1_in_proj_vd_lat")
    cqn = _rmsnorm_cols(cqkv, l1_q_norm, 0, MLA_Q_LORA, BF16)
    ckvn = _rmsnorm_cols(cqkv, l1_kv_norm, MLA_Q_LORA // MLA_KV_LORA, MLA_KV_LORA, F32)
    q_all = _matmul(cqn, w_uq, BF16, 2048, "l1_q_up")
    w_ukv = l1_w_ukv.astype(BF16)
    kv_all = _matmul(ckvn, w_ukv, BF16, 2048, "l1_kv_up")
    lc = cache_l1_ckv.shape[1]
    kv_ctx = _matmul(cache_l1_ckv.reshape(-1, MLA_KV_LORA), w_ukv, BF16, 2048, "l1_kv_up_ctx")

    out_ckv = ckvn[:rows_p].reshape(n_p, len_p, MLA_KV_LORA)
    out_kpe = kpe[:rows_p, :MLA_ROPE].reshape(n_p, len_p, MLA_ROPE)

    o_p = _ctx_odd_attention(q_all.reshape(seqs_p_units, len_p, -1),
                             kv_all.reshape(seqs_p_units, len_p, -1),
                             kpe.reshape(seqs_p_units, len_p, LANE),
                             qd.reshape(seqs_p_units, len_p, -1), out_k1, out_v1, n_p, len_p)
    cos_o, sin_o = _rope_tables(len_s, MLA_ROPE)
    kpe_ctx = jnp.pad(cache_l1_kpe, ((0, 0), (0, 0), (0, LANE - MLA_ROPE)))
    oc_s = _lat_mla_attention(q_all.reshape(seqs_s_units, len_s, -1),
                              kv_all.reshape(seqs_s_units, len_s, -1),
                              kv_ctx.reshape(n_s, lc, -1),
                              kpe.reshape(seqs_s_units, len_s, LANE), kpe_ctx,
                              seq0_s, n_s, len_s, cos_o, sin_o)
    od_s = _lat_na_attention(qd.reshape(seqs_s_units, len_s, -1), seq0_s, k1_s, v1_s,
                             cache_l1_k, cache_l1_v, _na_bias(l1_rpb, len_s))
    o_p2 = o_p.reshape(rows_p, 2 * half)
    x = _matmul_residual([[(o_p2, 0), (oc_s.reshape(rows_s, half), 0)],
                          [(o_p2, 1), (od_s.reshape(rows_s, half), 0)]],
                         l1_w_out.astype(BF16), [x], mod1, 2, 512, name="l1_out_proj", **groups)
    x = ffn(x, mod1, l1_norm2, l1_ffn_w_up, l1_ffn_conv_w, l1_ffn_conv_b, l1_ffn_w_down)

    y_prompt = _rmsnorm_cols(x, final_norm, 0, d, F32, 0, rows_p).reshape(n_p, len_p, d)
    y_sample = _rmsnorm_cols(x, final_norm, 0, d, F32, rows_p, rows_s).reshape(n_s, len_s, d)
    return (y_prompt, y_sample, out_k0, out_v0, out_sre, out_sim, out_ckv, out_kpe, out_k1, out_v1)
```

```python
import functools
import math

import jax
import jax.numpy as jnp
from jax import lax
from jax.experimental import pallas as pl
from jax.experimental.pallas import tpu as pltpu

F32 = jnp.float32
BF16 = jnp.bfloat16

D_MODEL = 4096
GRID_W = 64
HEAD_DIM = 128
S5_WIDTH = 2048
S5_GROUP = 16
S5_GROUPS = S5_WIDTH // S5_GROUP
S5_STATE = 64
S5_CHUNK = 16
WIN_HEADS = 16
WIN_KV_HEADS = 4
WIN_GROUP = WIN_HEADS // WIN_KV_HEADS
WIN_RADIUS = 128
WIN_Q_WIDTH = WIN_HEADS * HEAD_DIM
WIN_KV_WIDTH = WIN_KV_HEADS * HEAD_DIM
MLA_HEADS = 16
MLA_Q_LORA = 1024
MLA_KV_LORA = 512
MLA_NOPE = 128
MLA_ROPE = 64
MLA_V = 128
NA_HEADS = 16
NA_ROWS = 8
NA_COLS = 16
NA_WIDTH = NA_HEADS * HEAD_DIM
D_FF = 11008
ROPE_BASE = 10000.0
EPS = 1e-6
NEG = -1e30

LANE = 128
SUBLANE = 8
MAX_CONDS = 8
ROW_TILE = 1024
ATTN_HEADS_PER_STEP = 2


def _params(sem, vmem_mb):
    return pltpu.CompilerParams(dimension_semantics=sem, vmem_limit_bytes=vmem_mb * 1024 * 1024)


def _cond_index(i, bm, n_prompt_rows, lat_len):
    first = n_prompt_rows // bm
    per = lat_len // bm
    return jnp.where(i < first, 0, 1 + (i - first) // per)


def _dot(a, b):
    return jnp.dot(a, b, preferred_element_type=F32)


def _dot_t(a, b):
    return lax.dot_general(a, b, (((1,), (1,)), ((), ())), preferred_element_type=F32)


def _sigmoid(x):
    return 1.0 / (1.0 + jnp.exp(-x))


def _row_specs(parts, bm, width, col_fn, n_prompt_blocks, single_buffer=False):
    mode = dict(pipeline_mode=pl.Buffered(1)) if single_buffer else {}
    if len(parts) == 1:
        cb = parts[0][1]
        return [pl.BlockSpec((bm, width), lambda i, j: (i, col_fn(j, cb)), **mode)]
    cb_p, cb_s = parts[0][1], parts[1][1]
    last_p = n_prompt_blocks - 1
    return [pl.BlockSpec((bm, width), lambda i, j: (jnp.minimum(i, last_p), col_fn(j, cb_p)), **mode),
            pl.BlockSpec((bm, width), lambda i, j: (jnp.maximum(i - n_prompt_blocks, 0), col_fn(j, cb_s)), **mode)]


def _by_stream(i, n_prompt_blocks, operands, body):
    if all(len(o) == 1 for o in operands):
        body([o[0] for o in operands])
        return

    @pl.when(i < n_prompt_blocks)
    def _():
        body([o[0] for o in operands])

    @pl.when(i >= n_prompt_blocks)
    def _():
        body([o[-1] for o in operands])


def _mod_kernel(c_ref, w_ref, b_ref, o_ref):
    c = c_ref[...]
    s = c * _sigmoid(c)
    o_ref[...] = _dot(s.astype(BF16), w_ref[...].astype(BF16)) + b_ref[...]


def _modulation(cond, w, b):
    d, n = w.shape
    bn = 512
    out = pl.pallas_call(
        _mod_kernel,
        grid=(n // bn,),
        in_specs=[pl.BlockSpec((MAX_CONDS, d), lambda j: (0, 0)),
                  pl.BlockSpec((d, bn), lambda j: (0, j)),
                  pl.BlockSpec((1, bn), lambda j: (0, j))],
        out_specs=pl.BlockSpec((MAX_CONDS, bn), lambda j: (0, j)),
        out_shape=jax.ShapeDtypeStruct((MAX_CONDS, n), F32),
        compiler_params=_params(("arbitrary",), 40),
        name="modulation",
    )(cond, w, b.reshape(1, n))
    return out.reshape(MAX_CONDS, 1, n)


def _norm_mod_kernel(*refs, n_x, n_prompt_blocks):
    x_refs, (g_ref, sh_ref, sc_ref, o_ref) = refs[:n_x], refs[n_x:]

    def body(r):
        x = r[0][...]
        ms = jnp.mean(x * x, axis=-1, keepdims=True)
        y = x * lax.rsqrt(ms + EPS) * g_ref[...]
        o_ref[...] = (y * (1.0 + sc_ref[...]) + sh_ref[...]).astype(o_ref.dtype)

    _by_stream(pl.program_id(0), n_prompt_blocks, [x_refs], body)


def _norm_mod(x_parts, gain, mod, shift_slot, scale_slot, n_prompt_rows, lat_len):
    d = D_MODEL
    bm = 512
    t = sum(x.shape[0] for x in x_parts)
    assert n_prompt_rows % bm == 0 and lat_len % bm == 0
    npb = n_prompt_rows // bm
    cidx = functools.partial(_cond_index, bm=bm, n_prompt_rows=n_prompt_rows, lat_len=lat_len)
    specs = _row_specs([(x, 0) for x in x_parts], bm, d, lambda j, cb: 0, npb)
    to1d = lambda spec: pl.BlockSpec(spec.block_shape, lambda i, f=spec.index_map: f(i, 0))
    return pl.pallas_call(
        functools.partial(_norm_mod_kernel, n_x=len(x_parts), n_prompt_blocks=npb),
        grid=(t // bm,),
        in_specs=[to1d(s) for s in specs] + [
            pl.BlockSpec((1, d), lambda i: (0, 0)),
            pl.BlockSpec((None, 1, d), lambda i: (cidx(i), 0, shift_slot)),
            pl.BlockSpec((None, 1, d), lambda i: (cidx(i), 0, scale_slot))],
        out_specs=pl.BlockSpec((bm, d), lambda i: (i, 0)),
        out_shape=jax.ShapeDtypeStruct((t, d), BF16),
        compiler_params=_params(("arbitrary",), 56),
        name="norm_mod",
    )(*x_parts, gain.reshape(1, d), mod, mod)


def _rmsnorm_kernel(x_ref, g_ref, o_ref):
    x = x_ref[...].astype(F32)
    ms = jnp.mean(x * x, axis=-1, keepdims=True)
    o_ref[...] = (x * lax.rsqrt(ms + EPS) * g_ref[...]).astype(o_ref.dtype)


def _rmsnorm_cols(x, gain, col_block, width, out_dtype, row0=0, rows=None):
    bm = 256
    rows = x.shape[0] if rows is None else rows
    rb0 = row0 // bm
    return pl.pallas_call(
        _rmsnorm_kernel,
        grid=(rows // bm,),
        in_specs=[pl.BlockSpec((bm, width), lambda i: (i + rb0, col_block)),
                  pl.BlockSpec((1, width), lambda i: (0, 0))],
        out_specs=pl.BlockSpec((bm, width), lambda i: (i, 0)),
        out_shape=jax.ShapeDtypeStruct((rows, width), out_dtype),
        compiler_params=_params(("arbitrary",), 40),
        name="rmsnorm",
    )(x, gain.reshape(1, width))


def _mm_kernel(x_ref, w_ref, o_ref):
    o_ref[...] = _dot(x_ref[...].astype(BF16), w_ref[...].astype(BF16)).astype(o_ref.dtype)


def _matmul(x, w, out_dtype, bn, name, col0=0, ncols=None):
    m, k = x.shape
    ncols = w.shape[1] - col0 if ncols is None else ncols
    bm = min(ROW_TILE, m)
    assert m % bm == 0 and ncols % bn == 0 and col0 % bn == 0
    cb0 = col0 // bn
    return pl.pallas_call(
        _mm_kernel,
        grid=(m // bm, ncols // bn),
        in_specs=[pl.BlockSpec((bm, k), lambda i, j: (i, 0)),
                  pl.BlockSpec((k, bn), lambda i, j: (0, cb0 + j))],
        out_specs=pl.BlockSpec((bm, bn), lambda i, j: (i, j)),
        out_shape=jax.ShapeDtypeStruct((m, ncols), out_dtype),
        compiler_params=_params(("arbitrary", "arbitrary"), 56),
        name=name,
    )(x, w)


def _mm_heads_kernel(x_ref, w_ref, o_ref, *, seqs, seq_len, heads):
    acc = _dot(x_ref[...], w_ref[...].astype(BF16))
    for b in range(seqs):
        for hh in range(heads):
            o_ref[b, hh] = acc[b * seq_len:(b + 1) * seq_len, hh * HEAD_DIM:(hh + 1) * HEAD_DIM]


def _matmul_heads(x, w, col0, heads, row0, rows, seq_len, name):
    k = x.shape[1]
    bm = ROW_TILE
    hb = 4
    bn = hb * HEAD_DIM
    assert rows % bm == 0 and row0 % bm == 0 and bm % seq_len == 0 and heads % hb == 0 and col0 % bn == 0
    seqs = bm // seq_len
    rb0, cb0 = row0 // bm, col0 // bn
    return pl.pallas_call(
        functools.partial(_mm_heads_kernel, seqs=seqs, seq_len=seq_len, heads=hb),
        grid=(rows // bm, heads // hb),
        in_specs=[pl.BlockSpec((bm, k), lambda i, j: (rb0 + i, 0)),
                  pl.BlockSpec((k, bn), lambda i, j: (0, cb0 + j))],
        out_specs=pl.BlockSpec((seqs, hb, seq_len, HEAD_DIM), lambda i, j: (i, j, 0, 0)),
        out_shape=jax.ShapeDtypeStruct((rows // seq_len, heads, seq_len, HEAD_DIM), F32),
        compiler_params=_params(("arbitrary", "arbitrary"), 56),
        name=name,
    )(x, w)


def _mm_res_kernel(*refs, n_lhs, n_parts, n_res, n_prompt_blocks):
    pos = 0
    lhs = []
    for n in n_lhs:
        lhs.append(refs[pos:pos + n])
        pos += n
    n_w = 1 if n_parts > 1 else n_parts
    w_refs = refs[pos:pos + n_w]
    pos += n_w
    res = refs[pos:pos + n_res]
    pos += n_res
    gate_ref, o_ref = refs[pos], refs[pos + 1]
    i = pl.program_id(0)

    if n_parts == 1:
        def body(r):
            o_ref[...] = r[1][...] + gate_ref[...] * _dot(r[0][...], w_refs[0][...].astype(BF16))

        _by_stream(i, n_prompt_blocks, lhs + [res], body)
        return

    lhs_scr = refs[pos + 2]
    kq = lhs[0][0].shape[1]

    @pl.when(pl.program_id(1) == 0)
    def _():
        def stage(r):
            for q, x_ref in enumerate(r):
                lhs_scr[:, q * kq:(q + 1) * kq] = x_ref[...]

        _by_stream(i, n_prompt_blocks, lhs, stage)

    acc = _dot(lhs_scr[...], w_refs[0][...].astype(BF16))

    def finish(r):
        o_ref[...] = r[0][...] + gate_ref[...] * acc

    _by_stream(i, n_prompt_blocks, [res], finish)


def _matmul_residual(lhs_parts, w, res_parts, mod, gate_slot, bn, n_prompt_rows, lat_len, name,
                     single_buffer_x=False, vmem_mb=56, bm=ROW_TILE):
    n = w.shape[1]
    kq = w.shape[0] // len(lhs_parts)
    t = sum(r.shape[0] for r in res_parts)
    npb = n_prompt_rows // bm
    cidx = functools.partial(_cond_index, bm=bm, n_prompt_rows=n_prompt_rows, lat_len=lat_len)
    gate_col0 = gate_slot * (D_MODEL // bn)
    in_specs, args = [], []
    for parts in lhs_parts:
        in_specs += _row_specs(parts, bm, kq, lambda j, cb: cb, npb, single_buffer=single_buffer_x)
        args += [a for a, _ in parts]
    staged = len(lhs_parts) > 1
    in_specs.append(pl.BlockSpec((w.shape[0], bn), lambda i, j: (0, j)))
    args.append(w)
    in_specs += _row_specs([(r, 0) for r in res_parts], bm, bn, lambda j, cb: j, npb)
    args += list(res_parts)
    in_specs.append(pl.BlockSpec((None, 1, bn), lambda i, j: (cidx(i), 0, gate_col0 + j)))
    args.append(mod)
    body = functools.partial(_mm_res_kernel, n_lhs=tuple(len(p) for p in lhs_parts), n_parts=len(lhs_parts),
                             n_res=len(res_parts), n_prompt_blocks=npb)
    return pl.pallas_call(
        body,
        grid=(t // bm, n // bn),
        in_specs=in_specs,
        out_specs=pl.BlockSpec((bm, bn), lambda i, j: (i, j)),
        out_shape=jax.ShapeDtypeStruct((t, n), F32),
        scratch_shapes=[pltpu.VMEM((bm, w.shape[0]), BF16)] if staged else [],
        compiler_params=_params(("arbitrary", "arbitrary"), vmem_mb),
        name=name,
    )(*args)


def _glu_kernel(*refs, n_y, n_prompt_blocks):
    y_refs = refs[:n_y]
    w_ref, b_ref, o_ref = refs[n_y:]

    def body(r):
        y = r[0][...]
        z = _dot(y.astype(BF16), w_ref[...]) + b_ref[...]
        o_ref[...] = (y * _sigmoid(z)).astype(o_ref.dtype)

    _by_stream(pl.program_id(0), n_prompt_blocks, [y_refs], body)


def _glu(y_parts, w, b, n_prompt_rows):
    k, n = w.shape
    bm = ROW_TILE
    m = sum(y.shape[0] for y in y_parts)
    npb = n_prompt_rows // bm
    parts = [(y, 0) for y in y_parts]
    return pl.pallas_call(
        functools.partial(_glu_kernel, n_y=len(y_parts), n_prompt_blocks=npb),
        grid=(m // bm, 1),
        in_specs=(_row_specs(parts, bm, k, lambda j, cb: 0, npb)
                  + [pl.BlockSpec((k, n), lambda i, j: (0, 0), pipeline_mode=pl.Buffered(1)),
                     pl.BlockSpec((1, n), lambda i, j: (0, 0))]),
        out_specs=pl.BlockSpec((bm, n), lambda i, j: (i, 0)),
        out_shape=jax.ShapeDtypeStruct((m, n), BF16),
        compiler_params=_params(("arbitrary", "arbitrary"), 56),
        name="s5_glu",
    )(*y_parts, w.astype(BF16), b.reshape(1, n))


def _ffn_up_kernel(x_ref, wg_ref, wv_ref, cwg_ref, cwv_ref, cbg_ref, cbv_ref, o_ref, *,
                   sub, n_sub, prompt_blocks, prompt_len, lat_len):
    i = pl.program_id(0)
    row = lax.broadcasted_iota(jnp.int32, (sub, 1), 0)
    wg = wg_ref[...].astype(BF16)
    wv = wv_ref[...].astype(BF16)

    for s in range(n_sub):
        period = jnp.where(i * n_sub + s < prompt_blocks, prompt_len, lat_len)
        pos = row & (period - 1)
        first = pos == 0
        last = pos == period - 1

        def conv(u, cw_ref, cb_ref):
            prev = jnp.where(first, 0.0, pltpu.roll(u, 1, 0))
            nxt = jnp.where(last, 0.0, pltpu.roll(u, sub - 1, 0))
            return prev * cw_ref[0:1, :] + u * cw_ref[1:2, :] + nxt * cw_ref[2:3, :] + cb_ref[...]

        x = x_ref[s * sub:(s + 1) * sub, :]
        g = conv(_dot(x, wg), cwg_ref, cbg_ref)
        v = conv(_dot(x, wv), cwv_ref, cbv_ref)
        o_ref[s * sub:(s + 1) * sub, :] = (g * _sigmoid(g) * v).astype(o_ref.dtype)


FFN_SUB_BLOCKS = 4


def _ffn_up(h, w_up, conv_w, conv_b, n_prompt_rows, prompt_len, lat_len):
    t, d = h.shape
    sub, bn = lat_len, 256
    n_sub = FFN_SUB_BLOCKS if (t // sub) % FFN_SUB_BLOCKS == 0 else 2
    bm = sub * n_sub
    assert t % bm == 0 and n_prompt_rows % sub == 0 and sub % prompt_len == 0
    assert prompt_len & (prompt_len - 1) == 0 and lat_len & (lat_len - 1) == 0
    nj = D_FF // bn
    conv_b = conv_b.reshape(1, 2 * D_FF)
    body = functools.partial(_ffn_up_kernel, sub=sub, n_sub=n_sub, prompt_blocks=n_prompt_rows // sub,
                             prompt_len=prompt_len, lat_len=lat_len)
    return pl.pallas_call(
        body,
        grid=(t // bm, nj),
        in_specs=[pl.BlockSpec((bm, d), lambda i, j: (i, 0), pipeline_mode=pl.Buffered(1)),
                  pl.BlockSpec((d, bn), lambda i, j: (0, j)),
                  pl.BlockSpec((d, bn), lambda i, j: (0, nj + j)),
                  pl.BlockSpec((3, bn), lambda i, j: (0, j)),
                  pl.BlockSpec((3, bn), lambda i, j: (0, nj + j)),
                  pl.BlockSpec((1, bn), lambda i, j: (0, j)),
                  pl.BlockSpec((1, bn), lambda i, j: (0, nj + j))],
        out_specs=pl.BlockSpec((bm, bn), lambda i, j: (i, j)),
        out_shape=jax.ShapeDtypeStruct((t, D_FF), BF16),
        compiler_params=_params(("arbitrary", "arbitrary"), 60),
        name="ffn_up_conv",
    )(h, w_up, w_up, conv_w, conv_w, conv_b, conv_b)


def _s5_operators(lam_re, lam_im, log_dt, b_re, b_im, c_re, c_im, d_skip):
    q, g, p, c = S5_CHUNK, S5_GROUPS, S5_STATE, S5_GROUP
    dt = jnp.exp(log_dt)[None, :, :, None]
    tau = jnp.arange(q + 1, dtype=F32)[:, None, None, None]
    mag = jnp.exp(lam_re[None] * dt * tau)
    ang = lam_im[None] * dt * tau
    pw_re, pw_im = mag * jnp.cos(ang), mag * jnp.sin(ang)
    dt1 = jnp.exp(log_dt)[:, :, None]
    m1 = jnp.exp(lam_re * dt1)
    ab_re, ab_im = m1 * jnp.cos(lam_im * dt1), m1 * jnp.sin(lam_im * dt1)
    den = lam_re * lam_re + lam_im * lam_im
    nr, ni = ab_re - 1.0, ab_im
    f_re = ((nr * lam_re + ni * lam_im) / den)[..., None]
    f_im = ((ni * lam_re - nr * lam_im) / den)[..., None]
    bb_re = f_re * b_re - f_im * b_im
    bb_im = f_re * b_im + f_im * b_re
    ct_re, ct_im = jnp.swapaxes(c_re, -1, -2), jnp.swapaxes(c_im, -1, -2)

    def power_times_c(fwd_taus, bwd_taus):
        pr = jnp.stack([pw_re[fwd_taus, 0], pw_re[bwd_taus, 1]], axis=0).transpose(0, 2, 3, 1)
        pi = jnp.stack([pw_im[fwd_taus, 0], pw_im[bwd_taus, 1]], axis=0).transpose(0, 2, 3, 1)
        r_re = pr[..., None] * ct_re[:, :, :, None, :] - pi[..., None] * ct_im[:, :, :, None, :]
        r_im = pr[..., None] * ct_im[:, :, :, None, :] + pi[..., None] * ct_re[:, :, :, None, :]
        return r_re.reshape(2, g, p, q * c), r_im.reshape(2, g, p, q * c)

    lags = jnp.arange(q)
    rk_re, rk_im = power_times_c(lags, lags[::-1])
    diag = jnp.tile(d_skip[:, :, None] * jnp.eye(c, dtype=F32)[None], (1, 1, q))
    bbt_re, bbt_im = jnp.swapaxes(bb_re, -1, -2), jnp.swapaxes(bb_im, -1, -2)
    t_op = _s5_toeplitz(rk_re, rk_im, bbt_re, bbt_im, diag)
    re_re, re_im = power_times_c(lags + 1, q - lags)
    zero = jnp.zeros_like(re_re[0])
    e_op = jnp.concatenate([re_re[0], zero, -re_im[0], zero, zero, re_re[1], zero, -re_im[1]], axis=1)
    def s_op(taus, d):
        pr, pi = pw_re[taus, d][:, :, None, :], pw_im[taus, d][:, :, None, :]
        br, bi = bbt_re[d][None], bbt_im[d][None]
        s_re = (pr * br - pi * bi).transpose(1, 0, 2, 3).reshape(g, q * c, p)
        s_im = (pr * bi + pi * br).transpose(1, 0, 2, 3).reshape(g, q * c, p)
        return s_re, s_im

    sf_re, sf_im = s_op(lags[::-1], 0)
    sb_re, sb_im = s_op(lags, 1)
    sb = jnp.concatenate([sf_re, sb_re, sf_im, sb_im], axis=2)
    a16 = jnp.concatenate([pw_re[q, 0], pw_re[q, 1], pw_im[q, 0], pw_im[q, 1]], axis=-1)[:, None, :]
    return sb.astype(BF16), t_op, e_op.astype(BF16), a16


def _s5_toeplitz_kernel(rr_ref, ri_ref, br_ref, bi_ref, dg_ref, o_ref):
    c, q = S5_GROUP, S5_CHUNK
    lane = lax.broadcasted_iota(jnp.int32, (c, q * c), 1)

    def lag_kernels(d, gg):
        hp = lax.Precision.HIGHEST
        return (jnp.dot(br_ref[d, gg], rr_ref[d, gg], precision=hp, preferred_element_type=F32)
                - jnp.dot(bi_ref[d, gg], ri_ref[d, gg], precision=hp, preferred_element_type=F32))

    for gg in range(TOEPLITZ_GROUPS_PER_STEP):
        kf = lag_kernels(0, gg)
        kb = lag_kernels(1, gg)
        dg = dg_ref[gg]
        for s in range(q):
            f = kf if s == 0 else pltpu.roll(kf, c * s, 1)
            b = kb if s == q - 1 else pltpu.roll(kb, q * c - c * (q - 1 - s), 1)
            piece = jnp.where(lane >= c * s, f, 0.0) + jnp.where(lane < c * (s + 1), b, 0.0)
            piece = piece + jnp.where(jnp.logical_and(lane >= c * s, lane < c * (s + 1)), dg, 0.0)
            o_ref[gg, s * c:(s + 1) * c, :] = piece.astype(o_ref.dtype)


TOEPLITZ_GROUPS_PER_STEP = 4


def _s5_toeplitz(rk_re, rk_im, bbt_re, bbt_im, diag):
    g, p, c, qc = S5_GROUPS, S5_STATE, S5_GROUP, S5_CHUNK * S5_GROUP
    gs = TOEPLITZ_GROUPS_PER_STEP
    r_spec = pl.BlockSpec((2, gs, p, qc), lambda i: (0, i, 0, 0))
    b_spec = pl.BlockSpec((2, gs, c, p), lambda i: (0, i, 0, 0))
    return pl.pallas_call(
        _s5_toeplitz_kernel,
        grid=(g // gs,),
        in_specs=[r_spec, r_spec, b_spec, b_spec, pl.BlockSpec((gs, c, qc), lambda i: (i, 0, 0))],
        out_specs=pl.BlockSpec((gs, qc, qc), lambda i: (i, 0, 0)),
        out_shape=jax.ShapeDtypeStruct((g, qc, qc), BF16),
        compiler_params=_params(("arbitrary",), 32),
        name="s5_toeplitz",
    )(rk_re, rk_im, bbt_re, bbt_im, diag)


def _gelu_tanh(x):
    return 0.5 * x * (1.0 + jnp.tanh(math.sqrt(2.0 / math.pi) * (x + 0.044715 * (x * x * x))))


GROUPS_PER_STEP = LANE // S5_GROUP
STATE_W = GROUPS_PER_STEP * 2 * S5_STATE


def _s5_lane_permutation():
    j = jnp.arange(S5_CHUNK * LANE)
    s, gl, c = j // LANE, (j % LANE) // S5_GROUP, j % S5_GROUP
    k = gl * (S5_CHUNK * S5_GROUP) + s * S5_GROUP + c
    return (k[:, None] == jnp.arange(S5_CHUNK * LANE)[None, :]).astype(BF16)


def _s5_kernel(u_ref, sel_ref, sb_ref, t_ref, e_ref, a_ref, h0_ref, y_ref, hfin_ref, xg, w_scr, h_scr, *,
               batch, nb, nc, seq_len):
    qc = S5_CHUNK * S5_GROUP
    if batch < nb:
        xg[...] = jnp.zeros(xg.shape, BF16)
    for ch in range(nc):
        for s in range(S5_CHUNK):
            xg[ch * nb:ch * nb + batch, s * LANE:(s + 1) * LANE] = (
                u_ref[pl.ds(ch * S5_CHUNK + s, batch, stride=seq_len), :].astype(BF16))
    sel = sel_ref[...]
    ucat = _dot(xg[...], sel).astype(BF16)
    for gl in range(GROUPS_PER_STEP):
        w = _dot(ucat[:, gl * qc:(gl + 1) * qc], sb_ref[gl])
        w_scr[:, gl * LANE:(gl + 1) * LANE] = w[:, 0:LANE]
        w_scr[:, STATE_W + gl * LANE:STATE_W + (gl + 1) * LANE] = w[:, LANE:2 * LANE]
    ar = a_ref[0:1, 0:STATE_W]
    ai = a_ref[0:1, STATE_W:2 * STATE_W]
    fwd_lane = (lax.broadcasted_iota(jnp.int32, (1, STATE_W), 1) & S5_STATE) == 0
    hr = h0_ref[:, 0:STATE_W]
    hi = h0_ref[:, STATE_W:2 * STATE_W]
    for i in range(nc):
        ri = i * nb
        rj = (nc - 1 - i) * nb
        h_scr[ri:ri + nb, 0:STATE_W] = hr
        h_scr[ri:ri + nb, STATE_W:2 * STATE_W] = hi
        h_scr[rj:rj + nb, 2 * STATE_W:3 * STATE_W] = hr
        h_scr[rj:rj + nb, 3 * STATE_W:4 * STATE_W] = hi
        wr = jnp.where(fwd_lane, w_scr[ri:ri + nb, 0:STATE_W], w_scr[rj:rj + nb, 0:STATE_W])
        wi = jnp.where(fwd_lane, w_scr[ri:ri + nb, STATE_W:2 * STATE_W], w_scr[rj:rj + nb, STATE_W:2 * STATE_W])
        hr, hi = ar * hr - ai * hi + wr, ar * hi + ai * hr + wi
    hfin_ref[:, 0:STATE_W] = hr
    hfin_ref[:, STATE_W:2 * STATE_W] = hi
    for gl in range(GROUPS_PER_STEP):
        hcat = jnp.concatenate([h_scr[:, k * STATE_W + gl * LANE:k * STATE_W + (gl + 1) * LANE] for k in range(4)],
                               axis=1).astype(BF16)
        y = _dot(ucat[:, gl * qc:(gl + 1) * qc], t_ref[gl]) + _dot(hcat, e_ref[gl])
        xg[:, gl * qc:(gl + 1) * qc] = _gelu_tanh(y).astype(BF16)
    yp = _dot_t(xg[...], sel)
    for ch in range(nc):
        for s in range(S5_CHUNK):
            y_ref[pl.ds(ch * S5_CHUNK + s, batch, stride=seq_len), :] = (
                yp[ch * nb:ch * nb + batch, s * LANE:(s + 1) * LANE])


def _s5_chunk_rows(batch):
    return batch if batch % (SUBLANE // 2) == 0 else -(-batch // SUBLANE) * SUBLANE


def _s5_stream(u, row_block, batch, seq_len, sel, sb, t_op, e_op, a_planes, h0):
    n_rows = batch * seq_len
    nb = _s5_chunk_rows(batch)
    nc = seq_len // S5_CHUNK
    rows = nb * nc
    nblk = S5_WIDTH // LANE
    qc = S5_CHUNK * S5_GROUP
    body = functools.partial(_s5_kernel, batch=batch, nb=nb, nc=nc, seq_len=seq_len)
    return pl.pallas_call(
        body,
        grid=(nblk,),
        in_specs=[pl.BlockSpec((n_rows, LANE), lambda g: (row_block, g)),
                  pl.BlockSpec((S5_CHUNK * LANE, S5_CHUNK * LANE), lambda g: (0, 0), pipeline_mode=pl.Buffered(1)),
                  pl.BlockSpec((GROUPS_PER_STEP, qc, 4 * S5_STATE), lambda g: (g, 0, 0)),
                  pl.BlockSpec((GROUPS_PER_STEP, qc, qc), lambda g: (g, 0, 0)),
                  pl.BlockSpec((GROUPS_PER_STEP, 8 * S5_STATE, qc), lambda g: (g, 0, 0)),
                  pl.BlockSpec((None, 1, 2 * STATE_W), lambda g: (g, 0, 0)),
                  pl.BlockSpec((None, nb, 2 * STATE_W), lambda g: (g, 0, 0))],
        out_specs=[pl.BlockSpec((n_rows, LANE), lambda g: (0, g)),
                   pl.BlockSpec((None, nb, 2 * STATE_W), lambda g: (g, 0, 0))],
        out_shape=[jax.ShapeDtypeStruct((n_rows, S5_WIDTH), F32),
                   jax.ShapeDtypeStruct((nblk, nb, 2 * STATE_W), F32)],
        scratch_shapes=[pltpu.VMEM((rows, S5_CHUNK * LANE), BF16),
                        pltpu.VMEM((rows, 2 * STATE_W), F32),
                        pltpu.VMEM((rows, 4 * STATE_W), F32)],
        compiler_params=_params(("arbitrary",), 56),
        name="s5_chunked",
    )(u, sel, sb, t_op, e_op, a_planes, h0)


def _rope_tables(length, rot):
    n_freq = rot // 4
    t = jnp.arange(length)
    row = (t // GRID_W).astype(F32)
    col = (t % GRID_W).astype(F32)
    inv = ROPE_BASE ** (-jnp.arange(n_freq, dtype=F32) / n_freq)
    ar, ac = row[:, None] * inv, col[:, None] * inv
    cos = jnp.concatenate([jnp.cos(ar), jnp.cos(ar), jnp.cos(ac), jnp.cos(ac)], axis=-1)
    sin = jnp.concatenate([-jnp.sin(ar), jnp.sin(ar), -jnp.sin(ac), jnp.sin(ac)], axis=-1)
    pad = LANE - rot
    if pad:
        cos = jnp.concatenate([cos, jnp.ones((length, pad), F32)], axis=-1)
        sin = jnp.concatenate([sin, jnp.zeros((length, pad), F32)], axis=-1)
    return cos, sin


def _rope(x, cos, sin, blk):
    lane = lax.broadcasted_iota(jnp.int32, (1, LANE), 1)
    lower = (lane & blk) == 0
    partner = jnp.where(lower, pltpu.roll(x, LANE - blk, 1), pltpu.roll(x, blk, 1))
    return x * cos + partner * sin


def _softmax_pv(scores, values, extra_logit=None):
    m = scores[0].max(axis=-1, keepdims=True)
    for s in scores[1:]:
        m = jnp.maximum(m, s.max(axis=-1, keepdims=True))
    if extra_logit is not None:
        m = jnp.maximum(m, extra_logit)
    den = None
    out = None
    for s, v in zip(scores, values):
        p = jnp.exp(s - m)
        d = p.sum(axis=-1, keepdims=True)
        o = _dot(p.astype(BF16), v)
        den = d if den is None else den + d
        out = o if out is None else out + o
    if extra_logit is not None:
        den = den + jnp.exp(extra_logit - m)
    return out / den


def _ctx_gqa_kernel(sink_ref, q_ref, k_ref, v_ref, o_ref):
    scale = HEAD_DIM ** -0.5
    for g in range(WIN_KV_HEADS):
        k = k_ref[g].astype(BF16)
        v = v_ref[g].astype(BF16)
        for r in range(WIN_GROUP):
            cols = slice((g * WIN_GROUP + r) * HEAD_DIM, (g * WIN_GROUP + r + 1) * HEAD_DIM)
            s = _dot_t(q_ref[:, cols], k) * scale
            o = _softmax_pv([s], [v], sink_ref[g * WIN_GROUP + r])
            o_ref[:, cols] = o.astype(o_ref.dtype)


def _ctx_gqa_attention(uq3, k4, v4, sink, n_prompt, length):
    kv_spec = pl.BlockSpec((None, WIN_KV_HEADS, length, HEAD_DIM), lambda b: (b, 0, 0, 0))
    return pl.pallas_call(
        _ctx_gqa_kernel,
        grid=(n_prompt,),
        in_specs=[pl.BlockSpec(memory_space=pltpu.SMEM),
                  pl.BlockSpec((None, length, WIN_Q_WIDTH), lambda b: (b, 0, 0)),
                  kv_spec, kv_spec],
        out_specs=pl.BlockSpec((None, length, WIN_Q_WIDTH), lambda b: (b, 0, 0)),
        out_shape=jax.ShapeDtypeStruct((n_prompt, length, WIN_Q_WIDTH), BF16),
        compiler_params=_params(("arbitrary",), 40),
        name="ctx_gqa_attention",
    )(sink, uq3, k4, v4)


def _lat_window_kernel(sink_ref, q_ref, k_ref, v_ref, kc_ref, vc_ref, cos_ref, sin_ref, o_ref, *, length):
    g = pl.program_id(1)
    scale = HEAD_DIM ** -0.5
    blk = WIN_RADIUS
    nb = length // blk
    k = _rope(k_ref[...], cos_ref[...], sin_ref[...], HEAD_DIM // 4).astype(BF16)
    v = v_ref[...].astype(BF16)
    kc = kc_ref[...].astype(BF16)
    vc = vc_ref[...].astype(BF16)
    rows = WIN_GROUP * blk
    row = lax.broadcasted_iota(jnp.int32, (rows, 1), 0)
    sk = jnp.zeros((rows, 1), F32)
    for r in range(WIN_GROUP):
        sk = jnp.where(jnp.logical_and(row >= r * blk, row < (r + 1) * blk), sink_ref[g * WIN_GROUP + r], sk)
    qoff = row & (blk - 1)
    for n in range(nb):
        lo = max(0, n - 1) * blk
        hi = min(nb, n + 2) * blk
        cos = cos_ref[n * blk:(n + 1) * blk, :]
        sin = sin_ref[n * blk:(n + 1) * blk, :]
        q = jnp.concatenate(
            [_rope(q_ref[n * blk:(n + 1) * blk, r * HEAD_DIM:(r + 1) * HEAD_DIM].astype(F32), cos, sin, HEAD_DIM // 4)
             for r in range(WIN_GROUP)], axis=0).astype(BF16)
        dist = (n * blk + qoff) - (lo + lax.broadcasted_iota(jnp.int32, (1, hi - lo), 1))
        visible = jnp.logical_and(dist <= WIN_RADIUS, dist >= -WIN_RADIUS)
        s_loc = jnp.where(visible, _dot_t(q, k[lo:hi]) * scale, NEG)
        s_ctx = _dot_t(q, kc) * scale
        o = _softmax_pv([s_loc, s_ctx], [v[lo:hi], vc], sk)
        for r in range(WIN_GROUP):
            o_ref[n * blk:(n + 1) * blk, r * HEAD_DIM:(r + 1) * HEAD_DIM] = o[r * blk:(r + 1) * blk].astype(o_ref.dtype)


def _lat_window_attention(uq3, seq0, k4, v4, k_ctx, v_ctx, sink, cos, sin):
    n_lat, _, length, _ = k4.shape
    qw = WIN_GROUP * HEAD_DIM
    lc = k_ctx.shape[2]
    kv_spec = pl.BlockSpec((None, None, length, HEAD_DIM), lambda b, g: (b, g, 0, 0))
    ctx_spec = pl.BlockSpec((None, None, lc, HEAD_DIM), lambda b, g: (b, g, 0, 0))
    tab_spec = pl.BlockSpec((length, LANE), lambda b, g: (0, 0))
    return pl.pallas_call(
        functools.partial(_lat_window_kernel, length=length),
        grid=(n_lat, WIN_KV_HEADS),
        in_specs=[pl.BlockSpec(memory_space=pltpu.SMEM),
                  pl.BlockSpec((None, length, qw), lambda b, g: (seq0 + b, 0, g)),
                  kv_spec, kv_spec, ctx_spec, ctx_spec, tab_spec, tab_spec],
        out_specs=pl.BlockSpec((None, length, qw), lambda b, g: (b, 0, g)),
        out_shape=jax.ShapeDtypeStruct((n_lat, length, WIN_Q_WIDTH), BF16),
        compiler_params=_params(("arbitrary", "arbitrary"), 48),
        name="latent_window_attention",
    )(sink, uq3, k4, v4, k_ctx, v_ctx, cos, sin)


def _ctx_odd_kernel(q_ref, kv_ref, kpe_ref, qd_ref, kd_ref, vd_ref, o_ref):
    kw = MLA_NOPE + LANE
    kpe = kpe_ref[...].astype(BF16)
    scale_c = (MLA_NOPE + MLA_ROPE) ** -0.5
    for h in range(MLA_HEADS):
        q = q_ref[:, h * kw:(h + 1) * kw]
        k = jnp.concatenate([kv_ref[:, h * kw:h * kw + MLA_NOPE], kpe], axis=1)
        v = kv_ref[:, h * kw + MLA_NOPE:(h + 1) * kw]
        o = _softmax_pv([_dot_t(q, k) * scale_c], [v])
        o_ref[:, h * MLA_V:(h + 1) * MLA_V] = o.astype(o_ref.dtype)
    scale_d = HEAD_DIM ** -0.5
    base = MLA_HEADS * MLA_V
    for h in range(NA_HEADS):
        q = qd_ref[:, h * HEAD_DIM:(h + 1) * HEAD_DIM]
        o = _softmax_pv([_dot_t(q, kd_ref[h].astype(BF16)) * scale_d], [vd_ref[h].astype(BF16)])
        o_ref[:, base + h * HEAD_DIM:base + (h + 1) * HEAD_DIM] = o.astype(o_ref.dtype)


def _ctx_odd_attention(q3, kv3, kpe3, qd3, kd4, vd4, n_prompt, length):
    qn = q3.shape[-1]
    head_spec = pl.BlockSpec((None, NA_HEADS, length, HEAD_DIM), lambda b: (b, 0, 0, 0))
    return pl.pallas_call(
        _ctx_odd_kernel,
        grid=(n_prompt,),
        in_specs=[pl.BlockSpec((None, length, qn), lambda b: (b, 0, 0)),
                  pl.BlockSpec((None, length, qn), lambda b: (b, 0, 0)),
                  pl.BlockSpec((None, length, LANE), lambda b: (b, 0, 0)),
                  pl.BlockSpec((None, length, NA_WIDTH), lambda b: (b, 0, 0)),
                  head_spec, head_spec],
        out_specs=pl.BlockSpec((None, length, MLA_HEADS * MLA_V + NA_WIDTH), lambda b: (b, 0, 0)),
        out_shape=jax.ShapeDtypeStruct((n_prompt, length, MLA_HEADS * MLA_V + NA_WIDTH), BF16),
        compiler_params=_params(("arbitrary",), 48),
        name="ctx_odd_attention",
    )(q3, kv3, kpe3, qd3, kd4, vd4)


def _lat_mla_kernel(q_ref, kv_ref, kpe_ref, kvc_ref, kpec_ref, cos_ref, sin_ref, o_ref, *, length):
    scale = (MLA_NOPE + MLA_ROPE) ** -0.5
    rb = MLA_ROPE // 4
    hw = MLA_NOPE + LANE
    kpe = _rope(kpe_ref[...], cos_ref[...], sin_ref[...], rb).astype(BF16)
    kpe_ctx = kpec_ref[...].astype(BF16)
    qb = 256
    for hh in range(ATTN_HEADS_PER_STEP):
        c0 = hh * hw
        k_lat = jnp.concatenate([kv_ref[:, c0:c0 + MLA_NOPE], kpe], axis=1)
        v_lat = kv_ref[:, c0 + MLA_NOPE:c0 + hw]
        k_ctx = jnp.concatenate([kvc_ref[:, c0:c0 + MLA_NOPE], kpe_ctx], axis=1)
        v_ctx = kvc_ref[:, c0 + MLA_NOPE:c0 + hw]
        for n in range(length // qb):
            rows = slice(n * qb, (n + 1) * qb)
            q_pe = _rope(q_ref[rows, c0 + MLA_NOPE:c0 + hw].astype(F32), cos_ref[rows, :], sin_ref[rows, :],
                         rb).astype(BF16)
            q = jnp.concatenate([q_ref[rows, c0:c0 + MLA_NOPE], q_pe], axis=1)
            o = _softmax_pv([_dot_t(q, k_lat) * scale, _dot_t(q, k_ctx) * scale], [v_lat, v_ctx])
            o_ref[rows, hh * MLA_V:(hh + 1) * MLA_V] = o.astype(o_ref.dtype)


def _lat_mla_attention(q3, kv3, kvc3, kpe3, kpe_ctx, seq0, n_lat, length, cos, sin):
    kw = ATTN_HEADS_PER_STEP * (MLA_NOPE + LANE)
    lc = kpe_ctx.shape[1]
    return pl.pallas_call(
        functools.partial(_lat_mla_kernel, length=length),
        grid=(n_lat, MLA_HEADS // ATTN_HEADS_PER_STEP),
        in_specs=[pl.BlockSpec((None, length, kw), lambda b, h: (seq0 + b, 0, h)),
                  pl.BlockSpec((None, length, kw), lambda b, h: (seq0 + b, 0, h)),
                  pl.BlockSpec((None, length, LANE), lambda b, h: (seq0 + b, 0, 0)),
                  pl.BlockSpec((None, lc, kw), lambda b, h: (b, 0, h)),
                  pl.BlockSpec((None, lc, LANE), lambda b, h: (b, 0, 0)),
                  pl.BlockSpec((length, LANE), lambda b, h: (0, 0)),
                  pl.BlockSpec((length, LANE), lambda b, h: (0, 0))],
        out_specs=pl.BlockSpec((None, length, ATTN_HEADS_PER_STEP * MLA_V), lambda b, h: (b, 0, h)),
        out_shape=jax.ShapeDtypeStruct((n_lat, length, MLA_HEADS * MLA_V), BF16),
        compiler_params=_params(("arbitrary", "arbitrary"), 48),
        name="latent_mla_attention",
    )(q3, kv3, kpe3, kvc3, kpe_ctx, cos, sin)


def _na_row_start(r, rows):
    kr = min(NA_ROWS, rows)
    return min(max(r - kr // 2, 0), rows - kr)


def _lat_na_kernel(q_ref, k_ref, v_ref, kc_ref, vc_ref, bias_ref, o_ref, *, length):
    scale = HEAD_DIM ** -0.5
    rows = length // GRID_W
    kr = min(NA_ROWS, rows)
    for hh in range(ATTN_HEADS_PER_STEP):
        cols = slice(hh * HEAD_DIM, (hh + 1) * HEAD_DIM)
        k = k_ref[hh].astype(BF16)
        v = v_ref[hh].astype(BF16)
        kc = kc_ref[hh].astype(BF16)
        vc = vc_ref[hh].astype(BF16)
        r = 0
        while r < rows:
            r_end = r + 1
            while r_end < rows and _na_row_start(r_end, rows) == _na_row_start(r, rows):
                r_end += 1
            r0 = _na_row_start(r, rows) * GRID_W
            nq = (r_end - r) * GRID_W
            q = q_ref[r * GRID_W:r_end * GRID_W, cols]
            bias = bias_ref[hh, r:r_end].reshape(nq, kr * GRID_W)
            s_nb = _dot_t(q, k[r0:r0 + kr * GRID_W]) * scale + bias
            s_ctx = _dot_t(q, kc) * scale
            o = _softmax_pv([s_nb, s_ctx], [v[r0:r0 + kr * GRID_W], vc])
            o_ref[r * GRID_W:r_end * GRID_W, cols] = o.astype(o_ref.dtype)
            r = r_end


def _na_bias(rpb, length):
    rows = length // GRID_W
    kr = min(NA_ROWS, rows)
    col = jnp.arange(GRID_W)
    c_start = jnp.clip(col - NA_COLS // 2, 0, GRID_W - NA_COLS)
    col_valid = (col[None, :] >= c_start[:, None]) & (col[None, :] < c_start[:, None] + NA_COLS)
    off_c = jnp.clip(col[None, :] - col[:, None], -(NA_COLS - 1), NA_COLS - 1) + NA_COLS - 1
    onehot = (off_c[:, :, None] == jnp.arange(2 * NA_COLS - 1)[None, None, :]).astype(F32)
    table = jnp.einsum('hdj,qkj->hdqk', rpb.astype(F32), onehot, precision=lax.Precision.HIGHEST)
    table = jnp.where(col_valid[None, None], table, NEG)
    per_row = []
    for r in range(rows):
        r0 = _na_row_start(r, rows)
        per_row.append(jnp.concatenate([table[:, r0 + j - r + NA_ROWS - 1] for j in range(kr)], axis=-1))
    return jnp.stack(per_row, axis=1)


def _lat_na_attention(qd3, seq0, k4, v4, k_ctx, v_ctx, bias):
    n_lat, _, length, _ = k4.shape
    rows = length // GRID_W
    kr = min(NA_ROWS, rows)
    lc = k_ctx.shape[2]
    hp = ATTN_HEADS_PER_STEP
    kv_spec = pl.BlockSpec((None, hp, length, HEAD_DIM), lambda h, b: (b, h, 0, 0))
    ctx_spec = pl.BlockSpec((None, hp, lc, HEAD_DIM), lambda h, b: (b, h, 0, 0))
    return pl.pallas_call(
        functools.partial(_lat_na_kernel, length=length),
        grid=(NA_HEADS // hp, n_lat),
        in_specs=[pl.BlockSpec((None, length, hp * HEAD_DIM), lambda h, b: (seq0 + b, 0, h)),
                  kv_spec, kv_spec, ctx_spec, ctx_spec,
                  pl.BlockSpec((hp, rows, GRID_W, kr * GRID_W), lambda h, b: (h, 0, 0, 0))],
        out_specs=pl.BlockSpec((None, length, hp * HEAD_DIM), lambda h, b: (b, 0, h)),
        out_shape=jax.ShapeDtypeStruct((n_lat, length, NA_WIDTH), BF16),
        compiler_params=_params(("arbitrary", "arbitrary"), 48),
        name="latent_neighborhood_attention",
    )(qd3, k4, v4, k_ctx, v_ctx, bias)


def kernel(x_prompt, x_sample, cache_l0_k, cache_l0_v, state_l0_re, state_l0_im, cache_l1_ckv, cache_l1_kpe, cache_l1_k, cache_l1_v, c, c_ctx, l0_ada_w, l0_ada_b, l0_norm1, l0_norm2, l0_w_in, l0_lambda_re, l0_lambda_im, l0_log_dt, l0_b_re, l0_b_im, l0_c_re, l0_c_im, l0_d_skip, l0_w_glu, l0_b_glu, l0_sink, l0_w_out, l0_ffn_w_up, l0_ffn_conv_w, l0_ffn_conv_b, l0_ffn_w_down, l1_ada_w, l1_ada_b, l1_norm1, l1_norm2, l1_w_in, l1_q_norm, l1_kv_norm, l1_w_uq, l1_w_ukv, l1_rpb, l1_w_out, l1_ffn_w_up, l1_ffn_conv_w, l1_ffn_conv_b, l1_ffn_w_down, final_norm):
    n_p, len_p, d = x_prompt.shape
    n_s, len_s, _ = x_sample.shape
    rows_p, rows_s = n_p * len_p, n_s * len_s
    t = rows_p + rows_s
    assert d == D_MODEL and len_s == ROW_TILE and ROW_TILE % len_p == 0 and n_s + 1 <= MAX_CONDS
    assert rows_p % ROW_TILE == 0
    assert len_p % S5_CHUNK == 0 and len_s % (2 * WIN_RADIUS) == 0 and len_s % GRID_W == 0
    seqs_p_units = t // len_p
    seqs_s_units = t // len_s
    seq0_s = rows_p // len_s
    groups = dict(n_prompt_rows=rows_p, lat_len=len_s)
    half = D_MODEL // 2

    cond = jnp.zeros((MAX_CONDS, d), F32).at[0].set(c_ctx).at[1:1 + n_s].set(c)
    mod0 = _modulation(cond, l0_ada_w, l0_ada_b)
    mod1 = _modulation(cond, l1_ada_w, l1_ada_b)

    x_parts = [x_prompt.reshape(rows_p, d), x_sample.reshape(rows_s, d)]

    def ffn(x, mod, norm2, w_up, conv_w, conv_b, w_down):
        h = _norm_mod([x], norm2, mod, 3, 4, **groups)
        act = _ffn_up(h, w_up, conv_w, conv_b, rows_p, len_p, len_s)
        return _matmul_residual([[(act, 0)]], w_down.astype(BF16), [x], mod, 5, 512, name="ffn_down",
                                bm=512, **groups)

    h = _norm_mod(x_parts, l0_norm1, mod0, 0, 1, **groups)
    u = _matmul(h, l0_w_in, F32, 512, "l0_in_proj_u", 0, S5_WIDTH)
    q0 = _matmul(h, l0_w_in, BF16, 512, "l0_in_proj_q", S5_WIDTH, WIN_Q_WIDTH)
    kcol = S5_WIDTH + WIN_Q_WIDTH
    vcol = kcol + WIN_KV_WIDTH
    out_k0 = _matmul_heads(h, l0_w_in, kcol, WIN_KV_HEADS, 0, rows_p, len_p, "l0_in_proj_k_ctx")
    out_v0 = _matmul_heads(h, l0_w_in, vcol, WIN_KV_HEADS, 0, rows_p, len_p, "l0_in_proj_v_ctx")
    k0_s = _matmul_heads(h, l0_w_in, kcol, WIN_KV_HEADS, rows_p, rows_s, len_s, "l0_in_proj_k_lat")
    v0_s = _matmul_heads(h, l0_w_in, vcol, WIN_KV_HEADS, rows_p, rows_s, len_s, "l0_in_proj_v_lat")

    assert rows_p % rows_s == 0
    sel = _s5_lane_permutation()
    sb, t_op, e_op, a16 = _s5_operators(l0_lambda_re, l0_lambda_im, l0_log_dt, l0_b_re, l0_b_im, l0_c_re, l0_c_im, l0_d_skip)
    nblk, gps, pw = S5_WIDTH // LANE, GROUPS_PER_STEP, 2 * S5_STATE

    def planes(cols):
        b = cols.shape[1]
        c5 = cols.reshape(nblk, gps, b, 2, pw).transpose(0, 2, 3, 1, 4)
        return c5.reshape(nblk, b, 2 * STATE_W)

    def state_cols(s):
        return s.astype(F32).transpose(2, 0, 1, 3).reshape(S5_GROUPS, s.shape[0], pw)

    np8, ns8 = _s5_chunk_rows(n_p), _s5_chunk_rows(n_s)
    a_planes = planes(a16)
    h0_lat = planes(jnp.concatenate([state_cols(state_l0_re), state_cols(state_l0_im)], axis=-1))
    h0_s = jnp.pad(h0_lat, ((0, 0), (0, ns8 - n_s), (0, 0)))
    h0_p = jnp.zeros((nblk, np8, 2 * STATE_W), F32)
    y_p, hfin = _s5_stream(u, 0, n_p, len_p, sel, sb, t_op, e_op, a_planes, h0_p)
    y_s, _ = _s5_stream(u, rows_p // rows_s, n_s, len_s, sel, sb, t_op, e_op, a_planes, h0_s)
    a_out = _glu([y_p, y_s], l0_w_glu, l0_b_glu, rows_p)

    def state_out(plane):
        return (plane[:, :n_p].reshape(nblk, n_p, gps, 2, S5_STATE).transpose(1, 3, 0, 2, 4)
                .reshape(n_p, 2, S5_GROUPS, S5_STATE))

    out_sre = state_out(hfin[:, :, :STATE_W])
    out_sim = state_out(hfin[:, :, STATE_W:])

    cos_e, sin_e = _rope_tables(len_s, HEAD_DIM)
    o_p = _ctx_gqa_attention(q0.reshape(seqs_p_units, len_p, -1), out_k0, out_v0, l0_sink, n_p, len_p)
    o_s = _lat_window_attention(q0.reshape(seqs_s_units, len_s, -1), seq0_s, k0_s, v0_s,
                                cache_l0_k, cache_l0_v, l0_sink, cos_e, sin_e)
    x = _matmul_residual([[(a_out, 0)], [(o_p.reshape(rows_p, half), 0), (o_s.reshape(rows_s, half), 0)]],
                         l0_w_out.astype(BF16), x_parts, mod0, 2, 1024, name="l0_out_proj", bm=512, **groups)
    x = ffn(x, mod0, l0_norm2, l0_ffn_w_up, l0_ffn_conv_w, l0_ffn_conv_b, l0_ffn_w_down)

    s1, s2 = MLA_Q_LORA + MLA_KV_LORA, MLA_Q_LORA + MLA_KV_LORA + MLA_ROPE
    w_lora = l1_w_in[:, :s1].astype(BF16)
    w_dkv = l1_w_in[:, s2:].astype(BF16)
    w_kpe = jnp.pad(l1_w_in[:, s1:s2], ((0, 0), (0, LANE - MLA_ROPE))).astype(BF16)
    w_uq = l1_w_uq.reshape(MLA_Q_LORA, MLA_HEADS, MLA_NOPE + MLA_ROPE)
    w_uq = jnp.pad(w_uq, ((0, 0), (0, 0), (0, LANE - MLA_ROPE))).reshape(MLA_Q_LORA, -1).astype(BF16)

    h = _norm_mod([x], l1_norm1, mod1, 0, 1, **groups)
    cqkv = _matmul(h, w_lora, F32, 512, "l1_in_proj_lora")
    kpe = _matmul(h, w_kpe, F32, LANE, "l1_in_proj_kpe")
    qd = _matmul(h, w_dkv, BF16, 512, "l1_in_proj_qd", 0, NA_WIDTH)
    out_k1 = _matmul_heads(h, w_dkv, NA_WIDTH, NA_HEADS, 0, rows_p, len_p, "l1_in_proj_kd_ctx")
    out_v1 = _matmul_heads(h, w_dkv, 2 * NA_WIDTH, NA_HEADS, 0, rows_p, len_p, "l1_in_proj_vd_ctx")
    k1_s = _matmul_heads(h, w_dkv, NA_WIDTH, NA_HEADS, rows_p, rows_s, len_s, "l1_in_proj_kd_lat")
    v1_s = _matmul_heads(h, w_dkv, 2 * NA_WIDTH, NA_HEADS, rows_p, rows_s, len_s, "l1_in_proj_vd_lat")
    cqn = _rmsnorm_cols(cqkv, l1_q_norm, 0, MLA_Q_LORA, BF16)
    ckvn = _rmsnorm_cols(cqkv, l1_kv_norm, MLA_Q_LORA // MLA_KV_LORA, MLA_KV_LORA, F32)
    q_all = _matmul(cqn, w_uq, BF16, 2048, "l1_q_up")
    w_ukv = l1_w_ukv.astype(BF16)
    kv_all = _matmul(ckvn, w_ukv, BF16, 2048, "l1_kv_up")
    lc = cache_l1_ckv.shape[1]
    kv_ctx = _matmul(cache_l1_ckv.reshape(-1, MLA_KV_LORA), w_ukv, BF16, 2048, "l1_kv_up_ctx")

    out_ckv = ckvn[:rows_p].reshape(n_p, len_p, MLA_KV_LORA)
    out_kpe = kpe[:rows_p, :MLA_ROPE].reshape(n_p, len_p, MLA_ROPE)

    o_p = _ctx_odd_attention(q_all.reshape(seqs_p_units, len_p, -1),
                             kv_all.reshape(seqs_p_units, len_p, -1),
                             kpe.reshape(seqs_p_units, len_p, LANE),
                             qd.reshape(seqs_p_units, len_p, -1), out_k1, out_v1, n_p, len_p)
    cos_o, sin_o = _rope_tables(len_s, MLA_ROPE)
    kpe_ctx = jnp.pad(cache_l1_kpe, ((0, 0), (0, 0), (0, LANE - MLA_ROPE)))
    oc_s = _lat_mla_attention(q_all.reshape(seqs_s_units, len_s, -1),
                              kv_all.reshape(seqs_s_units, len_s, -1),
                              kv_ctx.reshape(n_s, lc, -1),
                              kpe.reshape(seqs_s_units, len_s, LANE), kpe_ctx,
                              seq0_s, n_s, len_s, cos_o, sin_o)
    od_s = _lat_na_attention(qd.reshape(seqs_s_units, len_s, -1), seq0_s, k1_s, v1_s,
                             cache_l1_k, cache_l1_v, _na_bias(l1_rpb, len_s))
    o_p2 = o_p.reshape(rows_p, 2 * half)
    x = _matmul_residual([[(o_p2, 0), (oc_s.reshape(rows_s, half), 0)],
                          [(o_p2, 1), (od_s.reshape(rows_s, half), 0)]],
                         l1_w_out.astype(BF16), [x], mod1, 2, 1024, name="l1_out_proj", bm=512, **groups)
    x = ffn(x, mod1, l1_norm2, l1_ffn_w_up, l1_ffn_conv_w, l1_ffn_conv_b, l1_ffn_w_down)

    y_prompt = _rmsnorm_cols(x, final_norm, 0, d, F32, 0, rows_p).reshape(n_p, len_p, d)
    y_sample = _rmsnorm_cols(x, final_norm, 0, d, F32, rows_p, rows_s).reshape(n_s, len_s, d)
    return (y_prompt, y_sample, out_k0, out_v0, out_sre, out_sim, out_ckv, out_kpe, out_k1, out_v1)
```
